```python
import math
import numpy as np
import jax
import jax.numpy as jnp
from jax import lax

D_MODEL = 1024
BATCH = 16
SEQ = 2048
DEPTH = 2

CTX_LEN = 256
GRID_W = 64
ROPE_BASE = 10000.0
EPS = 1e-6

DA_HEADS = 4
DA_DIM = 64
DA_VDIM = 2 * DA_DIM
Q_BLOCK = 128
SW_HEADS = 8
SW_KV_HEADS = 2
SW_GROUP = SW_HEADS // SW_KV_HEADS
SW_DIM = 64
WINDOW = 128
RT_HEADS = 4
RT_KDIM = 64
RT_VDIM = 128
RT_CHUNK = 128
BRANCH_W = 512
N_BRANCH = 3
SPLIT_SIZES = (
    DA_HEADS * 2 * DA_DIM, DA_HEADS * 2 * DA_DIM, DA_HEADS * DA_VDIM,
    SW_HEADS * SW_DIM, SW_KV_HEADS * SW_DIM, SW_KV_HEADS * SW_DIM,
    RT_HEADS * RT_KDIM, RT_HEADS * RT_KDIM, RT_HEADS * RT_VDIM,
    RT_HEADS * RT_VDIM,
    N_BRANCH * D_MODEL,
)
D_IN = sum(SPLIT_SIZES)
N_EXPERTS = 16
N_GROUPS = 4
EXPERTS_PER_GROUP = N_EXPERTS // N_GROUPS
TOP_K = 2
D_EXPERT = 512

kernel_name = "hybrid_diffusion_trunk_diffattn_swa_retnet_groupmoe"


def rms_norm(x, g):
    xf = x.astype(jnp.float32)
    y = xf * lax.rsqrt(jnp.mean(xf * xf, axis=-1, keepdims=True) + EPS)
    return (y * g.astype(jnp.float32)).astype(x.dtype)


def group_norm_heads(o, g):
    of = o.astype(jnp.float32)
    mu = jnp.mean(of, axis=-1, keepdims=True)
    var = jnp.mean(jnp.square(of - mu), axis=-1, keepdims=True)
    y = ((of - mu) * lax.rsqrt(var + EPS)).reshape(o.shape[0], o.shape[1], -1)
    return (y * g.astype(jnp.float32)).astype(o.dtype)


def adaln(cond, w, b):
    m = jax.nn.silu(cond) @ w + b
    return jnp.split(m[..., None, :], 6, axis=-1)


def modulate(h, shift, scale):
    return h * (1 + scale) + shift


def axial_rope(n_tokens, dim):
    n_rows = n_tokens // GRID_W
    quarter = dim // 4
    inv = ROPE_BASE ** (-jnp.arange(quarter, dtype=jnp.float32) / quarter)
    rows = jnp.repeat(jnp.arange(n_rows, dtype=jnp.float32), GRID_W)
    cols = jnp.tile(jnp.arange(GRID_W, dtype=jnp.float32), n_rows)
    ang = jnp.concatenate([rows[:, None] * inv, cols[:, None] * inv], axis=-1)
    return jnp.cos(ang), jnp.sin(ang)


def apply_rope(x, cos, sin):
    half = x.shape[-1] // 2
    shape = (cos.shape[0],) + (1,) * (x.ndim - 3) + (half,)
    cos, sin = cos.reshape(shape), sin.reshape(shape)
    x1, x2 = x[..., :half], x[..., half:]
    return jnp.concatenate([x1 * cos - x2 * sin, x2 * cos + x1 * sin], axis=-1).astype(x.dtype)


def split_heads(p):
    b, n = p.shape[:2]
    idx = np.cumsum(SPLIT_SIZES)[:-1].tolist()
    aq, ak, av, bq, bk, bv, cq, ck, cv, cg, gates = jnp.split(p, idx, axis=-1)
    return (aq.reshape(b, n, DA_HEADS, 2, DA_DIM), ak.reshape(b, n, DA_HEADS, 2, DA_DIM),
            av.reshape(b, n, DA_HEADS, DA_VDIM),
            bq.reshape(b, n, SW_HEADS, SW_DIM), bk.reshape(b, n, SW_KV_HEADS, SW_DIM),
            bv.reshape(b, n, SW_KV_HEADS, SW_DIM),
            cq.reshape(b, n, RT_HEADS, RT_KDIM), ck.reshape(b, n, RT_HEADS, RT_KDIM),
            cv.reshape(b, n, RT_HEADS, RT_VDIM), cg, gates)


def diff_softmax(q, k, v, lam):
    s = jnp.einsum('bqhmd,bkhmd->bhmqk', q, k).astype(jnp.float32) * (DA_DIM ** -0.5)
    p = jax.nn.softmax(s, axis=-1)
    a = p[:, :, 0] - lam * p[:, :, 1]
    return jnp.einsum('bhqk,bkhe->bqhe', a.astype(v.dtype), v)


def diff_attention_latent(q, k, v, kc, vc, lam):
    b, n = q.shape[:2]
    k_all = jnp.concatenate([kc, k], axis=1)
    v_all = jnp.concatenate([vc, v], axis=1)
    qb = jnp.moveaxis(q.reshape(b, n // Q_BLOCK, Q_BLOCK, DA_HEADS, 2, DA_DIM), 1, 0)
    o = lax.map(lambda blk: diff_softmax(blk, k_all, v_all, lam), qb)
    return jnp.moveaxis(o, 0, 1).reshape(b, n, DA_HEADS, DA_VDIM)


def diff_out(o, g, lam_init):
    o = rms_norm(o, g) * (1 - lam_init)
    return o.reshape(o.shape[0], o.shape[1], -1)


def window_attention_latent(q, k, v, kc, vc, sink):
    b, n = q.shape[:2]
    nb = n // WINDOW
    scale = SW_DIM ** -0.5
    qb = q.reshape(b, nb, WINDOW, SW_KV_HEADS, SW_GROUP, SW_DIM)
    pad = ((0, 0), (WINDOW, WINDOW), (0, 0), (0, 0))
    kp = jnp.pad(k, pad).reshape(b, nb + 2, WINDOW, SW_KV_HEADS, SW_DIM)
    vp = jnp.pad(v, pad).reshape(b, nb + 2, WINDOW, SW_KV_HEADS, SW_DIM)
    kb = jnp.concatenate([kp[:, :-2], kp[:, 1:-1], kp[:, 2:]], axis=2)
    vb = jnp.concatenate([vp[:, :-2], vp[:, 1:-1], vp[:, 2:]], axis=2)
    s_win = jnp.einsum('bnqkgd,bnskd->bnkgqs', qb, kb).astype(jnp.float32) * scale
    qpos = jnp.arange(WINDOW)[:, None]
    spos = jnp.arange(3 * WINDOW)[None, :]
    abs_k = jnp.arange(nb)[:, None, None] * WINDOW - WINDOW + spos
    valid = (jnp.abs(spos - WINDOW - qpos) <= WINDOW)[None] & (abs_k >= 0) & (abs_k < n)
    s_win = jnp.where(valid[None, :, None, None], s_win, -jnp.inf)
    s_ctx = jnp.einsum('bnqkgd,bckd->bnkgqc', qb, kc).astype(jnp.float32) * scale
    s_sink = jnp.broadcast_to(sink.astype(jnp.float32).reshape(1, 1, SW_KV_HEADS, SW_GROUP, 1, 1),
                              s_win.shape[:-1] + (1,))
    p = jax.nn.softmax(jnp.concatenate([s_sink, s_ctx, s_win], axis=-1), axis=-1)
    n_ctx = kc.shape[1]
    p_ctx = p[..., 1:1 + n_ctx].astype(v.dtype)
    p_win = p[..., 1 + n_ctx:].astype(v.dtype)
    o = (jnp.einsum('bnkgqc,bckd->bnqkgd', p_ctx, vc)
         + jnp.einsum('bnkgqs,bnskd->bnqkgd', p_win, vb))
    return o.reshape(b, n, SW_HEADS * SW_DIM)


def window_attention_context(qc, kc, vc, sink):
    b, n = qc.shape[:2]
    qg = qc.reshape(b, n, SW_KV_HEADS, SW_GROUP, SW_DIM)
    s = jnp.einsum('bqkgd,bckd->bkgqc', qg, kc).astype(jnp.float32) * (SW_DIM ** -0.5)
    s_sink = jnp.broadcast_to(sink.astype(jnp.float32).reshape(1, SW_KV_HEADS, SW_GROUP, 1, 1),
                              s.shape[:-1] + (1,))
    p = jax.nn.softmax(jnp.concatenate([s_sink, s], axis=-1), axis=-1)[..., 1:]
    o = jnp.einsum('bkgqc,bckd->bqkgd', p.astype(vc.dtype), vc)
    return o.reshape(b, n, SW_HEADS * SW_DIM)


def retention_scan(q, k, v, log_gamma, state0):
    b, n = q.shape[:2]
    nc = n // RT_CHUNK
    lg = log_gamma.astype(jnp.float32)
    pos = jnp.arange(RT_CHUNK, dtype=jnp.float32)
    diff = pos[:, None] - pos[None, :]
    decay = jnp.where(diff >= 0, jnp.exp(jnp.maximum(diff, 0.0)[None] * lg[:, None, None]), 0.0)
    q_decay = jnp.exp((pos + 1)[:, None] * lg)[..., None]
    k_decay = jnp.exp((RT_CHUNK - 1 - pos)[:, None] * lg)[..., None]
    chunk_decay = jnp.exp(RT_CHUNK * lg)[:, None, None]

    def step(state, xs):
        qc, kc, vc = xs
        inner = jnp.einsum('bqhd,bkhd->bhqk', qc, kc) * decay
        o = (jnp.einsum('bhqk,bkhe->bqhe', inner, vc)
             + jnp.einsum('bqhd,bhde->bqhe', qc * q_decay, state))
        state = state * chunk_decay + jnp.einsum('bkhd,bkhe->bhde', kc * k_decay, vc)
        return state, o

    xs = tuple(jnp.swapaxes(t.astype(jnp.float32).reshape(b, nc, RT_CHUNK, RT_HEADS, -1), 0, 1)
               for t in (q, k * (RT_KDIM ** -0.5), v))
    state, o = lax.scan(step, state0, xs)
    return jnp.swapaxes(o, 0, 1).reshape(b, n, RT_HEADS, RT_VDIM).astype(v.dtype), state


def bi_retention(q, k, v, qc, kc, vc, log_gamma):
    zero = jnp.zeros((q.shape[0], RT_HEADS, RT_KDIM, RT_VDIM), jnp.float32)
    flip = lambda t: jnp.flip(t, axis=1)
    oc_f, st_f = retention_scan(qc, kc, vc, log_gamma[0], zero)
    oc_b, st_b = retention_scan(flip(qc), flip(kc), flip(vc), log_gamma[1], zero)
    ox_f, _ = retention_scan(q, k, v, log_gamma[0], st_f)
    ox_b, _ = retention_scan(flip(q), flip(k), flip(v), log_gamma[1], st_b)
    return ox_f + flip(ox_b), oc_f + flip(oc_b)


def retention_out(o, gate, gn_g):
    return group_norm_heads(o, gn_g) * jax.nn.silu(gate)


def merge_branches(oa, ob, oc, gate_logits, w_branch, w_out):
    ga, gb, gc = jnp.split(jax.nn.sigmoid(gate_logits), N_BRANCH, axis=-1)
    y = ga * (oa @ w_branch[0]) + gb * (ob @ w_branch[1]) + gc * (oc @ w_branch[2])
    return y @ w_out


def token_mixers(hx, hc, cos, sin, w_in, lam_vec, lam_init, subln_g, sink, ret_decay, ret_gn_g,
                 w_branch, w_out, ctx_out):
    aq, ak, av, bq, bk, bv, cq, ck, cv, cg, gates = split_heads(hx @ w_in)
    aqc, akc, avc, bqc, bkc, bvc, cqc, ckc, cvc, cgc, gates_c = split_heads(hc @ w_in)
    aq, ak, bq, bk = (apply_rope(t, cos, sin) for t in (aq, ak, bq, bk))
    lv = lam_vec.astype(jnp.float32)
    lam = jnp.exp(jnp.sum(lv[0] * lv[1])) - jnp.exp(jnp.sum(lv[2] * lv[3])) + lam_init
    log_gamma = jax.nn.log_sigmoid(ret_decay.astype(jnp.float32))
    ret_x, ret_c = bi_retention(cq, ck, cv, cqc, ckc, cvc, log_gamma)
    y_x = merge_branches(diff_out(diff_attention_latent(aq, ak, av, akc, avc, lam), subln_g, lam_init),
                         window_attention_latent(bq, bk, bv, bkc, bvc, sink),
                         retention_out(ret_x, cg, ret_gn_g), gates, w_branch, w_out)
    if not ctx_out:
        return y_x, None
    y_c = merge_branches(diff_out(diff_softmax(aqc, akc, avc, lam), subln_g, lam_init),
                         window_attention_context(bqc, bkc, bvc, sink),
                         retention_out(ret_c, cgc, ret_gn_g), gates_c, w_branch, w_out)
    return y_x, y_c


def grouped_moe(h, router_w, router_b, w_gate, w_up, w_down):
    scores = jax.nn.sigmoid((h @ router_w).astype(jnp.float32))
    biased = scores + router_b.astype(jnp.float32)
    grouped = biased.reshape(biased.shape[:-1] + (N_GROUPS, EXPERTS_PER_GROUP))
    group_score = lax.top_k(grouped, TOP_K)[0].sum(-1)
    best = jnp.argmax(group_score, axis=-1)
    in_group = (best[..., None] == jnp.arange(N_GROUPS))[..., None]
    masked = jnp.where(in_group, grouped, -jnp.inf).reshape(biased.shape)
    _, idx = lax.top_k(masked, TOP_K)
    w = jnp.take_along_axis(scores, idx, axis=-1)
    w = w / jnp.sum(w, axis=-1, keepdims=True)
    combine = jnp.einsum('bnk,bnke->bne', w, jax.nn.one_hot(idx, N_EXPERTS, dtype=jnp.float32)).astype(h.dtype)
    out = jnp.zeros_like(h)
    for e in range(N_EXPERTS):
        he = jax.nn.silu(h @ w_gate[e]) * (h @ w_up[e])
        out = out + combine[..., e:e + 1] * (he @ w_down[e])
    return out


def setup_inputs(seed: int = 0) -> dict:
    key = jax.random.key(seed)
    ks = jax.random.split(key, 24)
    nrm = lambda k, shape, s: jax.random.normal(k, shape, jnp.float32) * s
    L, D, E, F = DEPTH, D_MODEL, N_EXPERTS, D_EXPERT
    gamma_logit = jnp.log(2.0 ** (5.0 + jnp.arange(RT_HEADS, dtype=jnp.float32)) - 1.0)
    return {
        "x": nrm(ks[0], (BATCH, SEQ, D), 1.0),
        "c": nrm(ks[1], (BATCH, D), 1.0),
        "ctx": nrm(ks[2], (BATCH, CTX_LEN, D), 1.0),
        "c_ctx": nrm(ks[3], (D,), 1.0),
        "w_ada": nrm(ks[4], (L, D, 6 * D), 0.5 * D ** -0.5),
        "b_ada": nrm(ks[5], (L, 6 * D), 0.01),
        "norm_mix_g": 1.0 + nrm(ks[6], (L, D), 0.05),
        "norm_ffn_g": 1.0 + nrm(ks[7], (L, D), 0.05),
        "w_in": nrm(ks[8], (L, D, D_IN), D ** -0.5),
        "diff_lambda": nrm(ks[9], (L, 4, DA_DIM), 0.1),
        "diff_subln_g": 1.0 + nrm(ks[10], (L, DA_VDIM), 0.05),
        "swa_sink": nrm(ks[11], (L, SW_HEADS), 0.5),
        "ret_decay": gamma_logit + nrm(ks[12], (L, 2, RT_HEADS), 0.1),
        "ret_gn_g": 1.0 + nrm(ks[13], (L, RT_HEADS * RT_VDIM), 0.05),
        "w_branch": nrm(ks[14], (L, N_BRANCH, BRANCH_W, D), BRANCH_W ** -0.5),
        "w_out": nrm(ks[15], (L, D, D), D ** -0.5),
        "router_w": nrm(ks[16], (D, E), D ** -0.5),
        "router_b": nrm(ks[17], (E,), 0.01),
        "w_exp_gate": nrm(ks[18], (L, E, D, F), D ** -0.5),
        "w_exp_up": nrm(ks[19], (L, E, D, F), D ** -0.5),
        "w_exp_down": nrm(ks[20], (L, E, F, D), F ** -0.5),
        "final_g": 1.0 + nrm(ks[21], (D,), 0.05),
    }


def reference(x, c, ctx, c_ctx, w_ada, b_ada, norm_mix_g, norm_ffn_g, w_in, diff_lambda, diff_subln_g,
              swa_sink, ret_decay, ret_gn_g, w_branch, w_out, router_w, router_b, w_exp_gate, w_exp_up,
              w_exp_down, final_g):
    cos, sin = axial_rope(x.shape[1], DA_DIM)
    for l in range(DEPTH):
        ctx_out = l < DEPTH - 1
        lam_init = 0.8 - 0.6 * math.exp(-0.3 * l)
        sh1, sc1, g1, sh2, sc2, g2 = adaln(c, w_ada[l], b_ada[l])
        csh1, csc1, cg1, csh2, csc2, cg2 = adaln(c_ctx, w_ada[l], b_ada[l])
        hx = modulate(rms_norm(x, norm_mix_g[l]), sh1, sc1)
        hc = modulate(rms_norm(ctx, norm_mix_g[l]), csh1, csc1)
        y_x, y_c = token_mixers(hx, hc, cos, sin, w_in[l], diff_lambda[l], lam_init, diff_subln_g[l],
                                swa_sink[l], ret_decay[l], ret_gn_g[l], w_branch[l], w_out[l], ctx_out)
        x = x + g1 * y_x
        hx2 = modulate(rms_norm(x, norm_ffn_g[l]), sh2, sc2)
        x = x + g2 * grouped_moe(hx2, router_w, router_b, w_exp_gate[l], w_exp_up[l], w_exp_down[l])
        if ctx_out:
            ctx = ctx + cg1 * y_c
            hc2 = modulate(rms_norm(ctx, norm_ffn_g[l]), csh2, csc2)
            ctx = ctx + cg2 * grouped_moe(hc2, router_w, router_b, w_exp_gate[l], w_exp_up[l], w_exp_down[l])
    return rms_norm(x, final_g)
```

```python
import functools
import math

import jax
import jax.numpy as jnp
from jax import lax
from jax.experimental import pallas as pl
from jax.experimental.pallas import tpu as pltpu

F32 = jnp.float32
BF16 = jnp.bfloat16
U32 = jnp.uint32
I32 = jnp.int32

D_MODEL = 1024
CTX_LEN = 256
GRID_W = 64
ROPE_BASE = 10000.0
EPS = 1e-6
HEAD_DIM = 64
WINDOW = 128
RT_CHUNK = 128
N_EXPERTS = 16
N_GROUPS = 4
D_EXPERT = 512
LANES = 128
TM = 256
TMF = 512
D_IN = 6912
N_SRC_BLK = D_IN // LANES
N_OUT_BLK = N_SRC_BLK + 2
VMEM_LIMIT = 56 * 1024 * 1024

BLK_AQ, BLK_AK, BLK_AV = 0, 4, 8
BLK_BQ, BLK_BK, BLK_BV = 12, 16, 18
BLK_CQ, BLK_CK, BLK_CV, BLK_CG, BLK_GATE = 20, 22, 24, 28, 32


def _cparams(sem):
    return pltpu.CompilerParams(dimension_semantics=sem, vmem_limit_bytes=VMEM_LIMIT)


def _ada_kernel(c_ref, w_ref, b_ref, o_ref):
    c = c_ref[...]
    s = (c * jax.nn.sigmoid(c)).astype(BF16)
    o_ref[0] = jnp.dot(s, w_ref[0].astype(BF16), preferred_element_type=F32) + b_ref[0]


def _ada(c_all, w_ada, b_ada):
    depth = w_ada.shape[0]
    rows = c_all.shape[0]
    nj = w_ada.shape[2] // D_MODEL
    return pl.pallas_call(
        _ada_kernel,
        grid=(depth, nj),
        in_specs=[
            pl.BlockSpec((rows, D_MODEL), lambda l, j: (0, 0)),
            pl.BlockSpec((1, D_MODEL, D_MODEL), lambda l, j: (l, 0, j)),
            pl.BlockSpec((1, 1, D_MODEL), lambda l, j: (l, 0, j)),
        ],
        out_specs=pl.BlockSpec((1, rows, D_MODEL), lambda l, j: (l, 0, j)),
        out_shape=jax.ShapeDtypeStruct((depth, rows, w_ada.shape[2]), F32),
        compiler_params=_cparams(("arbitrary", "arbitrary")),
    )(c_all, w_ada, b_ada.reshape(depth, 1, -1))


def _rope(a, cos, sin_signed, first_half):
    rot = jnp.where(first_half, pltpu.roll(a, LANES - HEAD_DIM // 2, 1), pltpu.roll(a, HEAD_DIM // 2, 1))
    return a * cos + rot * sin_signed


def _proj_kernel(x_ref, sh_ref, sc_ref, g_ref, cos_ref, sin_ref, w_ref, o_ref):
    x = x_ref[0]
    ms = jnp.mean(x * x, axis=-1, keepdims=True)
    h = x * lax.rsqrt(ms + EPS) * g_ref[...]
    h = h * (1.0 + sc_ref[0]) + sh_ref[0]
    hb = h.astype(BF16)
    cos = cos_ref[...]
    sin = sin_ref[...]
    lane = lax.broadcasted_iota(I32, (1, LANES), 1)
    first_half = (lane & (HEAD_DIM - 1)) < (HEAD_DIM // 2)
    low = lane < HEAD_DIM
    scale = HEAD_DIM ** -0.5
    for c in range(N_SRC_BLK // 2):
        acc = jnp.dot(hb, w_ref[:, c * 2 * LANES:(c + 1) * 2 * LANES], preferred_element_type=F32)
        for half in range(2):
            src = 2 * c + half
            a = acc[:, half * LANES:(half + 1) * LANES]
            if src < 8 or 12 <= src < 17:
                a = _rope(a, cos, sin, first_half)
            if src < 4 or 12 <= src < 16 or 20 <= src < 22:
                a = a * scale
            if src in (16, 17):
                sw = pltpu.roll(a, HEAD_DIM, 1)
                dst = BLK_BK if src == 16 else BLK_BV
                o_ref[0, :, dst * LANES:(dst + 1) * LANES] = jnp.where(low, a, sw).astype(BF16)
                o_ref[0, :, (dst + 1) * LANES:(dst + 2) * LANES] = jnp.where(low, sw, a).astype(BF16)
            else:
                dst = src if src < 16 else src + 2
                o_ref[0, :, dst * LANES:(dst + 1) * LANES] = a.astype(BF16)


def _proj(xa, mods, norm_g, cos_t, sin_t, w_in_bf):
    b, ntok, _ = xa.shape
    nt = ntok // TM
    mod_row = lambda bi, t: jnp.where(t == 0, b, bi)
    return pl.pallas_call(
        _proj_kernel,
        grid=(b, nt),
        in_specs=[
            pl.BlockSpec((1, TM, D_MODEL), lambda bi, t: (bi, t, 0)),
            pl.BlockSpec((1, 1, D_MODEL), lambda bi, t: (mod_row(bi, t), 0, 0)),
            pl.BlockSpec((1, 1, D_MODEL), lambda bi, t: (mod_row(bi, t), 0, 1)),
            pl.BlockSpec((1, D_MODEL), lambda bi, t: (0, 0)),
            pl.BlockSpec((TM, LANES), lambda bi, t: (t, 0)),
            pl.BlockSpec((TM, LANES), lambda bi, t: (t, 0)),
            pl.BlockSpec((D_MODEL, D_IN), lambda bi, t: (0, 0)),
        ],
        out_specs=pl.BlockSpec((1, TM, N_OUT_BLK * LANES), lambda bi, t: (bi, t, 0)),
        out_shape=jax.ShapeDtypeStruct((b, ntok, N_OUT_BLK * LANES), BF16),
        compiler_params=_cparams(("arbitrary", "arbitrary")),
    )(xa, mods, mods, norm_g.reshape(1, -1), cos_t, sin_t, w_in_bf)


def _diff_kernel(lam_ref, g_ref, q_ref, k_ref, v_ref, o_ref, *, lam_init, t0, ntok):
    t = pl.program_id(2) + t0
    lv = lam_ref[...]
    lam = (jnp.exp(jnp.sum(lv[0:1] * lv[1:2], axis=-1, keepdims=True))
           - jnp.exp(jnp.sum(lv[2:3] * lv[3:4], axis=-1, keepdims=True)) + lam_init)
    q = q_ref[0]
    lane = lax.broadcasted_iota(I32, (1, LANES), 1)
    zero = jnp.zeros_like(q)
    q0 = jnp.where(lane < HEAD_DIM, q, zero)
    q1 = jnp.where(lane >= HEAD_DIM, q, zero)
    nt_dims = (((1,), (1,)), ((), ()))

    def attend(nk):
        k = k_ref[0, 0:nk, :]
        v = v_ref[0, 0:nk, :]
        s0 = lax.dot_general(q0, k, nt_dims, preferred_element_type=F32)
        s1 = lax.dot_general(q1, k, nt_dims, preferred_element_type=F32)
        e0 = jnp.exp(s0 - jnp.max(s0, axis=-1, keepdims=True))
        e1 = jnp.exp(s1 - jnp.max(s1, axis=-1, keepdims=True))
        r0 = 1.0 / jnp.sum(e0, axis=-1, keepdims=True)
        r1 = lam / jnp.sum(e1, axis=-1, keepdims=True)
        a = e0 * r0 - e1 * r1
        o = jnp.dot(a.astype(BF16), v, preferred_element_type=F32)
        o = o * lax.rsqrt(jnp.mean(o * o, axis=-1, keepdims=True) + EPS) * g_ref[...] * (1.0 - lam_init)
        o_ref[0] = o.astype(BF16)

    if t0 == 0:
        @pl.when(t == 0)
        def _():
            attend(CTX_LEN)

        @pl.when(t > 0)
        def _():
            attend(ntok)
    else:
        attend(ntok)


def _diff(p, lam_vec, subln_g, lam_init, t0):
    b, ntok, _ = p.shape
    nt = ntok // TM - t0
    heads = 4
    return pl.pallas_call(
        functools.partial(_diff_kernel, lam_init=lam_init, t0=t0, ntok=ntok),
        grid=(b, heads, nt),
        in_specs=[
            pl.BlockSpec((4, HEAD_DIM), lambda bi, h, t: (0, 0)),
            pl.BlockSpec((1, LANES), lambda bi, h, t: (0, 0)),
            pl.BlockSpec((1, TM, LANES), lambda bi, h, t: (bi, t + t0, BLK_AQ + h)),
            pl.BlockSpec((1, ntok, LANES), lambda bi, h, t: (bi, 0, BLK_AK + h)),
            pl.BlockSpec((1, ntok, LANES), lambda bi, h, t: (bi, 0, BLK_AV + h)),
        ],
        out_specs=pl.BlockSpec((1, TM, LANES), lambda bi, h, t: (bi, t, h)),
        out_shape=jax.ShapeDtypeStruct((b, nt * TM, heads * LANES), BF16),
        compiler_params=_cparams(("arbitrary", "arbitrary", "arbitrary")),
    )(lam_vec, subln_g.reshape(1, -1), p, p, p)


def _win_kernel(sink_ref, q_ref, k_ref, v_ref, o_ref, *, t0, ntok):
    t = pl.program_id(1) + t0
    lane = lax.broadcasted_iota(I32, (1, LANES), 1)
    low = lane < HEAD_DIM
    nt_dims = (((1,), (1,)), ((), ()))
    wk = 2 * TM
    start = jnp.clip(t * TM - WINDOW, CTX_LEN, ntok - wk)
    start = pl.multiple_of(start, WINDOW)
    qpos = t * TM + lax.broadcasted_iota(I32, (TM, wk), 0)
    kpos = start + lax.broadcasted_iota(I32, (TM, wk), 1)
    in_band = (jnp.abs(kpos - qpos) <= WINDOW) & (t > 0)
    bias = jnp.where(in_band, 0.0, -jnp.inf).astype(F32)

    for pair in range(4):
        qb = q_ref[0, :, pair * LANES:(pair + 1) * LANES]
        zero = jnp.zeros_like(qb)
        outs = []
        for sub in range(2):
            h = 2 * pair + sub
            g = h // 4
            qm = jnp.where(low if sub == 0 else jnp.logical_not(low), qb, zero)
            kc = k_ref[0, 0:CTX_LEN, g * LANES:(g + 1) * LANES]
            vc = v_ref[0, 0:CTX_LEN, g * LANES:(g + 1) * LANES]
            kw = k_ref[0, pl.ds(start, wk), g * LANES:(g + 1) * LANES]
            vw = v_ref[0, pl.ds(start, wk), g * LANES:(g + 1) * LANES]
            s_c = lax.dot_general(qm, kc, nt_dims, preferred_element_type=F32)
            s_w = lax.dot_general(qm, kw, nt_dims, preferred_element_type=F32) + bias
            sk = sink_ref[h]
            m = jnp.maximum(jnp.maximum(jnp.max(s_c, axis=-1, keepdims=True),
                                        jnp.max(s_w, axis=-1, keepdims=True)), sk)
            e_c = jnp.exp(s_c - m)
            e_w = jnp.exp(s_w - m)
            l = (jnp.sum(e_c, axis=-1, keepdims=True) + jnp.sum(e_w, axis=-1, keepdims=True)
                 + jnp.exp(sk - m))
            o = (jnp.dot(e_c.astype(BF16), vc, preferred_element_type=F32)
                 + jnp.dot(e_w.astype(BF16), vw, preferred_element_type=F32))
            outs.append(o / l)
        o_ref[0, :, pair * LANES:(pair + 1) * LANES] = jnp.where(low, outs[0], outs[1]).astype(BF16)


def _win(p, sink, t0):
    b, ntok, _ = p.shape
    nt = ntok // TM - t0
    return pl.pallas_call(
        functools.partial(_win_kernel, t0=t0, ntok=ntok),
        grid=(b, nt),
        in_specs=[
            pl.BlockSpec(memory_space=pltpu.SMEM),
            pl.BlockSpec((1, TM, 4 * LANES), lambda bi, t: (bi, t + t0, BLK_BQ // 4)),
            pl.BlockSpec((1, ntok, 2 * LANES), lambda bi, t: (bi, 0, BLK_BK // 2)),
            pl.BlockSpec((1, ntok, 2 * LANES), lambda bi, t: (bi, 0, BLK_BV // 2)),
        ],
        out_specs=pl.BlockSpec((1, TM, 4 * LANES), lambda bi, t: (bi, t, 0)),
        out_shape=jax.ShapeDtypeStruct((b, nt * TM, 4 * LANES), BF16),
        compiler_params=_cparams(("arbitrary", "arbitrary")),
    )(sink, p, p, p)


def _ret_kernel(dec_ref, q_ref, k_ref, v_ref, o_ref, *, ntok):
    T = RT_CHUNK
    nchunk = ntok // T
    nctx = CTX_LEN // T
    x = dec_ref[...]
    lg_all = -(jnp.maximum(-x, 0.0) + jnp.log1p(jnp.exp(-jnp.abs(x))))
    lane = lax.broadcasted_iota(I32, (1, LANES), 1)
    low = lane < HEAD_DIM
    row_low = lax.broadcasted_iota(I32, (LANES, 1), 0) < HEAD_DIM
    pi = lax.broadcasted_iota(I32, (T, T), 0).astype(F32)
    pj = lax.broadcasted_iota(I32, (T, T), 1).astype(F32)
    pcol = lax.broadcasted_iota(I32, (T, 1), 0).astype(F32)
    nt_dims = (((1,), (1,)), ((), ()))
    tn_dims = (((0,), (0,)), ((), ()))

    for direction in range(2):
        lgs = [lg_all[direction:direction + 1, h:h + 1] for h in range(4)]
        if direction == 0:
            dist = pi - pj
            qd = [jnp.exp((pcol + 1.0) * lg) for lg in lgs]
            kd = [jnp.exp((T - 1.0 - pcol) * lg) for lg in lgs]
        else:
            dist = pj - pi
            qd = [jnp.exp((T - pcol) * lg) for lg in lgs]
            kd = [jnp.exp(pcol * lg) for lg in lgs]
        decay = [jnp.where(dist >= 0, jnp.exp(jnp.maximum(dist, 0.0) * lg), 0.0) for lg in lgs]
        cd = [jnp.exp(T * lg) for lg in lgs]

        def step(i, states, direction=direction, decay=decay, qd=qd, kd=kd, cd=cd):
            if direction == 0:
                c = i
            else:
                c = jnp.where(i < nctx, nctx - 1 - i, nchunk + nctx - 1 - i)
            r0 = pl.multiple_of(c * T, T)
            new_states = []
            for pair in range(2):
                qb = q_ref[0, pl.ds(r0, T), pair * LANES:(pair + 1) * LANES]
                kb = k_ref[0, pl.ds(r0, T), pair * LANES:(pair + 1) * LANES]
                st = states[pair]
                kdec = jnp.where(low, kd[2 * pair], kd[2 * pair + 1])
                kbd = (kb.astype(F32) * kdec).astype(BF16)
                stb = st.astype(BF16)
                kvs = []
                for sub in range(2):
                    h = 2 * pair + sub
                    vh = v_ref[0, pl.ds(r0, T), h * LANES:(h + 1) * LANES]
                    qm = jnp.where(low if sub == 0 else jnp.logical_not(low), qb, jnp.zeros_like(qb))
                    inner = lax.dot_general(qm, kb, nt_dims, preferred_element_type=F32) * decay[h]
                    qs = (qm.astype(F32) * qd[h]).astype(BF16)
                    o = (jnp.dot(inner.astype(BF16), vh, preferred_element_type=F32)
                         + jnp.dot(qs, stb, preferred_element_type=F32))
                    if direction == 0:
                        o_ref[0, pl.ds(r0, T), h * LANES:(h + 1) * LANES] = o
                    else:
                        o_ref[0, pl.ds(r0, T), h * LANES:(h + 1) * LANES] += o
                    kvs.append(lax.dot_general(kbd, vh, tn_dims, preferred_element_type=F32))
                cdp = jnp.where(row_low, cd[2 * pair], cd[2 * pair + 1])
                new_states.append(st * cdp + jnp.where(row_low, kvs[0], kvs[1]))
            return tuple(new_states)

        zero = jnp.zeros((LANES, LANES), F32)
        lax.fori_loop(0, nchunk, step, (zero, zero))


def _ret(p, ret_decay):
    b, ntok, _ = p.shape
    return pl.pallas_call(
        functools.partial(_ret_kernel, ntok=ntok),
        grid=(b,),
        in_specs=[
            pl.BlockSpec((2, 4), lambda bi: (0, 0)),
            pl.BlockSpec((1, ntok, 2 * LANES), lambda bi: (bi, 0, BLK_CQ // 2)),
            pl.BlockSpec((1, ntok, 2 * LANES), lambda bi: (bi, 0, BLK_CK // 2)),
            pl.BlockSpec((1, ntok, 4 * LANES), lambda bi: (bi, 0, BLK_CV // 4)),
        ],
        out_specs=pl.BlockSpec((1, ntok, 4 * LANES), lambda bi: (bi, 0, 0)),
        out_shape=jax.ShapeDtypeStruct((b, ntok, 4 * LANES), F32),
        compiler_params=_cparams(("arbitrary",)),
    )(ret_decay, p, p, p)


def _pack_rows(h):
    half = h.shape[1] // 2
    hi = lax.bitcast_convert_type(h[:, :half].astype(BF16).astype(F32), U32)
    lo = lax.bitcast_convert_type(h[:, half:].astype(BF16).astype(F32), U32)
    return hi | (lo >> 16)


def _unpack_rows(p):
    hi = lax.bitcast_convert_type(p & jnp.uint32(0xFFFF0000), F32)
    lo = lax.bitcast_convert_type(p << 16, F32)
    return hi, lo


def _merge_kernel(x_ref, oa_ref, ob_ref, oc_ref, cg_ref, ga_ref, gb_ref, gc_ref, g1_ref, sh2_ref, sc2_ref,
                  gn_ref, nf_ref, wb_ref, wo_ref, rw_ref, rb_ref,
                  xo_ref, hp_ref, ri_ref, rwt_ref, cnt_ref):
    first = (pl.program_id(0) == 0) & (pl.program_id(1) == 0)

    @pl.when(first)
    def _():
        cnt_ref[...] = jnp.zeros_like(cnt_ref)

    oc = oc_ref[0]
    cgv = cg_ref[0].astype(F32)
    gn = gn_ref[...]
    parts = []
    for h in range(4):
        o = oc[:, h * LANES:(h + 1) * LANES]
        mu = jnp.mean(o, axis=-1, keepdims=True)
        dlt = o - mu
        var = jnp.mean(dlt * dlt, axis=-1, keepdims=True)
        y = dlt * lax.rsqrt(var + EPS) * gn[:, h * LANES:(h + 1) * LANES]
        cgh = cgv[:, h * LANES:(h + 1) * LANES]
        parts.append((y * (cgh * jax.nn.sigmoid(cgh))).astype(BF16))
    ocn = jnp.concatenate(parts, axis=1)

    y = jax.nn.sigmoid(ga_ref[0].astype(F32)) * jnp.dot(oa_ref[0], wb_ref[0], preferred_element_type=F32)
    y += jax.nn.sigmoid(gb_ref[0].astype(F32)) * jnp.dot(ob_ref[0], wb_ref[1], preferred_element_type=F32)
    y += jax.nn.sigmoid(gc_ref[0].astype(F32)) * jnp.dot(ocn, wb_ref[2], preferred_element_type=F32)
    y2 = jnp.dot(y.astype(BF16), wo_ref[...], preferred_element_type=F32)
    x = x_ref[0] + g1_ref[0] * y2
    xo_ref[0] = x

    ms = jnp.mean(x * x, axis=-1, keepdims=True)
    h2 = x * lax.rsqrt(ms + EPS) * nf_ref[...]
    h2 = h2 * (1.0 + sc2_ref[0]) + sh2_ref[0]
    hp_ref[...] = _pack_rows(h2)

    nt_dims = (((1,), (1,)), ((), ()))
    logits = lax.dot_general(rw_ref[...], h2.astype(BF16), nt_dims, preferred_element_type=F32)
    sc = jax.nn.sigmoid(logits)
    bi = sc + rb_ref[...]
    ei_i = lax.broadcasted_iota(I32, (N_EXPERTS, 1), 0)
    ei = ei_i.astype(F32)
    epg = N_EXPERTS // N_GROUPS
    egroup = (ei_i >> 2).astype(F32)
    gsum = []
    for g in range(N_GROUPS):
        r = [bi[epg * g + j:epg * g + j + 1, :] for j in range(epg)]
        m = r[0] + r[1]
        for a in range(epg):
            for c in range(a + 1, epg):
                if (a, c) != (0, 1):
                    m = jnp.maximum(m, r[a] + r[c])
        gsum.append(m)
    best = jnp.zeros_like(gsum[0])
    bsc = gsum[0]
    for g in range(1, N_GROUPS):
        upd = gsum[g] > bsc
        best = jnp.where(upd, float(g), best)
        bsc = jnp.where(upd, gsum[g], bsc)
    neg = -jnp.inf
    masked = jnp.where(egroup == best, bi, neg)
    m1 = jnp.max(masked, axis=0, keepdims=True)
    i1 = jnp.min(jnp.where(masked == m1, ei, float(N_EXPERTS)), axis=0, keepdims=True)
    masked2 = jnp.where(ei == i1, neg, masked)
    m2 = jnp.max(masked2, axis=0, keepdims=True)
    i2 = jnp.min(jnp.where(masked2 == m2, ei, float(N_EXPERTS)), axis=0, keepdims=True)
    sel1 = ei == i1
    sel2 = ei == i2
    w1 = jnp.sum(jnp.where(sel1, sc, 0.0), axis=0, keepdims=True)
    w2 = jnp.sum(jnp.where(sel2, sc, 0.0), axis=0, keepdims=True)
    ws = w1 + w2
    rwt_ref[0, 0:1, :] = w1 / ws
    rwt_ref[0, 1:2, :] = w2 / ws

    oh = (sel1 | sel2).astype(F32)
    si = lax.broadcasted_iota(I32, (TM, TM), 0)
    ti = lax.broadcasted_iota(I32, (TM, TM), 1)
    upper = (si < ti).astype(BF16)
    prefix = jnp.dot(oh.astype(BF16), upper, preferred_element_type=F32) + cnt_ref[:, 0:1]
    rank1 = jnp.sum(jnp.where(sel1, prefix, 0.0), axis=0, keepdims=True)
    rank2 = jnp.sum(jnp.where(sel2, prefix, 0.0), axis=0, keepdims=True)
    ri_ref[0, 0:1, :] = i1.astype(I32)
    ri_ref[0, 1:2, :] = i2.astype(I32)
    ri_ref[0, 2:3, :] = rank1.astype(I32)
    ri_ref[0, 3:4, :] = rank2.astype(I32)
    cnt_ref[...] = cnt_ref[...] + jnp.sum(oh, axis=1, keepdims=True)


def _merge(xa, p, oa, ob, oc, mods, gn_g, nf_g, wb_bf, wo_bf, rw_t, rb_col, t0):
    b, ntok, _ = xa.shape
    nt = ntok // TM - t0
    ntiles = b * nt
    mod_row = lambda bi, t: jnp.where(t + t0 == 0, b, bi)
    tok = lambda bi, t: (bi, t + t0, 0)
    flat = lambda bi, t: (bi * nt + t, 0, 0)
    outs = pl.pallas_call(
        _merge_kernel,
        grid=(b, nt),
        in_specs=[
            pl.BlockSpec((1, TM, D_MODEL), tok),
            pl.BlockSpec((1, TM, 4 * LANES), lambda bi, t: (bi, t, 0)),
            pl.BlockSpec((1, TM, 4 * LANES), lambda bi, t: (bi, t, 0)),
            pl.BlockSpec((1, TM, 4 * LANES), tok),
            pl.BlockSpec((1, TM, 4 * LANES), lambda bi, t: (bi, t + t0, BLK_CG // 4)),
            pl.BlockSpec((1, TM, D_MODEL), lambda bi, t: (bi, t + t0, BLK_GATE // 8)),
            pl.BlockSpec((1, TM, D_MODEL), lambda bi, t: (bi, t + t0, BLK_GATE // 8 + 1)),
            pl.BlockSpec((1, TM, D_MODEL), lambda bi, t: (bi, t + t0, BLK_GATE // 8 + 2)),
            pl.BlockSpec((1, 1, D_MODEL), lambda bi, t: (mod_row(bi, t), 0, 2)),
            pl.BlockSpec((1, 1, D_MODEL), lambda bi, t: (mod_row(bi, t), 0, 3)),
            pl.BlockSpec((1, 1, D_MODEL), lambda bi, t: (mod_row(bi, t), 0, 4)),
            pl.BlockSpec((1, 4 * LANES), lambda bi, t: (0, 0)),
            pl.BlockSpec((1, D_MODEL), lambda bi, t: (0, 0)),
            pl.BlockSpec((3, 4 * LANES, D_MODEL), lambda bi, t: (0, 0, 0)),
            pl.BlockSpec((D_MODEL, D_MODEL), lambda bi, t: (0, 0)),
            pl.BlockSpec((N_EXPERTS, D_MODEL), lambda bi, t: (0, 0)),
            pl.BlockSpec((N_EXPERTS, 1), lambda bi, t: (0, 0)),
        ],
        out_specs=[
            pl.BlockSpec((1, TM, D_MODEL), tok),
            pl.BlockSpec((TM, D_MODEL // 2), lambda bi, t: (bi * nt + t, 0)),
            pl.BlockSpec((1, 4, TM), flat),
            pl.BlockSpec((1, 2, TM), flat),
            pl.BlockSpec((N_EXPERTS, LANES), lambda bi, t: (0, 0)),
        ],
        out_shape=[
            jax.ShapeDtypeStruct(xa.shape, F32),
            jax.ShapeDtypeStruct((ntiles * TM, D_MODEL // 2), U32),
            jax.ShapeDtypeStruct((ntiles, 4, TM), I32),
            jax.ShapeDtypeStruct((ntiles, 2, TM), F32),
            jax.ShapeDtypeStruct((N_EXPERTS, LANES), F32),
        ],
        input_output_aliases={0: 0},
        compiler_params=_cparams(("arbitrary", "arbitrary")),
    )(xa, oa, ob, oc, p, p, p, p, mods, mods, mods, gn_g.reshape(1, -1), nf_g.reshape(1, -1),
      wb_bf, wo_bf, rw_t, rb_col)
    return outs


def _dispatch_kernel(slot_ref, h_ref, xs_in_ref, xs_ref, sem):
    del xs_in_ref

    def row_copy(r, k):
        return pltpu.make_async_copy(h_ref.at[pl.ds(r, 1)], xs_ref.at[pl.ds(slot_ref[0, k, r], 1)], sem)

    def issue(r, carry):
        row_copy(r, 0).start()
        row_copy(r, 1).start()
        return carry

    def drain(r, carry):
        row_copy(r, 0).wait()
        row_copy(r, 1).wait()
        return carry

    lax.fori_loop(0, TM, issue, 0)
    lax.fori_loop(0, TM, drain, 0)


def _dispatch(hp, slots, n_rows):
    ntiles = slots.shape[0]
    xs0 = jnp.zeros((n_rows, D_MODEL // 2), U32)
    return pl.pallas_call(
        _dispatch_kernel,
        grid=(ntiles,),
        in_specs=[
            pl.BlockSpec((1, 2, TM), lambda i: (i, 0, 0), memory_space=pltpu.SMEM),
            pl.BlockSpec((TM, D_MODEL // 2), lambda i: (i, 0)),
            pl.BlockSpec(memory_space=pl.ANY),
        ],
        out_specs=pl.BlockSpec(memory_space=pl.ANY),
        out_shape=jax.ShapeDtypeStruct((n_rows, D_MODEL // 2), U32),
        scratch_shapes=[pltpu.SemaphoreType.DMA],
        input_output_aliases={2: 0},
        compiler_params=_cparams(("arbitrary",)),
    )(slots, hp, xs0)


def _ffn_kernel(te_ref, tv_ref, xs_ref, wg_ref, wu_ref, wd_ref, ys_ref, wgb, wub, wdb):
    i = pl.program_id(0)
    prev = te_ref[jnp.maximum(i - 1, 0)]
    fresh = (i == 0) | (te_ref[i] != prev)

    @pl.when(fresh)
    def _():
        wgb[...] = wg_ref[0].astype(BF16)
        wub[...] = wu_ref[0].astype(BF16)
        wdb[...] = wd_ref[0].astype(BF16)

    @pl.when(tv_ref[i] > 0)
    def _():
        hi, lo = _unpack_rows(xs_ref[...])
        xb = jnp.concatenate([hi.astype(BF16), lo.astype(BF16)], axis=1)
        g = jnp.dot(xb, wgb[...], preferred_element_type=F32)
        u = jnp.dot(xb, wub[...], preferred_element_type=F32)
        he = (g * jax.nn.sigmoid(g) * u).astype(BF16)
        ys_ref[...] = _pack_rows(jnp.dot(he, wdb[...], preferred_element_type=F32))

    @pl.when(tv_ref[i] == 0)
    def _():
        ys_ref[...] = jnp.zeros_like(ys_ref)


def _ffn(xs, tile_e, tile_v, wg, wu, wd):
    n_rows = xs.shape[0]
    n_tiles = n_rows // TMF
    grid_spec = pltpu.PrefetchScalarGridSpec(
        num_scalar_prefetch=2,
        grid=(n_tiles,),
        in_specs=[
            pl.BlockSpec((TMF, D_MODEL // 2), lambda i, te, tv: (i, 0)),
            pl.BlockSpec((1, D_MODEL, D_EXPERT), lambda i, te, tv: (te[i], 0, 0)),
            pl.BlockSpec((1, D_MODEL, D_EXPERT), lambda i, te, tv: (te[i], 0, 0)),
            pl.BlockSpec((1, D_EXPERT, D_MODEL), lambda i, te, tv: (te[i], 0, 0)),
        ],
        out_specs=pl.BlockSpec((TMF, D_MODEL // 2), lambda i, te, tv: (i, 0)),
        scratch_shapes=[
            pltpu.VMEM((D_MODEL, D_EXPERT), BF16),
            pltpu.VMEM((D_MODEL, D_EXPERT), BF16),
            pltpu.VMEM((D_EXPERT, D_MODEL), BF16),
        ],
    )
    return pl.pallas_call(
        _ffn_kernel,
        grid_spec=grid_spec,
        out_shape=jax.ShapeDtypeStruct((n_rows, D_MODEL // 2), U32),
        compiler_params=_cparams(("arbitrary",)),
    )(tile_e, tile_v, xs, wg, wu, wd)


def _combine_kernel(slot_ref, x_ref, w_ref, g2_ref, fg_ref, ys_ref, o_ref, y0, y1, sem, *, final):
    def row_copy(r, k):
        dst = y0 if k == 0 else y1
        return pltpu.make_async_copy(ys_ref.at[pl.ds(slot_ref[0, k, r], 1)], dst.at[pl.ds(r, 1)], sem)

    def issue(r, carry):
        row_copy(r, 0).start()
        row_copy(r, 1).start()
        return carry

    def drain(r, carry):
        row_copy(r, 0).wait()
        row_copy(r, 1).wait()
        return carry

    lax.fori_loop(0, TM, issue, 0)
    lax.fori_loop(0, TM, drain, 0)

    w = w_ref[...]
    a0, b0 = _unpack_rows(y0[...])
    a1, b1 = _unpack_rows(y1[...])
    w0 = w[:, 0:1]
    w1 = w[:, 1:2]
    moe = jnp.concatenate([w0 * a0 + w1 * a1, w0 * b0 + w1 * b1], axis=1)
    x = x_ref[0] + g2_ref[0] * moe
    if final:
        ms = jnp.mean(x * x, axis=-1, keepdims=True)
        x = x * lax.rsqrt(ms + EPS) * fg_ref[...]
    o_ref[0] = x


def _combine(xa, ys, slots, w_cols, mods, final_g, t0, final):
    b, ntok, _ = xa.shape
    nt = ntok // TM - t0
    mod_row = lambda bi, t: jnp.where(t + t0 == 0, b, bi)
    if final:
        out_shape = jax.ShapeDtypeStruct((b, nt * TM, D_MODEL), F32)
        out_spec = pl.BlockSpec((1, TM, D_MODEL), lambda bi, t: (bi, t, 0))
        aliases = {}
    else:
        out_shape = jax.ShapeDtypeStruct(xa.shape, F32)
        out_spec = pl.BlockSpec((1, TM, D_MODEL), lambda bi, t: (bi, t + t0, 0))
        aliases = {1: 0}
    return pl.pallas_call(
        functools.partial(_combine_kernel, final=final),
        grid=(b, nt),
        in_specs=[
            pl.BlockSpec((1, 2, TM), lambda bi, t: (bi * nt + t, 0, 0), memory_space=pltpu.SMEM),
            pl.BlockSpec((1, TM, D_MODEL), lambda bi, t: (bi, t + t0, 0)),
            pl.BlockSpec((TM, 2), lambda bi, t: (bi * nt + t, 0)),
            pl.BlockSpec((1, 1, D_MODEL), lambda bi, t: (mod_row(bi, t), 0, 5)),
            pl.BlockSpec((1, D_MODEL), lambda bi, t: (0, 0)),
            pl.BlockSpec(memory_space=pl.ANY),
        ],
        out_specs=out_spec,
        out_shape=out_shape,
        scratch_shapes=[
            pltpu.VMEM((TM, D_MODEL // 2), U32),
            pltpu.VMEM((TM, D_MODEL // 2), U32),
            pltpu.SemaphoreType.DMA,
        ],
        input_output_aliases=aliases,
        compiler_params=_cparams(("arbitrary", "arbitrary")),
    )(slots, xa, w_cols, mods, final_g.reshape(1, -1), ys)


def _route_tables(ri, cnt, n_assign):
    counts = cnt[:, 0].astype(I32)
    padded = ((counts + TMF - 1) // TMF) * TMF
    ends = jnp.cumsum(padded)
    off = ends - padded
    slots = jnp.take(off, ri[:, 0:2, :], axis=0) + ri[:, 2:4, :]
    n_tiles = n_assign // TMF + N_EXPERTS
    starts = jnp.arange(n_tiles, dtype=I32) * TMF
    tile_e = jnp.minimum(jnp.sum(starts[:, None] >= ends[None, :], axis=1), N_EXPERTS - 1).astype(I32)
    valid = starts < ends[-1]
    last_e = tile_e[jnp.maximum(ends[-1] // TMF - 1, 0)]
    tile_e = jnp.where(valid, tile_e, last_e)
    return slots, tile_e, valid.astype(I32), n_tiles * TMF


def _rope_tables(n_lat):
    quarter = HEAD_DIM // 4
    inv = ROPE_BASE ** (-jnp.arange(quarter, dtype=F32) / quarter)
    n_rows = n_lat // GRID_W
    rows = jnp.repeat(jnp.arange(n_rows, dtype=F32), GRID_W)
    cols = jnp.tile(jnp.arange(GRID_W, dtype=F32), n_rows)
    ang = jnp.concatenate([rows[:, None] * inv, cols[:, None] * inv], axis=-1)
    cos = jnp.cos(ang)
    sin = jnp.sin(ang)
    cos_t = jnp.tile(cos, (1, LANES // cos.shape[1]))
    sin_t = jnp.tile(jnp.concatenate([-sin, sin], axis=-1), (1, LANES // HEAD_DIM))
    cos_t = jnp.concatenate([jnp.ones((CTX_LEN, LANES), F32), cos_t], axis=0)
    sin_t = jnp.concatenate([jnp.zeros((CTX_LEN, LANES), F32), sin_t], axis=0)
    return cos_t, sin_t


def kernel(x, c, ctx, c_ctx, w_ada, b_ada, norm_mix_g, norm_ffn_g, w_in, diff_lambda, diff_subln_g, swa_sink,
           ret_decay, ret_gn_g, w_branch, w_out, router_w, router_b, w_exp_gate, w_exp_up, w_exp_down, final_g):
    b, n_lat, _ = x.shape
    depth = w_ada.shape[0]
    assert ctx.shape[1] == CTX_LEN == TM and n_lat % TM == 0 and n_lat >= 2 * TM

    pad_rows = (-(b + 1)) % 8
    c_all = jnp.concatenate([c, c_ctx[None, :], jnp.zeros((pad_rows, D_MODEL), F32)], axis=0)
    mods_all = _ada(c_all, w_ada, b_ada)
    cos_t, sin_t = _rope_tables(n_lat)
    xa = jnp.concatenate([ctx, x], axis=1)
    rw_t = router_w.T.astype(BF16)
    rb_col = router_b.reshape(-1, 1).astype(F32)

    for l in range(depth):
        last = l == depth - 1
        t0 = 1 if last else 0
        lam_init = 0.8 - 0.6 * math.exp(-0.3 * l)
        mods = mods_all[l].reshape(mods_all.shape[1], 1, -1)
        p = _proj(xa, mods, norm_mix_g[l], cos_t, sin_t, w_in[l].astype(BF16))
        oa = _diff(p, diff_lambda[l], diff_subln_g[l], lam_init, t0)
        ob = _win(p, swa_sink[l], t0)
        oc = _ret(p, ret_decay[l])
        xa, hp, ri, rwt, cnt = _merge(xa, p, oa, ob, oc, mods, ret_gn_g[l], norm_ffn_g[l],
                                      w_branch[l].astype(BF16), w_out[l].astype(BF16), rw_t, rb_col, t0)
        n_assign = 2 * hp.shape[0]
        slots, tile_e, tile_v, n_rows = _route_tables(ri, cnt, n_assign)
        xs = _dispatch(hp, slots, n_rows)
        ys = _ffn(xs, tile_e, tile_v, w_exp_gate[l], w_exp_up[l], w_exp_down[l])
        w_cols = jnp.swapaxes(rwt, 1, 2).reshape(-1, 2)
        xa = _combine(xa, ys, slots, w_cols, mods, final_g, t0, last)
    return xa
```

```python
import functools
import math

import jax
import jax.numpy as jnp
from jax import lax
from jax.experimental import pallas as pl
from jax.experimental.pallas import tpu as pltpu

F32 = jnp.float32
BF16 = jnp.bfloat16
U32 = jnp.uint32
I32 = jnp.int32

D_MODEL = 1024
CTX_LEN = 256
GRID_W = 64
ROPE_BASE = 10000.0
EPS = 1e-6
HEAD_DIM = 64
WINDOW = 128
RT_CHUNK = 128
N_EXPERTS = 16
N_GROUPS = 4
D_EXPERT = 512
LANES = 128
TM = 256
TMF = 512
SEG_ALIGN = 8
SEG_SIZES = (256, 128, 64, 32, 16, 8)
SORT_ROWS = 640
XS_COLS = D_MODEL // 2 + 128
D_IN = 6912
N_SRC_BLK = D_IN // LANES
N_OUT_BLK = N_SRC_BLK + 2
VMEM_LIMIT = 56 * 1024 * 1024

BLK_AQ, BLK_AK, BLK_AV = 0, 4, 8
BLK_BQ, BLK_BK, BLK_BV = 12, 16, 18
BLK_CQ, BLK_CK, BLK_CV, BLK_CG, BLK_GATE = 20, 22, 24, 28, 32


def _cparams(sem):
    return pltpu.CompilerParams(dimension_semantics=sem, vmem_limit_bytes=VMEM_LIMIT)


def _ada_kernel(c_ref, w_ref, b_ref, o_ref):
    c = c_ref[...]
    s = (c * jax.nn.sigmoid(c)).astype(BF16)
    o_ref[0] = jnp.dot(s, w_ref[0].astype(BF16), preferred_element_type=F32) + b_ref[0]


def _ada(c_all, w_ada, b_ada):
    depth = w_ada.shape[0]
    rows = c_all.shape[0]
    nj = w_ada.shape[2] // D_MODEL
    return pl.pallas_call(
        _ada_kernel,
        grid=(depth, nj),
        in_specs=[
            pl.BlockSpec((rows, D_MODEL), lambda l, j: (0, 0)),
            pl.BlockSpec((1, D_MODEL, D_MODEL), lambda l, j: (l, 0, j)),
            pl.BlockSpec((1, 1, D_MODEL), lambda l, j: (l, 0, j)),
        ],
        out_specs=pl.BlockSpec((1, rows, D_MODEL), lambda l, j: (l, 0, j)),
        out_shape=jax.ShapeDtypeStruct((depth, rows, w_ada.shape[2]), F32),
        compiler_params=_cparams(("arbitrary", "arbitrary")),
    )(c_all, w_ada, b_ada.reshape(depth, 1, -1))


def _rope(a, cos, sin_signed, first_half):
    rot = jnp.where(first_half, pltpu.roll(a, LANES - HEAD_DIM // 2, 1), pltpu.roll(a, HEAD_DIM // 2, 1))
    return a * cos + rot * sin_signed


def _proj_kernel(x_ref, sh_ref, sc_ref, g_ref, cos_ref, sin_ref, w_ref, o_ref):
    x = x_ref[0]
    ms = jnp.mean(x * x, axis=-1, keepdims=True)
    h = x * lax.rsqrt(ms + EPS) * g_ref[...]
    h = h * (1.0 + sc_ref[0]) + sh_ref[0]
    hb = h.astype(BF16)
    cos = cos_ref[...]
    sin = sin_ref[...]
    lane = lax.broadcasted_iota(I32, (1, LANES), 1)
    first_half = (lane & (HEAD_DIM - 1)) < (HEAD_DIM // 2)
    low = lane < HEAD_DIM
    scale = HEAD_DIM ** -0.5
    for c in range(N_SRC_BLK // 2):
        acc = jnp.dot(hb, w_ref[:, c * 2 * LANES:(c + 1) * 2 * LANES], preferred_element_type=F32)
        for half in range(2):
            src = 2 * c + half
            a = acc[:, half * LANES:(half + 1) * LANES]
            if src < 8 or 12 <= src < 17:
                a = _rope(a, cos, sin, first_half)
            if src < 4 or 12 <= src < 16 or 20 <= src < 22:
                a = a * scale
            if src in (16, 17):
                sw = pltpu.roll(a, HEAD_DIM, 1)
                dst = BLK_BK if src == 16 else BLK_BV
                o_ref[0, :, dst * LANES:(dst + 1) * LANES] = jnp.where(low, a, sw).astype(BF16)
                o_ref[0, :, (dst + 1) * LANES:(dst + 2) * LANES] = jnp.where(low, sw, a).astype(BF16)
            else:
                dst = src if src < 16 else src + 2
                o_ref[0, :, dst * LANES:(dst + 1) * LANES] = a.astype(BF16)


def _proj(xa, mods, norm_g, cos_t, sin_t, w_in_bf):
    b, ntok, _ = xa.shape
    nt = ntok // TM
    mod_row = lambda bi, t: jnp.where(t == 0, b, bi)
    return pl.pallas_call(
        _proj_kernel,
        grid=(b, nt),
        in_specs=[
            pl.BlockSpec((1, TM, D_MODEL), lambda bi, t: (bi, t, 0)),
            pl.BlockSpec((1, 1, D_MODEL), lambda bi, t: (mod_row(bi, t), 0, 0)),
            pl.BlockSpec((1, 1, D_MODEL), lambda bi, t: (mod_row(bi, t), 0, 1)),
            pl.BlockSpec((1, D_MODEL), lambda bi, t: (0, 0)),
            pl.BlockSpec((TM, LANES), lambda bi, t: (t, 0)),
            pl.BlockSpec((TM, LANES), lambda bi, t: (t, 0)),
            pl.BlockSpec((D_MODEL, D_IN), lambda bi, t: (0, 0)),
        ],
        out_specs=pl.BlockSpec((1, TM, N_OUT_BLK * LANES), lambda bi, t: (bi, t, 0)),
        out_shape=jax.ShapeDtypeStruct((b, ntok, N_OUT_BLK * LANES), BF16),
        compiler_params=_cparams(("arbitrary", "arbitrary")),
    )(xa, mods, mods, norm_g.reshape(1, -1), cos_t, sin_t, w_in_bf)


def _diff_kernel(lam_ref, g_ref, q_ref, k_ref, v_ref, o_ref, *, lam_init, t0, ntok):
    t = pl.program_id(2) + t0
    lv = lam_ref[...]
    lam = (jnp.exp(jnp.sum(lv[0:1] * lv[1:2], axis=-1, keepdims=True))
           - jnp.exp(jnp.sum(lv[2:3] * lv[3:4], axis=-1, keepdims=True)) + lam_init)
    q = q_ref[0]
    lane = lax.broadcasted_iota(I32, (1, LANES), 1)
    zero = jnp.zeros_like(q)
    q0 = jnp.where(lane < HEAD_DIM, q, zero)
    q1 = jnp.where(lane >= HEAD_DIM, q, zero)
    nt_dims = (((1,), (1,)), ((), ()))

    def attend(nk):
        k = k_ref[0, 0:nk, :]
        v = v_ref[0, 0:nk, :]
        s0 = lax.dot_general(q0, k, nt_dims, preferred_element_type=F32)
        s1 = lax.dot_general(q1, k, nt_dims, preferred_element_type=F32)
        e0 = jnp.exp(s0 - jnp.max(s0, axis=-1, keepdims=True))
        e1 = jnp.exp(s1 - jnp.max(s1, axis=-1, keepdims=True))
        r0 = 1.0 / jnp.sum(e0, axis=-1, keepdims=True)
        r1 = lam / jnp.sum(e1, axis=-1, keepdims=True)
        a = e0 * r0 - e1 * r1
        o = jnp.dot(a.astype(BF16), v, preferred_element_type=F32)
        o = o * lax.rsqrt(jnp.mean(o * o, axis=-1, keepdims=True) + EPS) * g_ref[...] * (1.0 - lam_init)
        o_ref[0] = o.astype(BF16)

    if t0 == 0:
        @pl.when(t == 0)
        def _():
            attend(CTX_LEN)

        @pl.when(t > 0)
        def _():
            attend(ntok)
    else:
        attend(ntok)


def _diff(p, lam_vec, subln_g, lam_init, t0):
    b, ntok, _ = p.shape
    nt = ntok // TM - t0
    heads = 4
    return pl.pallas_call(
        functools.partial(_diff_kernel, lam_init=lam_init, t0=t0, ntok=ntok),
        grid=(b, heads, nt),
        in_specs=[
            pl.BlockSpec((4, HEAD_DIM), lambda bi, h, t: (0, 0)),
            pl.BlockSpec((1, LANES), lambda bi, h, t: (0, 0)),
            pl.BlockSpec((1, TM, LANES), lambda bi, h, t: (bi, t + t0, BLK_AQ + h)),
            pl.BlockSpec((1, ntok, LANES), lambda bi, h, t: (bi, 0, BLK_AK + h)),
            pl.BlockSpec((1, ntok, LANES), lambda bi, h, t: (bi, 0, BLK_AV + h)),
        ],
        out_specs=pl.BlockSpec((1, TM, LANES), lambda bi, h, t: (bi, t, h)),
        out_shape=jax.ShapeDtypeStruct((b, nt * TM, heads * LANES), BF16),
        compiler_params=_cparams(("arbitrary", "arbitrary", "arbitrary")),
    )(lam_vec, subln_g.reshape(1, -1), p, p, p)


def _win_kernel(sink_ref, q_ref, k_ref, v_ref, o_ref, *, t0, ntok):
    t = pl.program_id(1) + t0
    lane = lax.broadcasted_iota(I32, (1, LANES), 1)
    low = lane < HEAD_DIM
    nt_dims = (((1,), (1,)), ((), ()))
    wk = 2 * TM
    start = jnp.clip(t * TM - WINDOW, CTX_LEN, ntok - wk)
    start = pl.multiple_of(start, WINDOW)
    qpos = t * TM + lax.broadcasted_iota(I32, (TM, wk), 0)
    kpos = start + lax.broadcasted_iota(I32, (TM, wk), 1)
    in_band = (jnp.abs(kpos - qpos) <= WINDOW) & (t > 0)
    bias = jnp.where(in_band, 0.0, -jnp.inf).astype(F32)

    for pair in range(4):
        qb = q_ref[0, :, pair * LANES:(pair + 1) * LANES]
        zero = jnp.zeros_like(qb)
        outs = []
        for sub in range(2):
            h = 2 * pair + sub
            g = h // 4
            qm = jnp.where(low if sub == 0 else jnp.logical_not(low), qb, zero)
            kc = k_ref[0, 0:CTX_LEN, g * LANES:(g + 1) * LANES]
            vc = v_ref[0, 0:CTX_LEN, g * LANES:(g + 1) * LANES]
            kw = k_ref[0, pl.ds(start, wk), g * LANES:(g + 1) * LANES]
            vw = v_ref[0, pl.ds(start, wk), g * LANES:(g + 1) * LANES]
            s_c = lax.dot_general(qm, kc, nt_dims, preferred_element_type=F32)
            s_w = lax.dot_general(qm, kw, nt_dims, preferred_element_type=F32) + bias
            sk = sink_ref[h]
            m = jnp.maximum(jnp.maximum(jnp.max(s_c, axis=-1, keepdims=True),
                                        jnp.max(s_w, axis=-1, keepdims=True)), sk)
            e_c = jnp.exp(s_c - m)
            e_w = jnp.exp(s_w - m)
            l = (jnp.sum(e_c, axis=-1, keepdims=True) + jnp.sum(e_w, axis=-1, keepdims=True)
                 + jnp.exp(sk - m))
            o = (jnp.dot(e_c.astype(BF16), vc, preferred_element_type=F32)
                 + jnp.dot(e_w.astype(BF16), vw, preferred_element_type=F32))
            outs.append(o / l)
        o_ref[0, :, pair * LANES:(pair + 1) * LANES] = jnp.where(low, outs[0], outs[1]).astype(BF16)


def _win(p, sink, t0):
    b, ntok, _ = p.shape
    nt = ntok // TM - t0
    return pl.pallas_call(
        functools.partial(_win_kernel, t0=t0, ntok=ntok),
        grid=(b, nt),
        in_specs=[
            pl.BlockSpec(memory_space=pltpu.SMEM),
            pl.BlockSpec((1, TM, 4 * LANES), lambda bi, t: (bi, t + t0, BLK_BQ // 4)),
            pl.BlockSpec((1, ntok, 2 * LANES), lambda bi, t: (bi, 0, BLK_BK // 2)),
            pl.BlockSpec((1, ntok, 2 * LANES), lambda bi, t: (bi, 0, BLK_BV // 2)),
        ],
        out_specs=pl.BlockSpec((1, TM, 4 * LANES), lambda bi, t: (bi, t, 0)),
        out_shape=jax.ShapeDtypeStruct((b, nt * TM, 4 * LANES), BF16),
        compiler_params=_cparams(("arbitrary", "arbitrary")),
    )(sink, p, p, p)


def _ret_kernel(dec_ref, q_ref, k_ref, v_ref, o_ref, *, ntok):
    T = RT_CHUNK
    nchunk = ntok // T
    nctx = CTX_LEN // T
    x = dec_ref[...]
    lg_all = -(jnp.maximum(-x, 0.0) + jnp.log1p(jnp.exp(-jnp.abs(x))))
    lane = lax.broadcasted_iota(I32, (1, LANES), 1)
    low = lane < HEAD_DIM
    row_low = lax.broadcasted_iota(I32, (LANES, 1), 0) < HEAD_DIM
    pi = lax.broadcasted_iota(I32, (T, T), 0).astype(F32)
    pj = lax.broadcasted_iota(I32, (T, T), 1).astype(F32)
    pcol = lax.broadcasted_iota(I32, (T, 1), 0).astype(F32)
    nt_dims = (((1,), (1,)), ((), ()))
    tn_dims = (((0,), (0,)), ((), ()))

    for direction in range(2):
        lgs = [lg_all[direction:direction + 1, h:h + 1] for h in range(4)]
        if direction == 0:
            dist = pi - pj
            qd = [jnp.exp((pcol + 1.0) * lg) for lg in lgs]
            kd = [jnp.exp((T - 1.0 - pcol) * lg) for lg in lgs]
        else:
            dist = pj - pi
            qd = [jnp.exp((T - pcol) * lg) for lg in lgs]
            kd = [jnp.exp(pcol * lg) for lg in lgs]
        decay = [jnp.where(dist >= 0, jnp.exp(jnp.maximum(dist, 0.0) * lg), 0.0) for lg in lgs]
        cd = [jnp.exp(T * lg) for lg in lgs]

        def step(i, states, direction=direction, decay=decay, qd=qd, kd=kd, cd=cd):
            if direction == 0:
                c = i
            else:
                c = jnp.where(i < nctx, nctx - 1 - i, nchunk + nctx - 1 - i)
            r0 = pl.multiple_of(c * T, T)
            new_states = []
            for pair in range(2):
                qb = q_ref[0, pl.ds(r0, T), pair * LANES:(pair + 1) * LANES]
                kb = k_ref[0, pl.ds(r0, T), pair * LANES:(pair + 1) * LANES]
                st = states[pair]
                kdec = jnp.where(low, kd[2 * pair], kd[2 * pair + 1])
                kbd = (kb.astype(F32) * kdec).astype(BF16)
                stb = st.astype(BF16)
                kvs = []
                for sub in range(2):
                    h = 2 * pair + sub
                    vh = v_ref[0, pl.ds(r0, T), h * LANES:(h + 1) * LANES]
                    qm = jnp.where(low if sub == 0 else jnp.logical_not(low), qb, jnp.zeros_like(qb))
                    inner = lax.dot_general(qm, kb, nt_dims, preferred_element_type=F32) * decay[h]
                    qs = (qm.astype(F32) * qd[h]).astype(BF16)
                    o = (jnp.dot(inner.astype(BF16), vh, preferred_element_type=F32)
                         + jnp.dot(qs, stb, preferred_element_type=F32))
                    if direction == 0:
                        o_ref[0, pl.ds(r0, T), h * LANES:(h + 1) * LANES] = o
                    else:
                        o_ref[0, pl.ds(r0, T), h * LANES:(h + 1) * LANES] += o
                    kvs.append(lax.dot_general(kbd, vh, tn_dims, preferred_element_type=F32))
                cdp = jnp.where(row_low, cd[2 * pair], cd[2 * pair + 1])
                new_states.append(st * cdp + jnp.where(row_low, kvs[0], kvs[1]))
            return tuple(new_states)

        zero = jnp.zeros((LANES, LANES), F32)
        lax.fori_loop(0, nchunk, step, (zero, zero))


def _ret(p, ret_decay):
    b, ntok, _ = p.shape
    return pl.pallas_call(
        functools.partial(_ret_kernel, ntok=ntok),
        grid=(b,),
        in_specs=[
            pl.BlockSpec((2, 4), lambda bi: (0, 0)),
            pl.BlockSpec((1, ntok, 2 * LANES), lambda bi: (bi, 0, BLK_CQ // 2)),
            pl.BlockSpec((1, ntok, 2 * LANES), lambda bi: (bi, 0, BLK_CK // 2)),
            pl.BlockSpec((1, ntok, 4 * LANES), lambda bi: (bi, 0, BLK_CV // 4)),
        ],
        out_specs=pl.BlockSpec((1, ntok, 4 * LANES), lambda bi: (bi, 0, 0)),
        out_shape=jax.ShapeDtypeStruct((b, ntok, 4 * LANES), F32),
        compiler_params=_cparams(("arbitrary",)),
    )(ret_decay, p, p, p)


def _pack_rows(h):
    half = h.shape[1] // 2
    hi = lax.bitcast_convert_type(h[:, :half].astype(BF16).astype(F32), U32)
    lo = lax.bitcast_convert_type(h[:, half:].astype(BF16).astype(F32), U32)
    return hi | (lo >> 16)


def _unpack_rows(p):
    hi = lax.bitcast_convert_type(p & jnp.uint32(0xFFFF0000), F32)
    lo = lax.bitcast_convert_type(p << 16, F32)
    return hi, lo


def _merge_kernel(x_ref, oa_ref, ob_ref, oc_ref, cg_ref, ga_ref, gb_ref, gc_ref, g1_ref, sh2_ref, sc2_ref,
                  gn_ref, nf_ref, wb_ref, wo_ref, rw_ref, rb_ref,
                  xo_ref, hp_ref, ri_ref, rwt_ref, meta_ref, cnt_ref):
    first = (pl.program_id(0) == 0) & (pl.program_id(1) == 0)

    @pl.when(first)
    def _():
        cnt_ref[...] = jnp.zeros_like(cnt_ref)

    oc = oc_ref[0]
    cgv = cg_ref[0].astype(F32)
    gn = gn_ref[...]
    parts = []
    for h in range(4):
        o = oc[:, h * LANES:(h + 1) * LANES]
        mu = jnp.mean(o, axis=-1, keepdims=True)
        dlt = o - mu
        var = jnp.mean(dlt * dlt, axis=-1, keepdims=True)
        y = dlt * lax.rsqrt(var + EPS) * gn[:, h * LANES:(h + 1) * LANES]
        cgh = cgv[:, h * LANES:(h + 1) * LANES]
        parts.append((y * (cgh * jax.nn.sigmoid(cgh))).astype(BF16))
    ocn = jnp.concatenate(parts, axis=1)

    y = jax.nn.sigmoid(ga_ref[0].astype(F32)) * jnp.dot(oa_ref[0], wb_ref[0], preferred_element_type=F32)
    y += jax.nn.sigmoid(gb_ref[0].astype(F32)) * jnp.dot(ob_ref[0], wb_ref[1], preferred_element_type=F32)
    y += jax.nn.sigmoid(gc_ref[0].astype(F32)) * jnp.dot(ocn, wb_ref[2], preferred_element_type=F32)
    y2 = jnp.dot(y.astype(BF16), wo_ref[...], preferred_element_type=F32)
    x = x_ref[0] + g1_ref[0] * y2
    xo_ref[0] = x

    ms = jnp.mean(x * x, axis=-1, keepdims=True)
    h2 = x * lax.rsqrt(ms + EPS) * nf_ref[...]
    h2 = h2 * (1.0 + sc2_ref[0]) + sh2_ref[0]
    hp_ref[...] = _pack_rows(h2)

    nt_dims = (((1,), (1,)), ((), ()))
    logits = lax.dot_general(rw_ref[...], h2.astype(BF16), nt_dims, preferred_element_type=F32)
    sc = jax.nn.sigmoid(logits)
    bi = sc + rb_ref[...]
    ei_i = lax.broadcasted_iota(I32, (N_EXPERTS, 1), 0)
    ei = ei_i.astype(F32)
    epg = N_EXPERTS // N_GROUPS
    egroup = (ei_i >> 2).astype(F32)
    gsum = []
    for g in range(N_GROUPS):
        r = [bi[epg * g + j:epg * g + j + 1, :] for j in range(epg)]
        m = r[0] + r[1]
        for a in range(epg):
            for c in range(a + 1, epg):
                if (a, c) != (0, 1):
                    m = jnp.maximum(m, r[a] + r[c])
        gsum.append(m)
    best = jnp.zeros_like(gsum[0])
    bsc = gsum[0]
    for g in range(1, N_GROUPS):
        upd = gsum[g] > bsc
        best = jnp.where(upd, float(g), best)
        bsc = jnp.where(upd, gsum[g], bsc)
    neg = -jnp.inf
    masked = jnp.where(egroup == best, bi, neg)
    m1 = jnp.max(masked, axis=0, keepdims=True)
    i1 = jnp.min(jnp.where(masked == m1, ei, float(N_EXPERTS)), axis=0, keepdims=True)
    masked2 = jnp.where(ei == i1, neg, masked)
    m2 = jnp.max(masked2, axis=0, keepdims=True)
    i2 = jnp.min(jnp.where(masked2 == m2, ei, float(N_EXPERTS)), axis=0, keepdims=True)
    sel1 = ei == i1
    sel2 = ei == i2
    w1 = jnp.sum(jnp.where(sel1, sc, 0.0), axis=0, keepdims=True)
    w2 = jnp.sum(jnp.where(sel2, sc, 0.0), axis=0, keepdims=True)
    ws = w1 + w2
    rwt_ref[0, 0:1, :] = w1 / ws
    rwt_ref[0, 1:2, :] = w2 / ws

    oh = (sel1 | sel2).astype(F32)
    si = lax.broadcasted_iota(I32, (TM, TM), 0)
    ti = lax.broadcasted_iota(I32, (TM, TM), 1)
    upper = (si < ti).astype(BF16)
    prefix = jnp.dot(oh.astype(BF16), upper, preferred_element_type=F32)
    n_col = jnp.sum(oh, axis=1, keepdims=True)
    seg_len = jnp.floor((n_col + (SEG_ALIGN - 1.0)) * (1.0 / SEG_ALIGN)) * SEG_ALIGN
    seg_len_b = jnp.broadcast_to(seg_len, (N_EXPERTS, LANES))
    er = lax.broadcasted_iota(I32, (N_EXPERTS, N_EXPERTS), 0)
    ec = lax.broadcasted_iota(I32, (N_EXPERTS, N_EXPERTS), 1)
    lower = (ec < er).astype(BF16)
    seg_start_b = jnp.dot(lower, seg_len_b.astype(BF16), preferred_element_type=F32)
    local = prefix + seg_start_b[:, 0:1]
    pos1 = jnp.sum(jnp.where(sel1, local, 0.0), axis=0, keepdims=True)
    pos2 = jnp.sum(jnp.where(sel2, local, 0.0), axis=0, keepdims=True)
    ri_ref[0, 0:1, :] = i1.astype(I32)
    ri_ref[0, 1:2, :] = i2.astype(I32)
    ri_ref[0, 2:3, :] = pos1.astype(I32)
    ri_ref[0, 3:4, :] = pos2.astype(I32)
    meta_ref[0, 0:N_EXPERTS, :] = seg_start_b.astype(I32)
    meta_ref[0, N_EXPERTS:2 * N_EXPERTS, :] = seg_len_b.astype(I32)
    meta_ref[0, 2 * N_EXPERTS:, :] = cnt_ref[...].astype(I32)
    cnt_ref[...] = cnt_ref[...] + seg_len_b


def _merge(xa, p, oa, ob, oc, mods, gn_g, nf_g, wb_bf, wo_bf, rw_t, rb_col, t0):
    b, ntok, _ = xa.shape
    nt = ntok // TM - t0
    ntiles = b * nt
    mod_row = lambda bi, t: jnp.where(t + t0 == 0, b, bi)
    tok = lambda bi, t: (bi, t + t0, 0)
    flat = lambda bi, t: (bi * nt + t, 0, 0)
    outs = pl.pallas_call(
        _merge_kernel,
        grid=(b, nt),
        in_specs=[
            pl.BlockSpec((1, TM, D_MODEL), tok),
            pl.BlockSpec((1, TM, 4 * LANES), lambda bi, t: (bi, t, 0)),
            pl.BlockSpec((1, TM, 4 * LANES), lambda bi, t: (bi, t, 0)),
            pl.BlockSpec((1, TM, 4 * LANES), tok),
            pl.BlockSpec((1, TM, 4 * LANES), lambda bi, t: (bi, t + t0, BLK_CG // 4)),
            pl.BlockSpec((1, TM, D_MODEL), lambda bi, t: (bi, t + t0, BLK_GATE // 8)),
            pl.BlockSpec((1, TM, D_MODEL), lambda bi, t: (bi, t + t0, BLK_GATE // 8 + 1)),
            pl.BlockSpec((1, TM, D_MODEL), lambda bi, t: (bi, t + t0, BLK_GATE // 8 + 2)),
            pl.BlockSpec((1, 1, D_MODEL), lambda bi, t: (mod_row(bi, t), 0, 2)),
            pl.BlockSpec((1, 1, D_MODEL), lambda bi, t: (mod_row(bi, t), 0, 3)),
            pl.BlockSpec((1, 1, D_MODEL), lambda bi, t: (mod_row(bi, t), 0, 4)),
            pl.BlockSpec((1, 4 * LANES), lambda bi, t: (0, 0)),
            pl.BlockSpec((1, D_MODEL), lambda bi, t: (0, 0)),
            pl.BlockSpec((3, 4 * LANES, D_MODEL), lambda bi, t: (0, 0, 0)),
            pl.BlockSpec((D_MODEL, D_MODEL), lambda bi, t: (0, 0)),
            pl.BlockSpec((N_EXPERTS, D_MODEL), lambda bi, t: (0, 0)),
            pl.BlockSpec((N_EXPERTS, 1), lambda bi, t: (0, 0)),
        ],
        out_specs=[
            pl.BlockSpec((1, TM, D_MODEL), tok),
            pl.BlockSpec((TM, D_MODEL // 2), lambda bi, t: (bi * nt + t, 0)),
            pl.BlockSpec((1, 4, TM), flat),
            pl.BlockSpec((1, 2, TM), flat),
            pl.BlockSpec((1, 3 * N_EXPERTS, LANES), flat),
            pl.BlockSpec((N_EXPERTS, LANES), lambda bi, t: (0, 0)),
        ],
        out_shape=[
            jax.ShapeDtypeStruct(xa.shape, F32),
            jax.ShapeDtypeStruct((ntiles * TM, D_MODEL // 2), U32),
            jax.ShapeDtypeStruct((ntiles, 4, TM), I32),
            jax.ShapeDtypeStruct((ntiles, 2, TM), F32),
            jax.ShapeDtypeStruct((ntiles, 3 * N_EXPERTS, LANES), I32),
            jax.ShapeDtypeStruct((N_EXPERTS, LANES), F32),
        ],
        input_output_aliases={0: 0},
        compiler_params=_cparams(("arbitrary", "arbitrary")),
    )(xa, oa, ob, oc, p, p, p, p, mods, mods, mods, gn_g.reshape(1, -1), nf_g.reshape(1, -1),
      wb_bf, wo_bf, rw_t, rb_col)
    return outs


def _segment_copies(tile, start_s, len_s, dst_s, make_copy, wait):
    for e in range(N_EXPERTS):
        n = len_s[tile * N_EXPERTS + e]
        a = start_s[tile * N_EXPERTS + e]
        d = dst_s[tile * N_EXPERTS + e]
        for size in SEG_SIZES:
            @pl.when((n & size) != 0)
            def _(n=n, a=a, d=d, size=size):
                off = n & (-2 * size)
                cp = make_copy(pl.multiple_of(a + off, SEG_ALIGN), pl.multiple_of(d + off, SEG_ALIGN), size)
                if wait:
                    cp.wait()
                else:
                    cp.start()


def _sort_matrix(ri_ref):
    srow = lax.broadcasted_iota(I32, (SORT_ROWS, TM), 0)
    return srow == ri_ref[0, 2:3, :], srow == ri_ref[0, 3:4, :]


def _dispatch_kernel(start_s, len_s, dst_s, h_ref, ri_ref, rw_ref, xs_in_ref, xs_ref, sb, sems):
    del xs_in_ref
    i = pl.program_id(0)
    last = pl.num_programs(0) - 1
    slot = lax.rem(i, 2)

    hi, lo = _unpack_rows(h_ref[...])
    hb = jnp.concatenate([hi.astype(BF16), lo.astype(BF16)], axis=1)
    m1, m2 = _sort_matrix(ri_ref)
    perm = jnp.where(m1 | m2, 1.0, 0.0).astype(BF16)
    rows = jnp.dot(perm, hb, preferred_element_type=F32)
    w = jnp.sum(jnp.where(m1, rw_ref[0, 0:1, :], 0.0) + jnp.where(m2, rw_ref[0, 1:2, :], 0.0),
                axis=1, keepdims=True)
    sb[slot, :, 0:D_MODEL // 2] = _pack_rows(rows)
    sb[slot, :, D_MODEL // 2:] = lax.bitcast_convert_type(jnp.broadcast_to(w, (SORT_ROWS, LANES)), U32)

    def copy_for(slot_k):
        def make_copy(local_row, global_row, size):
            return pltpu.make_async_copy(sb.at[slot_k, pl.ds(local_row, size)],
                                         xs_ref.at[pl.ds(global_row, size)], sems.at[slot_k])
        return make_copy

    for k in range(2):
        @pl.when(slot == k)
        def _(k=k):
            _segment_copies(i, start_s, len_s, dst_s, copy_for(k), wait=False)

            @pl.when(i > 0)
            def _():
                _segment_copies(i - 1, start_s, len_s, dst_s, copy_for(1 - k), wait=True)

            @pl.when(i == last)
            def _():
                _segment_copies(i, start_s, len_s, dst_s, copy_for(k), wait=True)


def _dispatch(hp, ri, rwt, seg_start, seg_len, seg_dst, n_rows):
    ntiles = ri.shape[0]
    xs0 = jnp.zeros((n_rows, XS_COLS), U32)
    grid_spec = pltpu.PrefetchScalarGridSpec(
        num_scalar_prefetch=3,
        grid=(ntiles,),
        in_specs=[
            pl.BlockSpec((TM, D_MODEL // 2), lambda i, *_: (i, 0)),
            pl.BlockSpec((1, 4, TM), lambda i, *_: (i, 0, 0)),
            pl.BlockSpec((1, 2, TM), lambda i, *_: (i, 0, 0)),
            pl.BlockSpec(memory_space=pl.ANY),
        ],
        out_specs=pl.BlockSpec(memory_space=pl.ANY),
        scratch_shapes=[pltpu.VMEM((2, SORT_ROWS, XS_COLS), U32), pltpu.SemaphoreType.DMA((2,))],
    )
    return pl.pallas_call(
        _dispatch_kernel,
        grid_spec=grid_spec,
        out_shape=jax.ShapeDtypeStruct((n_rows, XS_COLS), U32),
        input_output_aliases={6: 0},
        compiler_params=_cparams(("arbitrary",)),
    )(seg_start, seg_len, seg_dst, hp, ri, rwt, xs0)


def _ffn_kernel(te_ref, tv_ref, xs_ref, wg_ref, wu_ref, wd_ref, ys_ref, wgb, wub, wdb):
    i = pl.program_id(0)
    prev = te_ref[jnp.maximum(i - 1, 0)]
    fresh = (i == 0) | (te_ref[i] != prev)

    @pl.when(fresh)
    def _():
        wgb[...] = wg_ref[0].astype(BF16)
        wub[...] = wu_ref[0].astype(BF16)
        wdb[...] = wd_ref[0].astype(BF16)

    @pl.when(tv_ref[i] > 0)
    def _():
        hi, lo = _unpack_rows(xs_ref[:, 0:D_MODEL // 2])
        xb = jnp.concatenate([hi.astype(BF16), lo.astype(BF16)], axis=1)
        w = lax.bitcast_convert_type(xs_ref[:, D_MODEL // 2:D_MODEL // 2 + 1], F32)
        g = jnp.dot(xb, wgb[...], preferred_element_type=F32)
        u = jnp.dot(xb, wub[...], preferred_element_type=F32)
        he = (g * jax.nn.sigmoid(g) * u * w).astype(BF16)
        ys_ref[...] = _pack_rows(jnp.dot(he, wdb[...], preferred_element_type=F32))

    @pl.when(tv_ref[i] == 0)
    def _():
        ys_ref[...] = jnp.zeros_like(ys_ref)


def _ffn(xs, tile_e, tile_v, wg, wu, wd):
    n_rows = xs.shape[0]
    n_tiles = n_rows // TMF
    grid_spec = pltpu.PrefetchScalarGridSpec(
        num_scalar_prefetch=2,
        grid=(n_tiles,),
        in_specs=[
            pl.BlockSpec((TMF, XS_COLS), lambda i, te, tv: (i, 0)),
            pl.BlockSpec((1, D_MODEL, D_EXPERT), lambda i, te, tv: (te[i], 0, 0)),
            pl.BlockSpec((1, D_MODEL, D_EXPERT), lambda i, te, tv: (te[i], 0, 0)),
            pl.BlockSpec((1, D_EXPERT, D_MODEL), lambda i, te, tv: (te[i], 0, 0)),
        ],
        out_specs=pl.BlockSpec((TMF, D_MODEL // 2), lambda i, te, tv: (i, 0)),
        scratch_shapes=[
            pltpu.VMEM((D_MODEL, D_EXPERT), BF16),
            pltpu.VMEM((D_MODEL, D_EXPERT), BF16),
            pltpu.VMEM((D_EXPERT, D_MODEL), BF16),
        ],
    )
    return pl.pallas_call(
        _ffn_kernel,
        grid_spec=grid_spec,
        out_shape=jax.ShapeDtypeStruct((n_rows, D_MODEL // 2), U32),
        compiler_params=_cparams(("arbitrary",)),
    )(tile_e, tile_v, xs, wg, wu, wd)


def _combine_kernel(start_s, len_s, dst_s, x_ref, ri_ref, g2_ref, fg_ref, ys_ref, o_ref, yb, sems, *, final):
    nt = pl.num_programs(1)
    i = pl.program_id(0) * nt + pl.program_id(1)
    last = pl.num_programs(0) * nt - 1
    slot = lax.rem(i, 2)

    def copy_for(slot_k):
        def make_copy(local_row, global_row, size):
            return pltpu.make_async_copy(ys_ref.at[pl.ds(global_row, size)],
                                         yb.at[slot_k, pl.ds(local_row, size)], sems.at[slot_k])
        return make_copy

    @pl.when(i == 0)
    def _():
        yb[...] = jnp.zeros_like(yb)
        _segment_copies(i, start_s, len_s, dst_s, copy_for(0), wait=False)

    for k in range(2):
        @pl.when(slot == k)
        def _(k=k):
            @pl.when(i < last)
            def _():
                _segment_copies(i + 1, start_s, len_s, dst_s, copy_for(1 - k), wait=False)

            _segment_copies(i, start_s, len_s, dst_s, copy_for(k), wait=True)

    hi, lo = _unpack_rows(yb[slot])
    m1, m2 = _sort_matrix(ri_ref)
    perm = jnp.where(m1 | m2, 1.0, 0.0).astype(BF16)
    tn_dims = (((0,), (0,)), ((), ()))
    moe = jnp.concatenate(
        [lax.dot_general(perm, hi.astype(BF16), tn_dims, preferred_element_type=F32),
         lax.dot_general(perm, lo.astype(BF16), tn_dims, preferred_element_type=F32)], axis=1)
    x = x_ref[0] + g2_ref[0] * moe
    if final:
        ms = jnp.mean(x * x, axis=-1, keepdims=True)
        x = x * lax.rsqrt(ms + EPS) * fg_ref[...]
    o_ref[0] = x


def _combine(xa, ys, ri, seg_start, seg_len, seg_dst, mods, final_g, t0, final):
    b, ntok, _ = xa.shape
    nt = ntok // TM - t0
    mod_row = lambda bi, t: jnp.where(t + t0 == 0, b, bi)
    if final:
        out_shape = jax.ShapeDtypeStruct((b, nt * TM, D_MODEL), F32)
        out_spec = pl.BlockSpec((1, TM, D_MODEL), lambda bi, t, *_: (bi, t, 0))
        aliases = {}
    else:
        out_shape = jax.ShapeDtypeStruct(xa.shape, F32)
        out_spec = pl.BlockSpec((1, TM, D_MODEL), lambda bi, t, *_: (bi, t + t0, 0))
        aliases = {3: 0}
    grid_spec = pltpu.PrefetchScalarGridSpec(
        num_scalar_prefetch=3,
        grid=(b, nt),
        in_specs=[
            pl.BlockSpec((1, TM, D_MODEL), lambda bi, t, *_: (bi, t + t0, 0)),
            pl.BlockSpec((1, 4, TM), lambda bi, t, *_: (bi * nt + t, 0, 0)),
            pl.BlockSpec((1, 1, D_MODEL), lambda bi, t, *_: (mod_row(bi, t), 0, 5)),
            pl.BlockSpec((1, D_MODEL), lambda bi, t, *_: (0, 0)),
            pl.BlockSpec(memory_space=pl.ANY),
        ],
        out_specs=out_spec,
        scratch_shapes=[pltpu.VMEM((2, SORT_ROWS, D_MODEL // 2), U32), pltpu.SemaphoreType.DMA((2,))],
    )
    return pl.pallas_call(
        functools.partial(_combine_kernel, final=final),
        grid_spec=grid_spec,
        out_shape=out_shape,
        input_output_aliases=aliases,
        compiler_params=_cparams(("arbitrary", "arbitrary")),
    )(seg_start, seg_len, seg_dst, xa, ri, mods, final_g.reshape(1, -1), ys)


def _route_tables(meta, cnt):
    ntiles = meta.shape[0]
    m = meta[:, :, 0].reshape(ntiles, 3, N_EXPERTS)
    seg_start, seg_len, before = m[:, 0], m[:, 1], m[:, 2]
    totals = cnt[:, 0].astype(I32)
    region = ((totals + TMF - 1) // TMF) * TMF
    ends = jnp.cumsum(region)
    seg_dst = (ends - region)[None, :] + before
    max_rows = 2 * ntiles * TM + ntiles * N_EXPERTS * (SEG_ALIGN - 1)
    n_tiles = max_rows // TMF + 1 + N_EXPERTS
    starts = jnp.arange(n_tiles, dtype=I32) * TMF
    tile_e = jnp.minimum(jnp.sum(starts[:, None] >= ends[None, :], axis=1), N_EXPERTS - 1).astype(I32)
    valid = starts < ends[-1]
    last_e = tile_e[jnp.maximum(ends[-1] // TMF - 1, 0)]
    tile_e = jnp.where(valid, tile_e, last_e)
    flat = lambda v: v.reshape(-1).astype(I32)
    return flat(seg_start), flat(seg_len), flat(seg_dst), tile_e, valid.astype(I32), n_tiles * TMF


def _rope_tables(n_lat):
    quarter = HEAD_DIM // 4
    inv = ROPE_BASE ** (-jnp.arange(quarter, dtype=F32) / quarter)
    n_rows = n_lat // GRID_W
    rows = jnp.repeat(jnp.arange(n_rows, dtype=F32), GRID_W)
    cols = jnp.tile(jnp.arange(GRID_W, dtype=F32), n_rows)
    ang = jnp.concatenate([rows[:, None] * inv, cols[:, None] * inv], axis=-1)
    cos = jnp.cos(ang)
    sin = jnp.sin(ang)
    cos_t = jnp.tile(cos, (1, LANES // cos.shape[1]))
    sin_t = jnp.tile(jnp.concatenate([-sin, sin], axis=-1), (1, LANES // HEAD_DIM))
    cos_t = jnp.concatenate([jnp.ones((CTX_LEN, LANES), F32), cos_t], axis=0)
    sin_t = jnp.concatenate([jnp.zeros((CTX_LEN, LANES), F32), sin_t], axis=0)
    return cos_t, sin_t


def kernel(x, c, ctx, c_ctx, w_ada, b_ada, norm_mix_g, norm_ffn_g, w_in, diff_lambda, diff_subln_g, swa_sink,
           ret_decay, ret_gn_g, w_branch, w_out, router_w, router_b, w_exp_gate, w_exp_up, w_exp_down, final_g):
    b, n_lat, _ = x.shape
    depth = w_ada.shape[0]
    assert ctx.shape[1] == CTX_LEN == TM and n_lat % TM == 0 and n_lat >= 2 * TM

    pad_rows = (-(b + 1)) % 8
    c_all = jnp.concatenate([c, c_ctx[None, :], jnp.zeros((pad_rows, D_MODEL), F32)], axis=0)
    mods_all = _ada(c_all, w_ada, b_ada)
    cos_t, sin_t = _rope_tables(n_lat)
    xa = jnp.concatenate([ctx, x], axis=1)
    rw_t = router_w.T.astype(BF16)
    rb_col = router_b.reshape(-1, 1).astype(F32)

    for l in range(depth):
        last = l == depth - 1
        t0 = 1 if last else 0
        lam_init = 0.8 - 0.6 * math.exp(-0.3 * l)
        mods = mods_all[l].reshape(mods_all.shape[1], 1, -1)
        p = _proj(xa, mods, norm_mix_g[l], cos_t, sin_t, w_in[l].astype(BF16))
        oa = _diff(p, diff_lambda[l], diff_subln_g[l], lam_init, t0)
        ob = _win(p, swa_sink[l], t0)
        oc = _ret(p, ret_decay[l])
        xa, hp, ri, rwt, meta, cnt = _merge(xa, p, oa, ob, oc, mods, ret_gn_g[l], norm_ffn_g[l],
                                      w_branch[l].astype(BF16), w_out[l].astype(BF16), rw_t, rb_col, t0)
        seg_start, seg_len, seg_dst, tile_e, tile_v, n_rows = _route_tables(meta, cnt)
        xs = _dispatch(hp, ri, rwt, seg_start, seg_len, seg_dst, n_rows)
        ys = _ffn(xs, tile_e, tile_v, w_exp_gate[l], w_exp_up[l], w_exp_down[l])
        xa = _combine(xa, ys, ri, seg_start, seg_len, seg_dst, mods, final_g, t0, last)
    return xa
```

```python
import functools
import math

import jax
import jax.numpy as jnp
from jax import lax
from jax.experimental import pallas as pl
from jax.experimental.pallas import tpu as pltpu

F32 = jnp.float32
BF16 = jnp.bfloat16
U32 = jnp.uint32
I32 = jnp.int32

D_MODEL = 1024
CTX_LEN = 256
GRID_W = 64
ROPE_BASE = 10000.0
EPS = 1e-6
LOG2E = 1.4426950408889634
HEAD_DIM = 64
WINDOW = 128
RT_CHUNK = 128
N_EXPERTS = 16
N_GROUPS = 4
D_EXPERT = 512
LANES = 128
TM = 256
DIFF_HEADS = 4
TMF = 512
SEG_ALIGN = 8
SEG_SIZES = (256, 128, 64, 32, 16, 8)
SORT_ROWS = 640
XS_COLS = D_MODEL // 2 + 128
D_IN = 6912
N_SRC_BLK = D_IN // LANES
N_OUT_BLK = N_SRC_BLK + 2
VMEM_LIMIT = 56 * 1024 * 1024

BLK_AQ, BLK_AK, BLK_AV = 0, 4, 8
BLK_BQ, BLK_BK, BLK_BV = 12, 16, 18
BLK_CQ, BLK_CK, BLK_CV, BLK_CG, BLK_GATE = 20, 22, 24, 28, 32


def _cparams(sem):
    return pltpu.CompilerParams(dimension_semantics=sem, vmem_limit_bytes=VMEM_LIMIT)


def _sigmoid(x):
    return 0.5 * jnp.tanh(0.5 * x) + 0.5


def _ada_kernel(c_ref, w_ref, b_ref, o_ref):
    c = c_ref[...]
    s = (c * jax.nn.sigmoid(c)).astype(BF16)
    o_ref[0] = jnp.dot(s, w_ref[0].astype(BF16), preferred_element_type=F32) + b_ref[0]


def _ada(c_all, w_ada, b_ada):
    depth = w_ada.shape[0]
    rows = c_all.shape[0]
    nj = w_ada.shape[2] // D_MODEL
    return pl.pallas_call(
        _ada_kernel,
        grid=(depth, nj),
        in_specs=[
            pl.BlockSpec((rows, D_MODEL), lambda l, j: (0, 0)),
            pl.BlockSpec((1, D_MODEL, D_MODEL), lambda l, j: (l, 0, j)),
            pl.BlockSpec((1, 1, D_MODEL), lambda l, j: (l, 0, j)),
        ],
        out_specs=pl.BlockSpec((1, rows, D_MODEL), lambda l, j: (l, 0, j)),
        out_shape=jax.ShapeDtypeStruct((depth, rows, w_ada.shape[2]), F32),
        compiler_params=_cparams(("arbitrary", "arbitrary")),
    )(c_all, w_ada, b_ada.reshape(depth, 1, -1))


def _rope(a, cos, sin_signed, first_half):
    rot = jnp.where(first_half, pltpu.roll(a, LANES - HEAD_DIM // 2, 1), pltpu.roll(a, HEAD_DIM // 2, 1))
    return a * cos + rot * sin_signed


def _token_tile(ctx_ref, x_ref, tile):
    if ctx_ref is None:
        return x_ref[0]
    return jnp.where(tile == 0, ctx_ref[0], x_ref[0])


def _token_specs(ctx, xa, t0):
    if ctx is None:
        return [xa], [pl.BlockSpec((1, TM, D_MODEL), lambda bi, t, *_: (bi, t + t0, 0))]
    return [ctx, xa], [pl.BlockSpec((1, TM, D_MODEL), lambda bi, t, *_: (bi, 0, 0)),
                       pl.BlockSpec((1, TM, D_MODEL), lambda bi, t, *_: (bi, jnp.maximum(t + t0 - 1, 0), 0))]


def _proj_kernel(*refs, split):
    ctx_ref, refs = (refs[0], refs[1:]) if split else (None, refs)
    x_ref, sh_ref, sc_ref, g_ref, cos_ref, sin_ref, w_ref, o_ref = refs
    x = _token_tile(ctx_ref, x_ref, pl.program_id(1))
    ms = jnp.mean(x * x, axis=-1, keepdims=True)
    h = x * lax.rsqrt(ms + EPS) * g_ref[...]
    h = h * (1.0 + sc_ref[0]) + sh_ref[0]
    hb = h.astype(BF16)
    cos = cos_ref[...]
    sin = sin_ref[...]
    lane = lax.broadcasted_iota(I32, (1, LANES), 1)
    first_half = (lane & (HEAD_DIM - 1)) < (HEAD_DIM // 2)
    low = lane < HEAD_DIM
    scale = HEAD_DIM ** -0.5
    for c in range(N_SRC_BLK // 2):
        acc = jnp.dot(hb, w_ref[:, c * 2 * LANES:(c + 1) * 2 * LANES], preferred_element_type=F32)
        for half in range(2):
            src = 2 * c + half
            a = acc[:, half * LANES:(half + 1) * LANES]
            if src < 8 or 12 <= src < 17:
                a = _rope(a, cos, sin, first_half)
            if src < 4 or 12 <= src < 16:
                a = a * (scale * LOG2E)
            if 20 <= src < 22:
                a = a * scale
            if src in (16, 17):
                sw = pltpu.roll(a, HEAD_DIM, 1)
                dst = BLK_BK if src == 16 else BLK_BV
                o_ref[0, :, dst * LANES:(dst + 1) * LANES] = jnp.where(low, a, sw).astype(BF16)
                o_ref[0, :, (dst + 1) * LANES:(dst + 2) * LANES] = jnp.where(low, sw, a).astype(BF16)
            else:
                dst = src if src < 16 else src + 2
                o_ref[0, :, dst * LANES:(dst + 1) * LANES] = a.astype(BF16)


def _proj(ctx, xa, mods, norm_g, cos_t, sin_t, w_in_bf):
    b = xa.shape[0]
    ntok = xa.shape[1] + (0 if ctx is None else CTX_LEN)
    nt = ntok // TM
    mod_row = lambda bi, t: jnp.where(t == 0, b, bi)
    tok_args, tok_specs = _token_specs(ctx, xa, 0)
    return pl.pallas_call(
        functools.partial(_proj_kernel, split=ctx is not None),
        grid=(b, nt),
        in_specs=tok_specs + [
            pl.BlockSpec((1, 1, D_MODEL), lambda bi, t: (mod_row(bi, t), 0, 0)),
            pl.BlockSpec((1, 1, D_MODEL), lambda bi, t: (mod_row(bi, t), 0, 1)),
            pl.BlockSpec((1, D_MODEL), lambda bi, t: (0, 0)),
            pl.BlockSpec((TM, LANES), lambda bi, t: (t, 0)),
            pl.BlockSpec((TM, LANES), lambda bi, t: (t, 0)),
            pl.BlockSpec((D_MODEL, D_IN), lambda bi, t: (0, 0)),
        ],
        out_specs=pl.BlockSpec((1, TM, N_OUT_BLK * LANES), lambda bi, t: (bi, t, 0)),
        out_shape=jax.ShapeDtypeStruct((b, ntok, N_OUT_BLK * LANES), BF16),
        compiler_params=_cparams(("arbitrary", "arbitrary")),
    )(*tok_args, mods, mods, norm_g.reshape(1, -1), cos_t, sin_t, w_in_bf)


def _diff_kernel(lam_ref, g_ref, q_ref, k_ref, v_ref, o_ref, *, lam_init, t0, ntok):
    t = pl.program_id(2) + t0
    lv = lam_ref[...]
    lam = (jnp.exp(jnp.sum(lv[0:1] * lv[1:2], axis=-1, keepdims=True))
           - jnp.exp(jnp.sum(lv[2:3] * lv[3:4], axis=-1, keepdims=True)) + lam_init)
    lane = lax.broadcasted_iota(I32, (1, LANES), 1)
    nt_dims = (((1,), (1,)), ((), ()))

    def attend(nk):
        def scores(hd):
            cols = slice(hd * LANES, (hd + 1) * LANES)
            q = q_ref[0, :, cols]
            zero = jnp.zeros_like(q)
            q0 = jnp.where(lane < HEAD_DIM, q, zero)
            q1 = jnp.where(lane >= HEAD_DIM, q, zero)
            k = k_ref[0, 0:nk, cols]
            return (lax.dot_general(q0, k, nt_dims, preferred_element_type=F32),
                    lax.dot_general(q1, k, nt_dims, preferred_element_type=F32))

        nxt = scores(0)
        for hd in range(DIFF_HEADS):
            cols = slice(hd * LANES, (hd + 1) * LANES)
            v = v_ref[0, 0:nk, cols]
            s0, s1 = nxt
            if hd + 1 < DIFF_HEADS:
                nxt = scores(hd + 1)
            e0 = jnp.exp2(s0 - jnp.max(s0, axis=-1, keepdims=True))
            e1 = jnp.exp2(s1 - jnp.max(s1, axis=-1, keepdims=True))
            r0 = 1.0 / jnp.sum(e0, axis=-1, keepdims=True)
            r1 = lam / jnp.sum(e1, axis=-1, keepdims=True)
            a = e0 * r0 - e1 * r1
            o = jnp.dot(a.astype(BF16), v, preferred_element_type=F32)
            o = o * lax.rsqrt(jnp.mean(o * o, axis=-1, keepdims=True) + EPS) * g_ref[...] * (1.0 - lam_init)
            o_ref[0, :, cols] = o.astype(BF16)

    if t0 == 0:
        @pl.when(t == 0)
        def _():
            attend(CTX_LEN)

        @pl.when(t > 0)
        def _():
            attend(ntok)
    else:
        attend(ntok)


def _diff(p, lam_vec, subln_g, lam_init, t0):
    b, ntok, _ = p.shape
    nt = ntok // TM - t0
    heads = 4
    hw = DIFF_HEADS * LANES
    return pl.pallas_call(
        functools.partial(_diff_kernel, lam_init=lam_init, t0=t0, ntok=ntok),
        grid=(b, heads // DIFF_HEADS, nt),
        in_specs=[
            pl.BlockSpec((4, HEAD_DIM), lambda bi, h, t: (0, 0)),
            pl.BlockSpec((1, LANES), lambda bi, h, t: (0, 0)),
            pl.BlockSpec((1, TM, hw), lambda bi, h, t: (bi, t + t0, BLK_AQ // DIFF_HEADS + h)),
            pl.BlockSpec((1, ntok, hw), lambda bi, h, t: (bi, 0, BLK_AK // DIFF_HEADS + h)),
            pl.BlockSpec((1, ntok, hw), lambda bi, h, t: (bi, 0, BLK_AV // DIFF_HEADS + h)),
        ],
        out_specs=pl.BlockSpec((1, TM, hw), lambda bi, h, t: (bi, t, h)),
        out_shape=jax.ShapeDtypeStruct((b, nt * TM, heads * LANES), BF16),
        compiler_params=_cparams(("arbitrary", "arbitrary", "arbitrary")),
    )(lam_vec, subln_g.reshape(1, -1), p, p, p)


def _win_kernel(sink_ref, q_ref, k_ref, v_ref, o_ref, *, t0, ntok):
    t = pl.program_id(1) + t0
    lane = lax.broadcasted_iota(I32, (1, LANES), 1)
    low = lane < HEAD_DIM
    nt_dims = (((1,), (1,)), ((), ()))
    wk = 2 * TM
    start = jnp.clip(t * TM - WINDOW, CTX_LEN, ntok - wk)
    start = pl.multiple_of(start, WINDOW)
    qpos = t * TM + lax.broadcasted_iota(I32, (TM, wk), 0)
    kpos = start + lax.broadcasted_iota(I32, (TM, wk), 1)
    in_band = (jnp.abs(kpos - qpos) <= WINDOW) & (t > 0)
    bias = jnp.concatenate([jnp.zeros((TM, CTX_LEN), F32), jnp.where(in_band, 0.0, -jnp.inf).astype(F32)], axis=1)
    n_heads = 8
    group = n_heads // 2
    keys, vals = [], []
    for g in range(2):
        cols = slice(g * LANES, (g + 1) * LANES)
        keys.append(jnp.concatenate([k_ref[0, 0:CTX_LEN, cols], k_ref[0, pl.ds(start, wk), cols]], axis=0))
        vals.append(jnp.concatenate([v_ref[0, 0:CTX_LEN, cols], v_ref[0, pl.ds(start, wk), cols]], axis=0))

    def scores(h):
        qb = q_ref[0, :, (h // 2) * LANES:(h // 2 + 1) * LANES]
        qm = jnp.where(low if h % 2 == 0 else jnp.logical_not(low), qb, jnp.zeros_like(qb))
        return lax.dot_general(qm, keys[h // group], nt_dims, preferred_element_type=F32)

    nxt = scores(0)
    outs = []
    for h in range(n_heads):
        s = nxt + bias
        if h + 1 < n_heads:
            nxt = scores(h + 1)
        sk = sink_ref[h] * LOG2E
        m = jnp.maximum(jnp.max(s, axis=-1, keepdims=True), sk)
        e = jnp.exp2(s - m)
        l = jnp.sum(e, axis=-1, keepdims=True) + jnp.exp2(sk - m)
        o = jnp.dot(e.astype(BF16), vals[h // group], preferred_element_type=F32)
        outs.append(o * (1.0 / l))
        if h % 2 == 1:
            o_ref[0, :, (h // 2) * LANES:(h // 2 + 1) * LANES] = jnp.where(low, outs[h - 1], outs[h]).astype(BF16)


def _win(p, sink, t0):
    b, ntok, _ = p.shape
    nt = ntok // TM - t0
    return pl.pallas_call(
        functools.partial(_win_kernel, t0=t0, ntok=ntok),
        grid=(b, nt),
        in_specs=[
            pl.BlockSpec(memory_space=pltpu.SMEM),
            pl.BlockSpec((1, TM, 4 * LANES), lambda bi, t: (bi, t + t0, BLK_BQ // 4)),
            pl.BlockSpec((1, ntok, 2 * LANES), lambda bi, t: (bi, 0, BLK_BK // 2)),
            pl.BlockSpec((1, ntok, 2 * LANES), lambda bi, t: (bi, 0, BLK_BV // 2)),
        ],
        out_specs=pl.BlockSpec((1, TM, 4 * LANES), lambda bi, t: (bi, t, 0)),
        out_shape=jax.ShapeDtypeStruct((b, nt * TM, 4 * LANES), BF16),
        compiler_params=_cparams(("arbitrary", "arbitrary")),
    )(sink, p, p, p)


def _ret_kernel(dec_ref, q_ref, k_ref, v_ref, o_ref, ob_ref, *, ntok):
    T = RT_CHUNK
    nchunk = ntok // T
    nctx = CTX_LEN // T
    x = dec_ref[...]
    lg_all = -(jnp.maximum(-x, 0.0) + jnp.log1p(jnp.exp(-jnp.abs(x))))
    lane = lax.broadcasted_iota(I32, (1, LANES), 1)
    low = lane < HEAD_DIM
    row_low = lax.broadcasted_iota(I32, (LANES, 1), 0) < HEAD_DIM
    pi = lax.broadcasted_iota(I32, (T, T), 0).astype(F32)
    pj = lax.broadcasted_iota(I32, (T, T), 1).astype(F32)
    pcol = lax.broadcasted_iota(I32, (T, 1), 0).astype(F32)
    nt_dims = (((1,), (1,)), ((), ()))
    tn_dims = (((0,), (0,)), ((), ()))

    decay, qdp, kdp, cdp = [], [], [], []
    for direction in range(2):
        lgs = [lg_all[direction:direction + 1, h:h + 1] for h in range(4)]
        if direction == 0:
            dist = pi - pj
            qd = [jnp.exp((pcol + 1.0) * lg) for lg in lgs]
            kd = [jnp.exp((T - 1.0 - pcol) * lg) for lg in lgs]
        else:
            dist = pj - pi
            qd = [jnp.exp((T - pcol) * lg) for lg in lgs]
            kd = [jnp.exp(pcol * lg) for lg in lgs]
        decay.append([jnp.where(dist >= 0, jnp.exp(jnp.maximum(dist, 0.0) * lg), 0.0) for lg in lgs])
        qdp.append([jnp.where(low, qd[2 * pr], qd[2 * pr + 1]) for pr in range(2)])
        kdp.append([jnp.where(low, kd[2 * pr], kd[2 * pr + 1]) for pr in range(2)])
        cdp.append([jnp.where(row_low, jnp.exp(T * lgs[2 * pr]), jnp.exp(T * lgs[2 * pr + 1])) for pr in range(2)])

    def step(i, states):
        chunks = (i, jnp.where(i < nctx, nctx - 1 - i, nchunk + nctx - 1 - i))
        new_states = []
        for direction in range(2):
            r0 = pl.multiple_of(chunks[direction] * T, T)
            out_ref = o_ref.at[0] if direction == 0 else ob_ref
            for pair in range(2):
                qb = q_ref[0, pl.ds(r0, T), pair * LANES:(pair + 1) * LANES]
                kb = k_ref[0, pl.ds(r0, T), pair * LANES:(pair + 1) * LANES]
                st = states[2 * direction + pair]
                v2 = v_ref[0, pl.ds(r0, T), pair * 2 * LANES:(pair + 1) * 2 * LANES]
                kbd = (kb.astype(F32) * kdp[direction][pair]).astype(BF16)
                qsb = (qb.astype(F32) * qdp[direction][pair]).astype(BF16)
                stb = st.astype(BF16)
                zq = jnp.zeros_like(qb)
                halves = (low, jnp.logical_not(low))
                q2 = jnp.concatenate([jnp.where(hm, qb, zq) for hm in halves], axis=0)
                inner2 = lax.dot_general(q2, kb, nt_dims, preferred_element_type=F32)
                for sub in range(2):
                    h = 2 * pair + sub
                    inner = inner2[sub * T:(sub + 1) * T] * decay[direction][h]
                    lhs = jnp.concatenate([inner.astype(BF16), jnp.where(halves[sub], qsb, zq)], axis=1)
                    rhs = jnp.concatenate([v2[:, sub * LANES:(sub + 1) * LANES], stb], axis=0)
                    out_ref[pl.ds(r0, T), h * LANES:(h + 1) * LANES] = jnp.dot(
                        lhs, rhs, preferred_element_type=F32)
                kv2 = lax.dot_general(kbd, v2, tn_dims, preferred_element_type=F32)
                new_states.append(st * cdp[direction][pair]
                                  + jnp.where(row_low, kv2[:, 0:LANES], kv2[:, LANES:2 * LANES]))
        return tuple(new_states)

    zero = jnp.zeros((LANES, LANES), F32)
    lax.fori_loop(0, nchunk, step, (zero,) * 4, unroll=2)
    o_ref[0] = o_ref[0] + ob_ref[...]


def _ret(p, ret_decay):
    b, ntok, _ = p.shape
    return pl.pallas_call(
        functools.partial(_ret_kernel, ntok=ntok),
        grid=(b,),
        in_specs=[
            pl.BlockSpec((2, 4), lambda bi: (0, 0)),
            pl.BlockSpec((1, ntok, 2 * LANES), lambda bi: (bi, 0, BLK_CQ // 2)),
            pl.BlockSpec((1, ntok, 2 * LANES), lambda bi: (bi, 0, BLK_CK // 2)),
            pl.BlockSpec((1, ntok, 4 * LANES), lambda bi: (bi, 0, BLK_CV // 4)),
        ],
        out_specs=pl.BlockSpec((1, ntok, 4 * LANES), lambda bi: (bi, 0, 0)),
        out_shape=jax.ShapeDtypeStruct((b, ntok, 4 * LANES), F32),
        scratch_shapes=[pltpu.VMEM((ntok, 4 * LANES), F32)],
        compiler_params=_cparams(("arbitrary",)),
    )(ret_decay, p, p, p)


def _pack_rows(h):
    half = h.shape[1] // 2
    hi = lax.bitcast_convert_type(h[:, :half].astype(BF16).astype(F32), U32)
    lo = lax.bitcast_convert_type(h[:, half:].astype(BF16).astype(F32), U32)
    return hi | (lo >> 16)


def _unpack_rows(p):
    hi = lax.bitcast_convert_type(p & jnp.uint32(0xFFFF0000), F32)
    lo = lax.bitcast_convert_type(p << 16, F32)
    return hi, lo


def _merge_kernel(*refs, split, t0):
    ctx_ref, refs = (refs[0], refs[1:]) if split else (None, refs)
    (x_ref, oa_ref, ob_ref, oc_ref, cg_ref, ga_ref, gb_ref, gc_ref, g1_ref, sh2_ref, sc2_ref,
     gn_ref, nf_ref, wb_ref, wo_ref, rw_ref, rb_ref,
     xo_ref, hp_ref, ri_ref, rwt_ref, meta_ref, cnt_ref) = refs
    first = (pl.program_id(0) == 0) & (pl.program_id(1) == 0)

    @pl.when(first)
    def _():
        cnt_ref[...] = jnp.zeros_like(cnt_ref)

    oc = oc_ref[0]
    cgv = cg_ref[0].astype(F32)
    gn = gn_ref[...]
    parts = []
    for h in range(4):
        o = oc[:, h * LANES:(h + 1) * LANES]
        mu = jnp.mean(o, axis=-1, keepdims=True)
        dlt = o - mu
        var = jnp.mean(dlt * dlt, axis=-1, keepdims=True)
        y = dlt * lax.rsqrt(var + EPS) * gn[:, h * LANES:(h + 1) * LANES]
        cgh = cgv[:, h * LANES:(h + 1) * LANES]
        parts.append((y * (cgh * _sigmoid(cgh))).astype(BF16))
    ocn = jnp.concatenate(parts, axis=1)

    y = _sigmoid(ga_ref[0].astype(F32)) * jnp.dot(oa_ref[0], wb_ref[0], preferred_element_type=F32)
    y += _sigmoid(gb_ref[0].astype(F32)) * jnp.dot(ob_ref[0], wb_ref[1], preferred_element_type=F32)
    y += _sigmoid(gc_ref[0].astype(F32)) * jnp.dot(ocn, wb_ref[2], preferred_element_type=F32)
    y2 = jnp.dot(y.astype(BF16), wo_ref[...], preferred_element_type=F32)
    x = _token_tile(ctx_ref, x_ref, pl.program_id(1) + t0) + g1_ref[0] * y2
    xo_ref[0] = x

    ms = jnp.mean(x * x, axis=-1, keepdims=True)
    h2 = x * lax.rsqrt(ms + EPS) * nf_ref[...]
    h2 = h2 * (1.0 + sc2_ref[0]) + sh2_ref[0]
    hp_ref[...] = _pack_rows(h2)

    nt_dims = (((1,), (1,)), ((), ()))
    logits = lax.dot_general(rw_ref[...], h2.astype(BF16), nt_dims, preferred_element_type=F32)
    sc = jax.nn.sigmoid(logits)
    bi = sc + rb_ref[...]
    ei_i = lax.broadcasted_iota(I32, (N_EXPERTS, 1), 0)
    ei = ei_i.astype(F32)
    epg = N_EXPERTS // N_GROUPS
    egroup = (ei_i >> 2).astype(F32)
    gsum = []
    for g in range(N_GROUPS):
        r = [bi[epg * g + j:epg * g + j + 1, :] for j in range(epg)]
        m = r[0] + r[1]
        for a in range(epg):
            for c in range(a + 1, epg):
                if (a, c) != (0, 1):
                    m = jnp.maximum(m, r[a] + r[c])
        gsum.append(m)
    best = jnp.zeros_like(gsum[0])
    bsc = gsum[0]
    for g in range(1, N_GROUPS):
        upd = gsum[g] > bsc
        best = jnp.where(upd, float(g), best)
        bsc = jnp.where(upd, gsum[g], bsc)
    neg = -jnp.inf
    masked = jnp.where(egroup == best, bi, neg)
    m1 = jnp.max(masked, axis=0, keepdims=True)
    i1 = jnp.min(jnp.where(masked == m1, ei, float(N_EXPERTS)), axis=0, keepdims=True)
    masked2 = jnp.where(ei == i1, neg, masked)
    m2 = jnp.max(masked2, axis=0, keepdims=True)
    i2 = jnp.min(jnp.where(masked2 == m2, ei, float(N_EXPERTS)), axis=0, keepdims=True)
    sel1 = ei == i1
    sel2 = ei == i2
    w1 = jnp.sum(jnp.where(sel1, sc, 0.0), axis=0, keepdims=True)
    w2 = jnp.sum(jnp.where(sel2, sc, 0.0), axis=0, keepdims=True)
    ws = w1 + w2
    rwt_ref[0, 0:1, :] = w1 / ws
    rwt_ref[0, 1:2, :] = w2 / ws

    oh = (sel1 | sel2).astype(F32)
    si = lax.broadcasted_iota(I32, (TM, TM), 0)
    ti = lax.broadcasted_iota(I32, (TM, TM), 1)
    upper = (si < ti).astype(BF16)
    prefix = jnp.dot(oh.astype(BF16), upper, preferred_element_type=F32)
    n_col = jnp.sum(oh, axis=1, keepdims=True)
    seg_len = jnp.floor((n_col + (SEG_ALIGN - 1.0)) * (1.0 / SEG_ALIGN)) * SEG_ALIGN
    seg_len_b = jnp.broadcast_to(seg_len, (N_EXPERTS, LANES))
    er = lax.broadcasted_iota(I32, (N_EXPERTS, N_EXPERTS), 0)
    ec = lax.broadcasted_iota(I32, (N_EXPERTS, N_EXPERTS), 1)
    lower = (ec < er).astype(BF16)
    seg_start_b = jnp.dot(lower, seg_len_b.astype(BF16), preferred_element_type=F32)
    local = prefix + seg_start_b[:, 0:1]
    pos1 = jnp.sum(jnp.where(sel1, local, 0.0), axis=0, keepdims=True)
    pos2 = jnp.sum(jnp.where(sel2, local, 0.0), axis=0, keepdims=True)
    ri_ref[0, 0:1, :] = i1.astype(I32)
    ri_ref[0, 1:2, :] = i2.astype(I32)
    ri_ref[0, 2:3, :] = pos1.astype(I32)
    ri_ref[0, 3:4, :] = pos2.astype(I32)
    meta_ref[0, 0:N_EXPERTS, :] = seg_start_b.astype(I32)
    meta_ref[0, N_EXPERTS:2 * N_EXPERTS, :] = seg_len_b.astype(I32)
    meta_ref[0, 2 * N_EXPERTS:, :] = cnt_ref[...].astype(I32)
    cnt_ref[...] = cnt_ref[...] + seg_len_b


def _merge(ctx, xa, p, oa, ob, oc, mods, gn_g, nf_g, wb_bf, wo_bf, rw_t, rb_col, t0):
    b, ntok, _ = p.shape
    nt = ntok // TM - t0
    ntiles = b * nt
    mod_row = lambda bi, t: jnp.where(t + t0 == 0, b, bi)
    tok = lambda bi, t: (bi, t + t0, 0)
    flat = lambda bi, t: (bi * nt + t, 0, 0)
    tok_args, tok_specs = _token_specs(ctx, xa, t0)
    outs = pl.pallas_call(
        functools.partial(_merge_kernel, split=ctx is not None, t0=t0),
        grid=(b, nt),
        in_specs=tok_specs + [
            pl.BlockSpec((1, TM, 4 * LANES), lambda bi, t: (bi, t, 0)),
            pl.BlockSpec((1, TM, 4 * LANES), lambda bi, t: (bi, t, 0)),
            pl.BlockSpec((1, TM, 4 * LANES), tok),
            pl.BlockSpec((1, TM, 4 * LANES), lambda bi, t: (bi, t + t0, BLK_CG // 4)),
            pl.BlockSpec((1, TM, D_MODEL), lambda bi, t: (bi, t + t0, BLK_GATE // 8)),
            pl.BlockSpec((1, TM, D_MODEL), lambda bi, t: (bi, t + t0, BLK_GATE // 8 + 1)),
            pl.BlockSpec((1, TM, D_MODEL), lambda bi, t: (bi, t + t0, BLK_GATE // 8 + 2)),
            pl.BlockSpec((1, 1, D_MODEL), lambda bi, t: (mod_row(bi, t), 0, 2)),
            pl.BlockSpec((1, 1, D_MODEL), lambda bi, t: (mod_row(bi, t), 0, 3)),
            pl.BlockSpec((1, 1, D_MODEL), lambda bi, t: (mod_row(bi, t), 0, 4)),
            pl.BlockSpec((1, 4 * LANES), lambda bi, t: (0, 0)),
            pl.BlockSpec((1, D_MODEL), lambda bi, t: (0, 0)),
            pl.BlockSpec((3, 4 * LANES, D_MODEL), lambda bi, t: (0, 0, 0)),
            pl.BlockSpec((D_MODEL, D_MODEL), lambda bi, t: (0, 0)),
            pl.BlockSpec((N_EXPERTS, D_MODEL), lambda bi, t: (0, 0)),
            pl.BlockSpec((N_EXPERTS, 1), lambda bi, t: (0, 0)),
        ],
        out_specs=[
            pl.BlockSpec((1, TM, D_MODEL), tok),
            pl.BlockSpec((TM, D_MODEL // 2), lambda bi, t: (bi * nt + t, 0)),
            pl.BlockSpec((1, 4, TM), flat),
            pl.BlockSpec((1, 2, TM), flat),
            pl.BlockSpec((1, 3 * N_EXPERTS, LANES), flat),
            pl.BlockSpec((N_EXPERTS, LANES), lambda bi, t: (0, 0)),
        ],
        out_shape=[
            jax.ShapeDtypeStruct((b, ntok, D_MODEL), F32),
            jax.ShapeDtypeStruct((ntiles * TM, D_MODEL // 2), U32),
            jax.ShapeDtypeStruct((ntiles, 4, TM), I32),
            jax.ShapeDtypeStruct((ntiles, 2, TM), F32),
            jax.ShapeDtypeStruct((ntiles, 3 * N_EXPERTS, LANES), I32),
            jax.ShapeDtypeStruct((N_EXPERTS, LANES), F32),
        ],
        input_output_aliases={} if ctx is not None else {0: 0},
        compiler_params=_cparams(("arbitrary", "arbitrary")),
    )(*tok_args, oa, ob, oc, p, p, p, p, mods, mods, mods, gn_g.reshape(1, -1), nf_g.reshape(1, -1),
      wb_bf, wo_bf, rw_t, rb_col)
    return outs


def _segment_copies(tile, start_s, len_s, dst_s, make_copy, wait):
    for e in range(N_EXPERTS):
        n = len_s[tile * N_EXPERTS + e]
        a = start_s[tile * N_EXPERTS + e]
        d = dst_s[tile * N_EXPERTS + e]
        for size in SEG_SIZES:
            @pl.when((n & size) != 0)
            def _(n=n, a=a, d=d, size=size):
                off = n & (-2 * size)
                cp = make_copy(pl.multiple_of(a + off, SEG_ALIGN), pl.multiple_of(d + off, SEG_ALIGN), size)
                if wait:
                    cp.wait()
                else:
                    cp.start()


def _sort_matrix(ri_ref):
    srow = lax.broadcasted_iota(I32, (SORT_ROWS, TM), 0)
    return srow == ri_ref[0, 2:3, :], srow == ri_ref[0, 3:4, :]


def _dispatch_kernel(start_s, len_s, dst_s, h_ref, ri_ref, rw_ref, xs_ref, sb, zb, sems):
    i = pl.program_id(0)
    last = pl.num_programs(0) - 1
    slot = lax.rem(i, 2)

    hi, lo = _unpack_rows(h_ref[...])
    hb = jnp.concatenate([hi.astype(BF16), lo.astype(BF16)], axis=1)
    m1, m2 = _sort_matrix(ri_ref)
    perm = jnp.where(m1 | m2, 1.0, 0.0).astype(BF16)
    rows = jnp.dot(perm, hb, preferred_element_type=F32)
    w = jnp.sum(jnp.where(m1, rw_ref[0, 0:1, :], 0.0) + jnp.where(m2, rw_ref[0, 1:2, :], 0.0),
                axis=1, keepdims=True)
    sb[slot, :, 0:D_MODEL // 2] = _pack_rows(rows)
    sb[slot, :, D_MODEL // 2:] = lax.bitcast_convert_type(jnp.broadcast_to(w, (SORT_ROWS, LANES)), U32)

    def copy_for(slot_k):
        def make_copy(local_row, global_row, size):
            return pltpu.make_async_copy(sb.at[slot_k, pl.ds(local_row, size)],
                                         xs_ref.at[pl.ds(global_row, size)], sems.at[slot_k])
        return make_copy

    for k in range(2):
        @pl.when(slot == k)
        def _(k=k):
            _segment_copies(i, start_s, len_s, dst_s, copy_for(k), wait=False)

            @pl.when(i > 0)
            def _():
                _segment_copies(i - 1, start_s, len_s, dst_s, copy_for(1 - k), wait=True)

            @pl.when(i == last)
            def _():
                _segment_copies(i, start_s, len_s, dst_s, copy_for(k), wait=True)

    @pl.when(i == last)
    def _():
        zb[...] = jnp.zeros_like(zb)

        def tail_copy(local_row, global_row, size):
            del local_row
            return pltpu.make_async_copy(zb.at[pl.ds(0, size)], xs_ref.at[pl.ds(global_row, size)], sems.at[0])

        _segment_copies(last + 1, start_s, len_s, dst_s, tail_copy, wait=False)
        _segment_copies(last + 1, start_s, len_s, dst_s, tail_copy, wait=True)


def _dispatch(hp, ri, rwt, seg_start, seg_len, seg_dst, n_rows):
    ntiles = ri.shape[0]
    grid_spec = pltpu.PrefetchScalarGridSpec(
        num_scalar_prefetch=3,
        grid=(ntiles,),
        in_specs=[
            pl.BlockSpec((TM, D_MODEL // 2), lambda i, *_: (i, 0)),
            pl.BlockSpec((1, 4, TM), lambda i, *_: (i, 0, 0)),
            pl.BlockSpec((1, 2, TM), lambda i, *_: (i, 0, 0)),
        ],
        out_specs=pl.BlockSpec(memory_space=pl.ANY),
        scratch_shapes=[pltpu.VMEM((2, SORT_ROWS, XS_COLS), U32), pltpu.VMEM((SEG_SIZES[0], XS_COLS), U32),
                        pltpu.SemaphoreType.DMA((2,))],
    )
    return pl.pallas_call(
        _dispatch_kernel,
        grid_spec=grid_spec,
        out_shape=jax.ShapeDtypeStruct((n_rows, XS_COLS), U32),
        compiler_params=_cparams(("arbitrary",)),
    )(seg_start, seg_len, seg_dst, hp, ri, rwt)


def _ffn_kernel(te_ref, tv_ref, ts_ref, xs_ref, wg_ref, wu_ref, wd_ref, ys_ref, wgb, wub, wdb):
    del ts_ref
    i = pl.program_id(0)
    prev = te_ref[jnp.maximum(i - 1, 0)]
    fresh = (i == 0) | (te_ref[i] != prev)

    @pl.when(fresh)
    def _():
        wgb[...] = wg_ref[0, 0].astype(BF16)
        wub[...] = wu_ref[0, 0].astype(BF16)
        wdb[...] = wd_ref[0, 0].astype(BF16)

    @pl.when(tv_ref[i] > 0)
    def _():
        hi, lo = _unpack_rows(xs_ref[:, 0:D_MODEL // 2])
        xb = jnp.concatenate([hi.astype(BF16), lo.astype(BF16)], axis=1)
        w = lax.bitcast_convert_type(xs_ref[:, D_MODEL // 2:D_MODEL // 2 + 1], F32)
        g = jnp.dot(xb, wgb[...], preferred_element_type=F32)
        u = jnp.dot(xb, wub[...], preferred_element_type=F32)
        he = (g * _sigmoid(g) * u * w).astype(BF16)
        ys_ref[...] = _pack_rows(jnp.dot(he, wdb[...], preferred_element_type=F32))

    @pl.when(tv_ref[i] == 0)
    def _():
        ys_ref[...] = jnp.zeros_like(ys_ref)


def _ffn(xs, tile_e, tile_v, tile_src, wg, wu, wd, layer):
    n_rows = xs.shape[0]
    n_tiles = n_rows // TMF
    grid_spec = pltpu.PrefetchScalarGridSpec(
        num_scalar_prefetch=3,
        grid=(n_tiles,),
        in_specs=[
            pl.BlockSpec((TMF, XS_COLS), lambda i, te, tv, ts: (ts[i], 0)),
            pl.BlockSpec((1, 1, D_MODEL, D_EXPERT), lambda i, te, tv, ts: (layer, te[i], 0, 0)),
            pl.BlockSpec((1, 1, D_MODEL, D_EXPERT), lambda i, te, tv, ts: (layer, te[i], 0, 0)),
            pl.BlockSpec((1, 1, D_EXPERT, D_MODEL), lambda i, te, tv, ts: (layer, te[i], 0, 0)),
        ],
        out_specs=pl.BlockSpec((TMF, D_MODEL // 2), lambda i, te, tv, ts: (i, 0)),
        scratch_shapes=[
            pltpu.VMEM((D_MODEL, D_EXPERT), BF16),
            pltpu.VMEM((D_MODEL, D_EXPERT), BF16),
            pltpu.VMEM((D_EXPERT, D_MODEL), BF16),
        ],
    )
    return pl.pallas_call(
        _ffn_kernel,
        grid_spec=grid_spec,
        out_shape=jax.ShapeDtypeStruct((n_rows, D_MODEL // 2), U32),
        compiler_params=_cparams(("arbitrary",)),
    )(tile_e, tile_v, tile_src, xs, wg, wu, wd)


def _combine_kernel(start_s, len_s, dst_s, x_ref, ri_ref, g2_ref, fg_ref, ys_ref, o_ref, yb, sems, *, final):
    nt = pl.num_programs(1)
    i = pl.program_id(0) * nt + pl.program_id(1)
    last = pl.num_programs(0) * nt - 1
    slot = lax.rem(i, 2)

    def copy_for(slot_k):
        def make_copy(local_row, global_row, size):
            return pltpu.make_async_copy(ys_ref.at[pl.ds(global_row, size)],
                                         yb.at[slot_k, pl.ds(local_row, size)], sems.at[slot_k])
        return make_copy

    @pl.when(i == 0)
    def _():
        yb[...] = jnp.zeros_like(yb)
        _segment_copies(i, start_s, len_s, dst_s, copy_for(0), wait=False)

    for k in range(2):
        @pl.when(slot == k)
        def _(k=k):
            @pl.when(i < last)
            def _():
                _segment_copies(i + 1, start_s, len_s, dst_s, copy_for(1 - k), wait=False)

            _segment_copies(i, start_s, len_s, dst_s, copy_for(k), wait=True)

    hi, lo = _unpack_rows(yb[slot])
    m1, m2 = _sort_matrix(ri_ref)
    perm = jnp.where(m1 | m2, 1.0, 0.0).astype(BF16)
    tn_dims = (((0,), (0,)), ((), ()))
    moe = jnp.concatenate(
        [lax.dot_general(perm, hi.astype(BF16), tn_dims, preferred_element_type=F32),
         lax.dot_general(perm, lo.astype(BF16), tn_dims, preferred_element_type=F32)], axis=1)
    x = x_ref[0] + g2_ref[0] * moe
    if final:
        ms = jnp.mean(x * x, axis=-1, keepdims=True)
        x = x * lax.rsqrt(ms + EPS) * fg_ref[...]
    o_ref[0] = x


def _combine(xa, ys, ri, seg_start, seg_len, seg_dst, mods, final_g, t0, final):
    b, ntok, _ = xa.shape
    nt = ntok // TM - t0
    mod_row = lambda bi, t: jnp.where(t + t0 == 0, b, bi)
    if final:
        out_shape = jax.ShapeDtypeStruct((b, nt * TM, D_MODEL), F32)
        out_spec = pl.BlockSpec((1, TM, D_MODEL), lambda bi, t, *_: (bi, t, 0))
        aliases = {}
    else:
        out_shape = jax.ShapeDtypeStruct(xa.shape, F32)
        out_spec = pl.BlockSpec((1, TM, D_MODEL), lambda bi, t, *_: (bi, t + t0, 0))
        aliases = {3: 0}
    grid_spec = pltpu.PrefetchScalarGridSpec(
        num_scalar_prefetch=3,
        grid=(b, nt),
        in_specs=[
            pl.BlockSpec((1, TM, D_MODEL), lambda bi, t, *_: (bi, t + t0, 0)),
            pl.BlockSpec((1, 4, TM), lambda bi, t, *_: (bi * nt + t, 0, 0)),
            pl.BlockSpec((1, 1, D_MODEL), lambda bi, t, *_: (mod_row(bi, t), 0, 5)),
            pl.BlockSpec((1, D_MODEL), lambda bi, t, *_: (0, 0)),
            pl.BlockSpec(memory_space=pl.ANY),
        ],
        out_specs=out_spec,
        scratch_shapes=[pltpu.VMEM((2, SORT_ROWS, D_MODEL // 2), U32), pltpu.SemaphoreType.DMA((2,))],
    )
    return pl.pallas_call(
        functools.partial(_combine_kernel, final=final),
        grid_spec=grid_spec,
        out_shape=out_shape,
        input_output_aliases=aliases,
        compiler_params=_cparams(("arbitrary", "arbitrary")),
    )(seg_start, seg_len, seg_dst, xa, ri, mods, final_g.reshape(1, -1), ys)


def _route_tables(meta, cnt):
    ntiles = meta.shape[0]
    m = meta[:, :, 0].reshape(ntiles, 3, N_EXPERTS)
    seg_start, seg_len, before = m[:, 0], m[:, 1], m[:, 2]
    totals = cnt[:, 0].astype(I32)
    region = ((totals + TMF - 1) // TMF) * TMF
    ends = jnp.cumsum(region)
    seg_dst = (ends - region)[None, :] + before
    seg_start = jnp.concatenate([seg_start, jnp.zeros((1, N_EXPERTS), I32)], axis=0)
    seg_len = jnp.concatenate([seg_len, (region - totals)[None, :]], axis=0)
    seg_dst = jnp.concatenate([seg_dst, (ends - region + totals)[None, :]], axis=0)
    max_rows = 2 * ntiles * TM + ntiles * N_EXPERTS * (SEG_ALIGN - 1)
    n_tiles = max_rows // TMF + 1 + N_EXPERTS
    tile_id = jnp.arange(n_tiles, dtype=I32)
    starts = tile_id * TMF
    tile_e = jnp.minimum(jnp.sum(starts[:, None] >= ends[None, :], axis=1), N_EXPERTS - 1).astype(I32)
    valid = starts < ends[-1]
    last_used = jnp.maximum(ends[-1] // TMF - 1, 0)
    tile_e = jnp.where(valid, tile_e, tile_e[last_used])
    tile_src = jnp.minimum(tile_id, last_used)
    flat = lambda v: v.reshape(-1).astype(I32)
    return flat(seg_start), flat(seg_len), flat(seg_dst), tile_e, valid.astype(I32), tile_src, n_tiles * TMF


def _rope_tables(n_lat):
    quarter = HEAD_DIM // 4
    inv = ROPE_BASE ** (-jnp.arange(quarter, dtype=F32) / quarter)
    n_rows = n_lat // GRID_W
    rows = jnp.repeat(jnp.arange(n_rows, dtype=F32), GRID_W)
    cols = jnp.tile(jnp.arange(GRID_W, dtype=F32), n_rows)
    ang = jnp.concatenate([rows[:, None] * inv, cols[:, None] * inv], axis=-1)
    cos = jnp.cos(ang)
    sin = jnp.sin(ang)
    cos_t = jnp.tile(cos, (1, LANES // cos.shape[1]))
    sin_t = jnp.tile(jnp.concatenate([-sin, sin], axis=-1), (1, LANES // HEAD_DIM))
    cos_t = jnp.concatenate([jnp.ones((CTX_LEN, LANES), F32), cos_t], axis=0)
    sin_t = jnp.concatenate([jnp.zeros((CTX_LEN, LANES), F32), sin_t], axis=0)
    return cos_t, sin_t


def kernel(x, c, ctx, c_ctx, w_ada, b_ada, norm_mix_g, norm_ffn_g, w_in, diff_lambda, diff_subln_g, swa_sink,
           ret_decay, ret_gn_g, w_branch, w_out, router_w, router_b, w_exp_gate, w_exp_up, w_exp_down, final_g):
    b, n_lat, _ = x.shape
    depth = w_ada.shape[0]
    assert ctx.shape[1] == CTX_LEN == TM and n_lat % TM == 0 and n_lat >= 2 * TM

    pad_rows = (-(b + 1)) % 8
    c_all = jnp.concatenate([c, c_ctx[None, :], jnp.zeros((pad_rows, D_MODEL), F32)], axis=0)
    mods_all = _ada(c_all, w_ada, b_ada)
    cos_t, sin_t = _rope_tables(n_lat)
    xa = x
    rw_t = router_w.T.astype(BF16)
    rb_col = router_b.reshape(-1, 1).astype(F32)

    for l in range(depth):
        last = l == depth - 1
        t0 = 1 if last else 0
        lam_init = 0.8 - 0.6 * math.exp(-0.3 * l)
        mods = mods_all[l].reshape(mods_all.shape[1], 1, -1)
        ctx_in = ctx if l == 0 else None
        p = _proj(ctx_in, xa, mods, norm_mix_g[l], cos_t, sin_t, w_in[l].astype(BF16))
        oa = _diff(p, diff_lambda[l], diff_subln_g[l], lam_init, t0)
        ob = _win(p, swa_sink[l], t0)
        oc = _ret(p, ret_decay[l])
        xa, hp, ri, rwt, meta, cnt = _merge(ctx_in, xa, p, oa, ob, oc, mods, ret_gn_g[l], norm_ffn_g[l],
                                            w_branch[l].astype(BF16), w_out[l].astype(BF16), rw_t, rb_col, t0)
        seg_start, seg_len, seg_dst, tile_e, tile_v, tile_src, n_rows = _route_tables(meta, cnt)
        xs = _dispatch(hp, ri, rwt, seg_start, seg_len, seg_dst, n_rows)
        ys = _ffn(xs, tile_e, tile_v, tile_src, w_exp_gate, w_exp_up, w_exp_down, l)
        xa = _combine(xa, ys, ri, seg_start, seg_len, seg_dst, mods, final_g, t0, last)
    return xa
```

```python
import functools
import math

import jax
import jax.numpy as jnp
from jax import lax
from jax.experimental import pallas as pl
from jax.experimental.pallas import tpu as pltpu

F32 = jnp.float32
BF16 = jnp.bfloat16
U32 = jnp.uint32
I32 = jnp.int32

D_MODEL = 1024
CTX_LEN = 256
GRID_W = 64
ROPE_BASE = 10000.0
EPS = 1e-6
LOG2E = 1.4426950408889634
HEAD_DIM = 64
WINDOW = 128
RT_CHUNK = 128
N_EXPERTS = 16
N_GROUPS = 4
D_EXPERT = 512
LANES = 128
TM = 256
DIFF_HEADS = 4
TMF = 512
SEG_ALIGN = 8
SEG_SIZES = (256, 128, 64, 32, 16, 8)
SORT_ROWS = 640
XS_COLS = D_MODEL // 2 + 128
D_IN = 6912
N_SRC_BLK = D_IN // LANES
N_OUT_BLK = N_SRC_BLK + 2
VMEM_LIMIT = 56 * 1024 * 1024

BLK_AQ, BLK_AK, BLK_AV = 0, 4, 8
BLK_BQ, BLK_BK, BLK_BV = 12, 16, 18
BLK_CQ, BLK_CK, BLK_CV, BLK_CG, BLK_GATE = 20, 22, 24, 28, 32


def _cparams(sem):
    return pltpu.CompilerParams(dimension_semantics=sem, vmem_limit_bytes=VMEM_LIMIT)


def _ada_kernel(c_ref, w_ref, b_ref, o_ref):
    c = c_ref[...]
    s = (c * jax.nn.sigmoid(c)).astype(BF16)
    o_ref[0] = jnp.dot(s, w_ref[0].astype(BF16), preferred_element_type=F32) + b_ref[0]


def _ada(c_all, w_ada, b_ada):
    depth = w_ada.shape[0]
    rows = c_all.shape[0]
    nj = w_ada.shape[2] // D_MODEL
    return pl.pallas_call(
        _ada_kernel,
        grid=(depth, nj),
        in_specs=[
            pl.BlockSpec((rows, D_MODEL), lambda l, j: (0, 0)),
            pl.BlockSpec((1, D_MODEL, D_MODEL), lambda l, j: (l, 0, j)),
            pl.BlockSpec((1, 1, D_MODEL), lambda l, j: (l, 0, j)),
        ],
        out_specs=pl.BlockSpec((1, rows, D_MODEL), lambda l, j: (l, 0, j)),
        out_shape=jax.ShapeDtypeStruct((depth, rows, w_ada.shape[2]), F32),
        compiler_params=_cparams(("arbitrary", "arbitrary")),
    )(c_all, w_ada, b_ada.reshape(depth, 1, -1))


def _rope(a, cos, sin_signed, first_half):
    rot = jnp.where(first_half, pltpu.roll(a, LANES - HEAD_DIM // 2, 1), pltpu.roll(a, HEAD_DIM // 2, 1))
    return a * cos + rot * sin_signed


def _token_tile(ctx_ref, x_ref, tile):
    if ctx_ref is None:
        return x_ref[0]
    return jnp.where(tile == 0, ctx_ref[0], x_ref[0])


def _token_specs(ctx, xa, t0):
    if ctx is None:
        return [xa], [pl.BlockSpec((1, TM, D_MODEL), lambda bi, t, *_: (bi, t + t0, 0))]
    return [ctx, xa], [pl.BlockSpec((1, TM, D_MODEL), lambda bi, t, *_: (bi, 0, 0)),
                       pl.BlockSpec((1, TM, D_MODEL), lambda bi, t, *_: (bi, jnp.maximum(t + t0 - 1, 0), 0))]


def _proj_kernel(*refs, split):
    ctx_ref, refs = (refs[0], refs[1:]) if split else (None, refs)
    x_ref, sh_ref, sc_ref, g_ref, cos_ref, sin_ref, w_ref, o_ref = refs
    x = _token_tile(ctx_ref, x_ref, pl.program_id(1))
    ms = jnp.mean(x * x, axis=-1, keepdims=True)
    h = x * lax.rsqrt(ms + EPS) * g_ref[...]
    h = h * (1.0 + sc_ref[0]) + sh_ref[0]
    hb = h.astype(BF16)
    cos = cos_ref[...]
    sin = sin_ref[...]
    lane = lax.broadcasted_iota(I32, (1, LANES), 1)
    first_half = (lane & (HEAD_DIM - 1)) < (HEAD_DIM // 2)
    low = lane < HEAD_DIM
    scale = HEAD_DIM ** -0.5
    for c in range(N_SRC_BLK // 2):
        acc = jnp.dot(hb, w_ref[:, c * 2 * LANES:(c + 1) * 2 * LANES], preferred_element_type=F32)
        for half in range(2):
            src = 2 * c + half
            a = acc[:, half * LANES:(half + 1) * LANES]
            if src < 8 or 12 <= src < 17:
                a = _rope(a, cos, sin, first_half)
            if src < 4 or 12 <= src < 16:
                a = a * (scale * LOG2E)
            if 20 <= src < 22:
                a = a * scale
            if src >= 26:
                a = a * 0.5
            if src in (16, 17):
                sw = pltpu.roll(a, HEAD_DIM, 1)
                dst = BLK_BK if src == 16 else BLK_BV
                o_ref[0, :, dst * LANES:(dst + 1) * LANES] = jnp.where(low, a, sw).astype(BF16)
                o_ref[0, :, (dst + 1) * LANES:(dst + 2) * LANES] = jnp.where(low, sw, a).astype(BF16)
            else:
                dst = src if src < 16 else src + 2
                o_ref[0, :, dst * LANES:(dst + 1) * LANES] = a.astype(BF16)


def _proj(ctx, xa, mods, norm_g, cos_t, sin_t, w_in_bf):
    b = xa.shape[0]
    ntok = xa.shape[1] + (0 if ctx is None else CTX_LEN)
    nt = ntok // TM
    mod_row = lambda bi, t: jnp.where(t == 0, b, bi)
    tok_args, tok_specs = _token_specs(ctx, xa, 0)
    return pl.pallas_call(
        functools.partial(_proj_kernel, split=ctx is not None),
        grid=(b, nt),
        in_specs=tok_specs + [
            pl.BlockSpec((1, 1, D_MODEL), lambda bi, t: (mod_row(bi, t), 0, 0)),
            pl.BlockSpec((1, 1, D_MODEL), lambda bi, t: (mod_row(bi, t), 0, 1)),
            pl.BlockSpec((1, D_MODEL), lambda bi, t: (0, 0)),
            pl.BlockSpec((TM, LANES), lambda bi, t: (t, 0)),
            pl.BlockSpec((TM, LANES), lambda bi, t: (t, 0)),
            pl.BlockSpec((D_MODEL, D_IN), lambda bi, t: (0, 0)),
        ],
        out_specs=pl.BlockSpec((1, TM, N_OUT_BLK * LANES), lambda bi, t: (bi, t, 0)),
        out_shape=jax.ShapeDtypeStruct((b, ntok, N_OUT_BLK * LANES), BF16),
        compiler_params=_cparams(("arbitrary", "arbitrary")),
    )(*tok_args, mods, mods, norm_g.reshape(1, -1), cos_t, sin_t, w_in_bf)


def _diff_kernel(lam_ref, g_ref, q_ref, k_ref, v_ref, o_ref, *, lam_init, t0, ntok):
    t = pl.program_id(2) + t0
    lv = lam_ref[...]
    lam = (jnp.exp(jnp.sum(lv[0:1] * lv[1:2], axis=-1, keepdims=True))
           - jnp.exp(jnp.sum(lv[2:3] * lv[3:4], axis=-1, keepdims=True)) + lam_init)
    lane = lax.broadcasted_iota(I32, (1, LANES), 1)
    nt_dims = (((1,), (1,)), ((), ()))

    def attend(nk):
        def scores(hd):
            cols = slice(hd * LANES, (hd + 1) * LANES)
            q = q_ref[0, :, cols]
            zero = jnp.zeros_like(q)
            q0 = jnp.where(lane < HEAD_DIM, q, zero)
            q1 = jnp.where(lane >= HEAD_DIM, q, zero)
            k = k_ref[0, 0:nk, cols]
            return (lax.dot_general(q0, k, nt_dims, preferred_element_type=F32),
                    lax.dot_general(q1, k, nt_dims, preferred_element_type=F32))

        ones_blk = jnp.where(lax.broadcasted_iota(I32, (nk, LANES), 1) == 0, 1.0, 0.0).astype(BF16)
        nxt = scores(0)
        for hd in range(DIFF_HEADS):
            cols = slice(hd * LANES, (hd + 1) * LANES)
            v1 = jnp.concatenate([v_ref[0, 0:nk, cols], ones_blk], axis=1)
            s0, s1 = nxt
            if hd + 1 < DIFF_HEADS:
                nxt = scores(hd + 1)
            e0 = jnp.exp2(s0 - jnp.max(s0, axis=-1, keepdims=True)).astype(BF16)
            e1 = jnp.exp2(s1 - jnp.max(s1, axis=-1, keepdims=True)).astype(BF16)
            ov0 = jnp.dot(e0, v1, preferred_element_type=F32)
            ov1 = jnp.dot(e1, v1, preferred_element_type=F32)
            r0 = 1.0 / ov0[:, LANES:LANES + 1]
            r1 = lam / ov1[:, LANES:LANES + 1]
            o = ov0[:, 0:LANES] * r0 - ov1[:, 0:LANES] * r1
            o = o * lax.rsqrt(jnp.mean(o * o, axis=-1, keepdims=True) + EPS) * g_ref[...] * (1.0 - lam_init)
            o_ref[0, :, cols] = o.astype(BF16)

    if t0 == 0:
        @pl.when(t == 0)
        def _():
            attend(CTX_LEN)

        @pl.when(t > 0)
        def _():
            attend(ntok)
    else:
        attend(ntok)


def _diff(p, lam_vec, subln_g, lam_init, t0):
    b, ntok, _ = p.shape
    nt = ntok // TM - t0
    heads = 4
    hw = DIFF_HEADS * LANES
    return pl.pallas_call(
        functools.partial(_diff_kernel, lam_init=lam_init, t0=t0, ntok=ntok),
        grid=(b, heads // DIFF_HEADS, nt),
        in_specs=[
            pl.BlockSpec((4, HEAD_DIM), lambda bi, h, t: (0, 0)),
            pl.BlockSpec((1, LANES), lambda bi, h, t: (0, 0)),
            pl.BlockSpec((1, TM, hw), lambda bi, h, t: (bi, t + t0, BLK_AQ // DIFF_HEADS + h)),
            pl.BlockSpec((1, ntok, hw), lambda bi, h, t: (bi, 0, BLK_AK // DIFF_HEADS + h)),
            pl.BlockSpec((1, ntok, hw), lambda bi, h, t: (bi, 0, BLK_AV // DIFF_HEADS + h)),
        ],
        out_specs=pl.BlockSpec((1, TM, hw), lambda bi, h, t: (bi, t, h)),
        out_shape=jax.ShapeDtypeStruct((b, nt * TM, heads * LANES), BF16),
        compiler_params=_cparams(("arbitrary", "arbitrary", "arbitrary")),
    )(lam_vec, subln_g.reshape(1, -1), p, p, p)


def _win_kernel(sink_ref, q_ref, k_ref, v_ref, o_ref, *, t0, ntok):
    t = pl.program_id(1) + t0
    lane = lax.broadcasted_iota(I32, (1, LANES), 1)
    low = lane < HEAD_DIM
    nt_dims = (((1,), (1,)), ((), ()))
    wk = 2 * TM
    start = jnp.clip(t * TM - WINDOW, CTX_LEN, ntok - wk)
    start = pl.multiple_of(start, WINDOW)
    qpos = t * TM + lax.broadcasted_iota(I32, (TM, wk), 0)
    kpos = start + lax.broadcasted_iota(I32, (TM, wk), 1)
    in_band = (jnp.abs(kpos - qpos) <= WINDOW) & (t > 0)
    bias = jnp.concatenate([jnp.zeros((TM, CTX_LEN), F32), jnp.where(in_band, 0.0, -jnp.inf).astype(F32)], axis=1)
    n_heads = 8
    group = n_heads // 2
    keys, vals = [], []
    ones_blk = jnp.where(lax.broadcasted_iota(I32, (CTX_LEN + wk, LANES), 1) == 0, 1.0, 0.0).astype(BF16)
    for g in range(2):
        cols = slice(g * LANES, (g + 1) * LANES)
        keys.append(jnp.concatenate([k_ref[0, 0:CTX_LEN, cols], k_ref[0, pl.ds(start, wk), cols]], axis=0))
        vg = jnp.concatenate([v_ref[0, 0:CTX_LEN, cols], v_ref[0, pl.ds(start, wk), cols]], axis=0)
        vals.append(jnp.concatenate([vg, ones_blk], axis=1))

    def scores(h):
        qb = q_ref[0, :, (h // 2) * LANES:(h // 2 + 1) * LANES]
        qm = jnp.where(low if h % 2 == 0 else jnp.logical_not(low), qb, jnp.zeros_like(qb))
        return lax.dot_general(qm, keys[h // group], nt_dims, preferred_element_type=F32)

    nxt = scores(0)
    outs = []
    for h in range(n_heads):
        s = nxt + bias
        if h + 1 < n_heads:
            nxt = scores(h + 1)
        sk = sink_ref[h] * LOG2E
        m = jnp.maximum(jnp.max(s, axis=-1, keepdims=True), sk)
        e = jnp.exp2(s - m).astype(BF16)
        ov = jnp.dot(e, vals[h // group], preferred_element_type=F32)
        l = ov[:, LANES:LANES + 1] + jnp.exp2(sk - m)
        outs.append(ov[:, 0:LANES] * (1.0 / l))
        if h % 2 == 1:
            o_ref[0, :, (h // 2) * LANES:(h // 2 + 1) * LANES] = jnp.where(low, outs[h - 1], outs[h]).astype(BF16)


def _win(p, sink, t0):
    b, ntok, _ = p.shape
    nt = ntok // TM - t0
    return pl.pallas_call(
        functools.partial(_win_kernel, t0=t0, ntok=ntok),
        grid=(b, nt),
        in_specs=[
            pl.BlockSpec(memory_space=pltpu.SMEM),
            pl.BlockSpec((1, TM, 4 * LANES), lambda bi, t: (bi, t + t0, BLK_BQ // 4)),
            pl.BlockSpec((1, ntok, 2 * LANES), lambda bi, t: (bi, 0, BLK_BK // 2)),
            pl.BlockSpec((1, ntok, 2 * LANES), lambda bi, t: (bi, 0, BLK_BV // 2)),
        ],
        out_specs=pl.BlockSpec((1, TM, 4 * LANES), lambda bi, t: (bi, t, 0)),
        out_shape=jax.ShapeDtypeStruct((b, nt * TM, 4 * LANES), BF16),
        compiler_params=_cparams(("arbitrary", "arbitrary")),
    )(sink, p, p, p)


def _ret_kernel(dec_ref, q_ref, k_ref, v_ref, o_ref, ob_ref, *, ntok):
    T = RT_CHUNK
    nchunk = ntok // T
    nctx = CTX_LEN // T
    x = dec_ref[...]
    lg_all = -(jnp.maximum(-x, 0.0) + jnp.log1p(jnp.exp(-jnp.abs(x))))
    lane = lax.broadcasted_iota(I32, (1, LANES), 1)
    low = lane < HEAD_DIM
    row_low = lax.broadcasted_iota(I32, (LANES, 1), 0) < HEAD_DIM
    pi = lax.broadcasted_iota(I32, (T, T), 0).astype(F32)
    pj = lax.broadcasted_iota(I32, (T, T), 1).astype(F32)
    pcol = lax.broadcasted_iota(I32, (T, 1), 0).astype(F32)
    nt_dims = (((1,), (1,)), ((), ()))
    tn_dims = (((0,), (0,)), ((), ()))

    decay, qdp, kdp, cdp = [], [], [], []
    for direction in range(2):
        lgs = [lg_all[direction:direction + 1, h:h + 1] for h in range(4)]
        if direction == 0:
            dist = pi - pj
            qd = [jnp.exp((pcol + 1.0) * lg) for lg in lgs]
            kd = [jnp.exp((T - 1.0 - pcol) * lg) for lg in lgs]
        else:
            dist = pj - pi
            qd = [jnp.exp((T - pcol) * lg) for lg in lgs]
            kd = [jnp.exp(pcol * lg) for lg in lgs]
        decay.append([jnp.where(dist >= 0, jnp.exp(jnp.maximum(dist, 0.0) * lg), 0.0) for lg in lgs])
        qdp.append([jnp.where(low, qd[2 * pr], qd[2 * pr + 1]) for pr in range(2)])
        kdp.append([jnp.where(low, kd[2 * pr], kd[2 * pr + 1]) for pr in range(2)])
        cdp.append([jnp.where(row_low, jnp.exp(T * lgs[2 * pr]), jnp.exp(T * lgs[2 * pr + 1])) for pr in range(2)])

    def step(i, states):
        chunks = (i, jnp.where(i < nctx, nctx - 1 - i, nchunk + nctx - 1 - i))
        new_states = []
        for direction in range(2):
            r0 = pl.multiple_of(chunks[direction] * T, T)
            out_ref = o_ref.at[0] if direction == 0 else ob_ref
            for pair in range(2):
                qb = q_ref[0, pl.ds(r0, T), pair * LANES:(pair + 1) * LANES]
                kb = k_ref[0, pl.ds(r0, T), pair * LANES:(pair + 1) * LANES]
                st = states[2 * direction + pair]
                v2 = v_ref[0, pl.ds(r0, T), pair * 2 * LANES:(pair + 1) * 2 * LANES]
                kbd = (kb.astype(F32) * kdp[direction][pair]).astype(BF16)
                qsb = (qb.astype(F32) * qdp[direction][pair]).astype(BF16)
                stb = st.astype(BF16)
                zq = jnp.zeros_like(qb)
                halves = (low, jnp.logical_not(low))
                q2 = jnp.concatenate([jnp.where(hm, qb, zq) for hm in halves], axis=0)
                inner2 = lax.dot_general(q2, kb, nt_dims, preferred_element_type=F32)
                for sub in range(2):
                    h = 2 * pair + sub
                    inner = inner2[sub * T:(sub + 1) * T] * decay[direction][h]
                    lhs = jnp.concatenate([inner.astype(BF16), jnp.where(halves[sub], qsb, zq)], axis=1)
                    rhs = jnp.concatenate([v2[:, sub * LANES:(sub + 1) * LANES], stb], axis=0)
                    out_ref[pl.ds(r0, T), h * LANES:(h + 1) * LANES] = jnp.dot(
                        lhs, rhs, preferred_element_type=F32)
                kv2 = lax.dot_general(kbd, v2, tn_dims, preferred_element_type=F32)
                new_states.append(st * cdp[direction][pair]
                                  + jnp.where(row_low, kv2[:, 0:LANES], kv2[:, LANES:2 * LANES]))
        return tuple(new_states)

    zero = jnp.zeros((LANES, LANES), F32)
    lax.fori_loop(0, nchunk, step, (zero,) * 4, unroll=2)
    o_ref[0] = o_ref[0] + ob_ref[...]


def _ret(p, ret_decay):
    b, ntok, _ = p.shape
    return pl.pallas_call(
        functools.partial(_ret_kernel, ntok=ntok),
        grid=(b,),
        in_specs=[
            pl.BlockSpec((2, 4), lambda bi: (0, 0)),
            pl.BlockSpec((1, ntok, 2 * LANES), lambda bi: (bi, 0, BLK_CQ // 2)),
            pl.BlockSpec((1, ntok, 2 * LANES), lambda bi: (bi, 0, BLK_CK // 2)),
            pl.BlockSpec((1, ntok, 4 * LANES), lambda bi: (bi, 0, BLK_CV // 4)),
        ],
        out_specs=pl.BlockSpec((1, ntok, 4 * LANES), lambda bi: (bi, 0, 0)),
        out_shape=jax.ShapeDtypeStruct((b, ntok, 4 * LANES), F32),
        scratch_shapes=[pltpu.VMEM((ntok, 4 * LANES), F32)],
        compiler_params=_cparams(("arbitrary",)),
    )(ret_decay, p, p, p)


def _pack_rows(h):
    half = h.shape[1] // 2
    hi = lax.bitcast_convert_type(h[:, :half].astype(BF16).astype(F32), U32)
    lo = lax.bitcast_convert_type(h[:, half:].astype(BF16).astype(F32), U32)
    return hi | (lo >> 16)


def _unpack_rows(p):
    hi = lax.bitcast_convert_type(p & jnp.uint32(0xFFFF0000), F32)
    lo = lax.bitcast_convert_type(p << 16, F32)
    return hi, lo


def _merge_kernel(*refs, split, t0):
    ctx_ref, refs = (refs[0], refs[1:]) if split else (None, refs)
    (x_ref, oa_ref, ob_ref, oc_ref, cg_ref, ga_ref, gb_ref, gc_ref, g1_ref, sh2_ref, sc2_ref,
     gn_ref, nf_ref, wb_ref, wo_ref, rw_ref, rb_ref,
     xo_ref, hp_ref, ri_ref, rwt_ref, meta_ref, cnt_ref) = refs
    first = (pl.program_id(0) == 0) & (pl.program_id(1) == 0)

    @pl.when(first)
    def _():
        cnt_ref[...] = jnp.zeros_like(cnt_ref)

    oc = oc_ref[0]
    cgv = cg_ref[0].astype(F32)
    gn = gn_ref[...]
    parts = []
    for h in range(4):
        o = oc[:, h * LANES:(h + 1) * LANES]
        mu = jnp.mean(o, axis=-1, keepdims=True)
        dlt = o - mu
        var = jnp.mean(dlt * dlt, axis=-1, keepdims=True)
        y = dlt * lax.rsqrt(var + EPS) * gn[:, h * LANES:(h + 1) * LANES]
        cgh = cgv[:, h * LANES:(h + 1) * LANES]
        parts.append((y * (cgh * (1.0 + jnp.tanh(cgh)))).astype(BF16))
    ocn = jnp.concatenate(parts, axis=1)

    y = (1.0 + jnp.tanh(ga_ref[0].astype(F32))) * jnp.dot(oa_ref[0], wb_ref[0], preferred_element_type=F32)
    y += (1.0 + jnp.tanh(gb_ref[0].astype(F32))) * jnp.dot(ob_ref[0], wb_ref[1], preferred_element_type=F32)
    y += (1.0 + jnp.tanh(gc_ref[0].astype(F32))) * jnp.dot(ocn, wb_ref[2], preferred_element_type=F32)
    y2 = jnp.dot(y.astype(BF16), wo_ref[...], preferred_element_type=F32)
    x = _token_tile(ctx_ref, x_ref, pl.program_id(1) + t0) + (0.5 * g1_ref[0]) * y2
    xo_ref[0] = x

    ms = jnp.mean(x * x, axis=-1, keepdims=True)
    h2 = x * lax.rsqrt(ms + EPS) * nf_ref[...]
    h2 = h2 * (1.0 + sc2_ref[0]) + sh2_ref[0]
    hp_ref[...] = _pack_rows(h2)

    nt_dims = (((1,), (1,)), ((), ()))
    logits = lax.dot_general(rw_ref[...], h2.astype(BF16), nt_dims, preferred_element_type=F32)
    sc = jax.nn.sigmoid(logits)
    bi = sc + rb_ref[...]
    ei_i = lax.broadcasted_iota(I32, (N_EXPERTS, 1), 0)
    ei = ei_i.astype(F32)
    epg = N_EXPERTS // N_GROUPS
    egroup = (ei_i >> 2).astype(F32)
    gsum = []
    for g in range(N_GROUPS):
        r = [bi[epg * g + j:epg * g + j + 1, :] for j in range(epg)]
        m = r[0] + r[1]
        for a in range(epg):
            for c in range(a + 1, epg):
                if (a, c) != (0, 1):
                    m = jnp.maximum(m, r[a] + r[c])
        gsum.append(m)
    best = jnp.zeros_like(gsum[0])
    bsc = gsum[0]
    for g in range(1, N_GROUPS):
        upd = gsum[g] > bsc
        best = jnp.where(upd, float(g), best)
        bsc = jnp.where(upd, gsum[g], bsc)
    neg = -jnp.inf
    masked = jnp.where(egroup == best, bi, neg)
    m1 = jnp.max(masked, axis=0, keepdims=True)
    i1 = jnp.min(jnp.where(masked == m1, ei, float(N_EXPERTS)), axis=0, keepdims=True)
    masked2 = jnp.where(ei == i1, neg, masked)
    m2 = jnp.max(masked2, axis=0, keepdims=True)
    i2 = jnp.min(jnp.where(masked2 == m2, ei, float(N_EXPERTS)), axis=0, keepdims=True)
    sel1 = ei == i1
    sel2 = ei == i2
    w1 = jnp.sum(jnp.where(sel1, sc, 0.0), axis=0, keepdims=True)
    w2 = jnp.sum(jnp.where(sel2, sc, 0.0), axis=0, keepdims=True)
    ws = w1 + w2
    rwt_ref[0, 0:1, :] = w1 / ws
    rwt_ref[0, 1:2, :] = w2 / ws

    oh = (sel1 | sel2).astype(F32)
    si = lax.broadcasted_iota(I32, (TM, TM), 0)
    ti = lax.broadcasted_iota(I32, (TM, TM), 1)
    upper = (si < ti).astype(BF16)
    prefix = jnp.dot(oh.astype(BF16), upper, preferred_element_type=F32)
    n_col = jnp.sum(oh, axis=1, keepdims=True)
    seg_len = jnp.floor((n_col + (SEG_ALIGN - 1.0)) * (1.0 / SEG_ALIGN)) * SEG_ALIGN
    seg_len_b = jnp.broadcast_to(seg_len, (N_EXPERTS, LANES))
    er = lax.broadcasted_iota(I32, (N_EXPERTS, N_EXPERTS), 0)
    ec = lax.broadcasted_iota(I32, (N_EXPERTS, N_EXPERTS), 1)
    lower = (ec < er).astype(BF16)
    seg_start_b = jnp.dot(lower, seg_len_b.astype(BF16), preferred_element_type=F32)
    local = prefix + seg_start_b[:, 0:1]
    pos1 = jnp.sum(jnp.where(sel1, local, 0.0), axis=0, keepdims=True)
    pos2 = jnp.sum(jnp.where(sel2, local, 0.0), axis=0, keepdims=True)
    ri_ref[0, 0:1, :] = i1.astype(I32)
    ri_ref[0, 1:2, :] = i2.astype(I32)
    ri_ref[0, 2:3, :] = pos1.astype(I32)
    ri_ref[0, 3:4, :] = pos2.astype(I32)
    meta_ref[0, 0:N_EXPERTS, :] = seg_start_b.astype(I32)
    meta_ref[0, N_EXPERTS:2 * N_EXPERTS, :] = seg_len_b.astype(I32)
    meta_ref[0, 2 * N_EXPERTS:, :] = cnt_ref[...].astype(I32)
    cnt_ref[...] = cnt_ref[...] + seg_len_b


def _merge(ctx, xa, p, oa, ob, oc, mods, gn_g, nf_g, wb_bf, wo_bf, rw_t, rb_col, t0):
    b, ntok, _ = p.shape
    nt = ntok // TM - t0
    ntiles = b * nt
    mod_row = lambda bi, t: jnp.where(t + t0 == 0, b, bi)
    tok = lambda bi, t: (bi, t + t0, 0)
    flat = lambda bi, t: (bi * nt + t, 0, 0)
    tok_args, tok_specs = _token_specs(ctx, xa, t0)
    outs = pl.pallas_call(
        functools.partial(_merge_kernel, split=ctx is not None, t0=t0),
        grid=(b, nt),
        in_specs=tok_specs + [
            pl.BlockSpec((1, TM, 4 * LANES), lambda bi, t: (bi, t, 0)),
            pl.BlockSpec((1, TM, 4 * LANES), lambda bi, t: (bi, t, 0)),
            pl.BlockSpec((1, TM, 4 * LANES), tok),
            pl.BlockSpec((1, TM, 4 * LANES), lambda bi, t: (bi, t + t0, BLK_CG // 4)),
            pl.BlockSpec((1, TM, D_MODEL), lambda bi, t: (bi, t + t0, BLK_GATE // 8)),
            pl.BlockSpec((1, TM, D_MODEL), lambda bi, t: (bi, t + t0, BLK_GATE // 8 + 1)),
            pl.BlockSpec((1, TM, D_MODEL), lambda bi, t: (bi, t + t0, BLK_GATE // 8 + 2)),
            pl.BlockSpec((1, 1, D_MODEL), lambda bi, t: (mod_row(bi, t), 0, 2)),
            pl.BlockSpec((1, 1, D_MODEL), lambda bi, t: (mod_row(bi, t), 0, 3)),
            pl.BlockSpec((1, 1, D_MODEL), lambda bi, t: (mod_row(bi, t), 0, 4)),
            pl.BlockSpec((1, 4 * LANES), lambda bi, t: (0, 0)),
            pl.BlockSpec((1, D_MODEL), lambda bi, t: (0, 0)),
            pl.BlockSpec((3, 4 * LANES, D_MODEL), lambda bi, t: (0, 0, 0)),
            pl.BlockSpec((D_MODEL, D_MODEL), lambda bi, t: (0, 0)),
            pl.BlockSpec((N_EXPERTS, D_MODEL), lambda bi, t: (0, 0)),
            pl.BlockSpec((N_EXPERTS, 1), lambda bi, t: (0, 0)),
        ],
        out_specs=[
            pl.BlockSpec((1, TM, D_MODEL), tok),
            pl.BlockSpec((TM, D_MODEL // 2), lambda bi, t: (bi * nt + t, 0)),
            pl.BlockSpec((1, 4, TM), flat),
            pl.BlockSpec((1, 2, TM), flat),
            pl.BlockSpec((1, 3 * N_EXPERTS, LANES), flat),
            pl.BlockSpec((N_EXPERTS, LANES), lambda bi, t: (0, 0)),
        ],
        out_shape=[
            jax.ShapeDtypeStruct((b, ntok, D_MODEL), F32),
            jax.ShapeDtypeStruct((ntiles * TM, D_MODEL // 2), U32),
            jax.ShapeDtypeStruct((ntiles, 4, TM), I32),
            jax.ShapeDtypeStruct((ntiles, 2, TM), F32),
            jax.ShapeDtypeStruct((ntiles, 3 * N_EXPERTS, LANES), I32),
            jax.ShapeDtypeStruct((N_EXPERTS, LANES), F32),
        ],
        input_output_aliases={} if ctx is not None else {0: 0},
        compiler_params=_cparams(("arbitrary", "arbitrary")),
    )(*tok_args, oa, ob, oc, p, p, p, p, mods, mods, mods, gn_g.reshape(1, -1), nf_g.reshape(1, -1),
      wb_bf, wo_bf, rw_t, rb_col)
    return outs


def _segment_copies(tile, start_s, len_s, dst_s, make_copy, wait):
    for e in range(N_EXPERTS):
        n = len_s[tile * N_EXPERTS + e]
        a = start_s[tile * N_EXPERTS + e]
        d = dst_s[tile * N_EXPERTS + e]
        for size in SEG_SIZES:
            @pl.when((n & size) != 0)
            def _(n=n, a=a, d=d, size=size):
                off = n & (-2 * size)
                cp = make_copy(pl.multiple_of(a + off, SEG_ALIGN), pl.multiple_of(d + off, SEG_ALIGN), size)
                if wait:
                    cp.wait()
                else:
                    cp.start()


def _sort_matrix(ri_ref):
    srow = lax.broadcasted_iota(I32, (SORT_ROWS, TM), 0)
    return srow == ri_ref[0, 2:3, :], srow == ri_ref[0, 3:4, :]


def _dispatch_kernel(start_s, len_s, dst_s, h_ref, ri_ref, rw_ref, xs_ref, sb, zb, sems):
    i = pl.program_id(0)
    last = pl.num_programs(0) - 1
    slot = lax.rem(i, 2)

    hi, lo = _unpack_rows(h_ref[...])
    hb = jnp.concatenate([hi.astype(BF16), lo.astype(BF16)], axis=1)
    m1, m2 = _sort_matrix(ri_ref)
    perm = jnp.where(m1 | m2, 1.0, 0.0).astype(BF16)
    rows = jnp.dot(perm, hb, preferred_element_type=F32)
    w = jnp.sum(jnp.where(m1, rw_ref[0, 0:1, :], 0.0) + jnp.where(m2, rw_ref[0, 1:2, :], 0.0),
                axis=1, keepdims=True)
    sb[slot, :, 0:D_MODEL // 2] = _pack_rows(rows)
    sb[slot, :, D_MODEL // 2:] = lax.bitcast_convert_type(jnp.broadcast_to(w, (SORT_ROWS, LANES)), U32)

    def copy_for(slot_k):
        def make_copy(local_row, global_row, size):
            return pltpu.make_async_copy(sb.at[slot_k, pl.ds(local_row, size)],
                                         xs_ref.at[pl.ds(global_row, size)], sems.at[slot_k])
        return make_copy

    for k in range(2):
        @pl.when(slot == k)
        def _(k=k):
            _segment_copies(i, start_s, len_s, dst_s, copy_for(k), wait=False)

            @pl.when(i > 0)
            def _():
                _segment_copies(i - 1, start_s, len_s, dst_s, copy_for(1 - k), wait=True)

            @pl.when(i == last)
            def _():
                _segment_copies(i, start_s, len_s, dst_s, copy_for(k), wait=True)

    @pl.when(i == last)
    def _():
        zb[...] = jnp.zeros_like(zb)

        def tail_copy(local_row, global_row, size):
            del local_row
            return pltpu.make_async_copy(zb.at[pl.ds(0, size)], xs_ref.at[pl.ds(global_row, size)], sems.at[0])

        _segment_copies(last + 1, start_s, len_s, dst_s, tail_copy, wait=False)
        _segment_copies(last + 1, start_s, len_s, dst_s, tail_copy, wait=True)

        def fill(j, carry):
            cp = pltpu.make_async_copy(zb, xs_ref.at[pl.ds(pl.multiple_of(j * TMF, TMF), TMF)], sems.at[1])
            cp.start()
            cp.wait()
            return carry

        lax.fori_loop(dst_s[(last + 2) * N_EXPERTS], xs_ref.shape[0] // TMF, fill, 0)


def _dispatch(hp, ri, rwt, seg_start, seg_len, seg_dst, n_rows):
    ntiles = ri.shape[0]
    grid_spec = pltpu.PrefetchScalarGridSpec(
        num_scalar_prefetch=3,
        grid=(ntiles,),
        in_specs=[
            pl.BlockSpec((TM, D_MODEL // 2), lambda i, *_: (i, 0)),
            pl.BlockSpec((1, 4, TM), lambda i, *_: (i, 0, 0)),
            pl.BlockSpec((1, 2, TM), lambda i, *_: (i, 0, 0)),
        ],
        out_specs=pl.BlockSpec(memory_space=pl.ANY),
        scratch_shapes=[pltpu.VMEM((2, SORT_ROWS, XS_COLS), U32), pltpu.VMEM((TMF, XS_COLS), U32),
                        pltpu.SemaphoreType.DMA((2,))],
    )
    return pl.pallas_call(
        _dispatch_kernel,
        grid_spec=grid_spec,
        out_shape=jax.ShapeDtypeStruct((n_rows, XS_COLS), U32),
        compiler_params=_cparams(("arbitrary",)),
    )(seg_start, seg_len, seg_dst, hp, ri, rwt)


def _ffn_kernel(te_ref, tv_ref, ts_ref, xs_ref, wg_ref, wu_ref, wd_ref, ys_ref, wgb, wub, wdb):
    del ts_ref
    i = pl.program_id(0)
    prev = te_ref[jnp.maximum(i - 1, 0)]
    fresh = (i == 0) | (te_ref[i] != prev)

    @pl.when(fresh)
    def _():
        wgb[...] = wg_ref[0, 0].astype(BF16)
        wub[...] = wu_ref[0, 0].astype(BF16)
        wdb[...] = wd_ref[0, 0].astype(BF16)

    @pl.when(tv_ref[i] > 0)
    def _():
        hi, lo = _unpack_rows(xs_ref[:, 0:D_MODEL // 2])
        xb = jnp.concatenate([hi.astype(BF16), lo.astype(BF16)], axis=1)
        w = lax.bitcast_convert_type(xs_ref[:, D_MODEL // 2:D_MODEL // 2 + 1], F32)
        g = jnp.dot(xb, wgb[...], preferred_element_type=F32)
        u = jnp.dot(xb, wub[...], preferred_element_type=F32)
        he = ((g * (1.0 + jnp.tanh(0.5 * g))) * (u * (0.5 * w))).astype(BF16)
        ys_ref[...] = _pack_rows(jnp.dot(he, wdb[...], preferred_element_type=F32))

    @pl.when(tv_ref[i] == 0)
    def _():
        ys_ref[...] = jnp.zeros_like(ys_ref)


def _ffn(xs, tile_e, tile_v, tile_src, wg, wu, wd, layer):
    n_rows = xs.shape[0]
    n_tiles = n_rows // TMF
    grid_spec = pltpu.PrefetchScalarGridSpec(
        num_scalar_prefetch=3,
        grid=(n_tiles,),
        in_specs=[
            pl.BlockSpec((TMF, XS_COLS), lambda i, te, tv, ts: (ts[i], 0)),
            pl.BlockSpec((1, 1, D_MODEL, D_EXPERT), lambda i, te, tv, ts: (layer, te[i], 0, 0)),
            pl.BlockSpec((1, 1, D_MODEL, D_EXPERT), lambda i, te, tv, ts: (layer, te[i], 0, 0)),
            pl.BlockSpec((1, 1, D_EXPERT, D_MODEL), lambda i, te, tv, ts: (layer, te[i], 0, 0)),
        ],
        out_specs=pl.BlockSpec((TMF, D_MODEL // 2), lambda i, te, tv, ts: (i, 0)),
        scratch_shapes=[
            pltpu.VMEM((D_MODEL, D_EXPERT), BF16),
            pltpu.VMEM((D_MODEL, D_EXPERT), BF16),
            pltpu.VMEM((D_EXPERT, D_MODEL), BF16),
        ],
    )
    return pl.pallas_call(
        _ffn_kernel,
        grid_spec=grid_spec,
        out_shape=jax.ShapeDtypeStruct((n_rows, D_MODEL // 2), U32),
        compiler_params=_cparams(("arbitrary",)),
    )(tile_e, tile_v, tile_src, xs, wg, wu, wd)


def _combine_kernel(start_s, len_s, dst_s, x_ref, ri_ref, g2_ref, fg_ref, ys_ref, o_ref, yb, sems, *, final):
    nt = pl.num_programs(1)
    i = pl.program_id(0) * nt + pl.program_id(1)
    last = pl.num_programs(0) * nt - 1
    slot = lax.rem(i, 2)

    def copy_for(slot_k):
        def make_copy(local_row, global_row, size):
            return pltpu.make_async_copy(ys_ref.at[pl.ds(global_row, size)],
                                         yb.at[slot_k, pl.ds(local_row, size)], sems.at[slot_k])
        return make_copy

    @pl.when(i == 0)
    def _():
        yb[...] = jnp.zeros_like(yb)
        _segment_copies(i, start_s, len_s, dst_s, copy_for(0), wait=False)

    for k in range(2):
        @pl.when(slot == k)
        def _(k=k):
            @pl.when(i < last)
            def _():
                _segment_copies(i + 1, start_s, len_s, dst_s, copy_for(1 - k), wait=False)

            _segment_copies(i, start_s, len_s, dst_s, copy_for(k), wait=True)

    hi, lo = _unpack_rows(yb[slot])
    m1, m2 = _sort_matrix(ri_ref)
    perm = jnp.where(m1 | m2, 1.0, 0.0).astype(BF16)
    tn_dims = (((0,), (0,)), ((), ()))
    moe = jnp.concatenate(
        [lax.dot_general(perm, hi.astype(BF16), tn_dims, preferred_element_type=F32),
         lax.dot_general(perm, lo.astype(BF16), tn_dims, preferred_element_type=F32)], axis=1)
    x = x_ref[0] + g2_ref[0] * moe
    if final:
        ms = jnp.mean(x * x, axis=-1, keepdims=True)
        x = x * lax.rsqrt(ms + EPS) * fg_ref[...]
    o_ref[0] = x


def _combine(xa, ys, ri, seg_start, seg_len, seg_dst, mods, final_g, t0, final):
    b, ntok, _ = xa.shape
    nt = ntok // TM - t0
    mod_row = lambda bi, t: jnp.where(t + t0 == 0, b, bi)
    if final:
        out_shape = jax.ShapeDtypeStruct((b, nt * TM, D_MODEL), F32)
        out_spec = pl.BlockSpec((1, TM, D_MODEL), lambda bi, t, *_: (bi, t, 0))
        aliases = {}
    else:
        out_shape = jax.ShapeDtypeStruct(xa.shape, F32)
        out_spec = pl.BlockSpec((1, TM, D_MODEL), lambda bi, t, *_: (bi, t + t0, 0))
        aliases = {3: 0}
    grid_spec = pltpu.PrefetchScalarGridSpec(
        num_scalar_prefetch=3,
        grid=(b, nt),
        in_specs=[
            pl.BlockSpec((1, TM, D_MODEL), lambda bi, t, *_: (bi, t + t0, 0)),
            pl.BlockSpec((1, 4, TM), lambda bi, t, *_: (bi * nt + t, 0, 0)),
            pl.BlockSpec((1, 1, D_MODEL), lambda bi, t, *_: (mod_row(bi, t), 0, 5)),
            pl.BlockSpec((1, D_MODEL), lambda bi, t, *_: (0, 0)),
            pl.BlockSpec(memory_space=pl.ANY),
        ],
        out_specs=out_spec,
        scratch_shapes=[pltpu.VMEM((2, SORT_ROWS, D_MODEL // 2), U32), pltpu.SemaphoreType.DMA((2,))],
    )
    return pl.pallas_call(
        functools.partial(_combine_kernel, final=final),
        grid_spec=grid_spec,
        out_shape=out_shape,
        input_output_aliases=aliases,
        compiler_params=_cparams(("arbitrary", "arbitrary")),
    )(seg_start, seg_len, seg_dst, xa, ri, mods, final_g.reshape(1, -1), ys)


def _route_tables(meta, cnt):
    ntiles = meta.shape[0]
    m = meta[:, :, 0].reshape(ntiles, 3, N_EXPERTS)
    seg_start, seg_len, before = m[:, 0], m[:, 1], m[:, 2]
    totals = cnt[:, 0].astype(I32)
    region = ((totals + TMF - 1) // TMF) * TMF
    ends = jnp.cumsum(region)
    seg_dst = (ends - region)[None, :] + before
    seg_start = jnp.concatenate([seg_start, jnp.zeros((1, N_EXPERTS), I32)], axis=0)
    seg_len = jnp.concatenate([seg_len, (region - totals)[None, :]], axis=0)
    seg_dst = jnp.concatenate([seg_dst, (ends - region + totals)[None, :]], axis=0)
    max_rows = 2 * ntiles * TM + ntiles * N_EXPERTS * (SEG_ALIGN - 1)
    n_tiles = max_rows // TMF + 1 + N_EXPERTS
    tile_id = jnp.arange(n_tiles, dtype=I32)
    starts = tile_id * TMF
    tile_e = jnp.minimum(jnp.sum(starts[:, None] >= ends[None, :], axis=1), N_EXPERTS - 1).astype(I32)
    valid = starts < ends[-1]
    last_used = jnp.maximum(ends[-1] // TMF - 1, 0)
    tile_e = jnp.where(valid, tile_e, tile_e[last_used])
    tile_src = jnp.minimum(tile_id, last_used)
    flat = lambda v: v.reshape(-1).astype(I32)
    seg_dst = jnp.concatenate([flat(seg_dst), (ends[-1:] // TMF).astype(I32)])
    return flat(seg_start), flat(seg_len), seg_dst, tile_e, valid.astype(I32), tile_src, n_tiles * TMF


def _rope_tables(n_lat):
    quarter = HEAD_DIM // 4
    inv = ROPE_BASE ** (-jnp.arange(quarter, dtype=F32) / quarter)
    n_rows = n_lat // GRID_W
    rows = jnp.repeat(jnp.arange(n_rows, dtype=F32), GRID_W)
    cols = jnp.tile(jnp.arange(GRID_W, dtype=F32), n_rows)
    ang = jnp.concatenate([rows[:, None] * inv, cols[:, None] * inv], axis=-1)
    cos = jnp.cos(ang)
    sin = jnp.sin(ang)
    cos_t = jnp.tile(cos, (1, LANES // cos.shape[1]))
    sin_t = jnp.tile(jnp.concatenate([-sin, sin], axis=-1), (1, LANES // HEAD_DIM))
    cos_t = jnp.concatenate([jnp.ones((CTX_LEN, LANES), F32), cos_t], axis=0)
    sin_t = jnp.concatenate([jnp.zeros((CTX_LEN, LANES), F32), sin_t], axis=0)
    return cos_t, sin_t


def kernel(x, c, ctx, c_ctx, w_ada, b_ada, norm_mix_g, norm_ffn_g, w_in, diff_lambda, diff_subln_g, swa_sink,
           ret_decay, ret_gn_g, w_branch, w_out, router_w, router_b, w_exp_gate, w_exp_up, w_exp_down, final_g):
    b, n_lat, _ = x.shape
    depth = w_ada.shape[0]
    assert ctx.shape[1] == CTX_LEN == TM and n_lat % TM == 0 and n_lat >= 2 * TM

    pad_rows = (-(b + 1)) % 8
    c_all = jnp.concatenate([c, c_ctx[None, :], jnp.zeros((pad_rows, D_MODEL), F32)], axis=0)
    mods_all = _ada(c_all, w_ada, b_ada)
    cos_t, sin_t = _rope_tables(n_lat)
    xa = x
    rw_t = router_w.T.astype(BF16)
    rb_col = router_b.reshape(-1, 1).astype(F32)

    for l in range(depth):
        last = l == depth - 1
        t0 = 1 if last else 0
        lam_init = 0.8 - 0.6 * math.exp(-0.3 * l)
        mods = mods_all[l].reshape(mods_all.shape[1], 1, -1)
        ctx_in = ctx if l == 0 else None
        p = _proj(ctx_in, xa, mods, norm_mix_g[l], cos_t, sin_t, w_in[l].astype(BF16))
        oa = _diff(p, diff_lambda[l], diff_subln_g[l], lam_init, t0)
        ob = _win(p, swa_sink[l], t0)
        oc = _ret(p, ret_decay[l])
        xa, hp, ri, rwt, meta, cnt = _merge(ctx_in, xa, p, oa, ob, oc, mods, ret_gn_g[l], norm_ffn_g[l],
                                            w_branch[l].astype(BF16), w_out[l].astype(BF16), rw_t, rb_col, t0)
        seg_start, seg_len, seg_dst, tile_e, tile_v, tile_src, n_rows = _route_tables(meta, cnt)
        xs = _dispatch(hp, ri, rwt, seg_start, seg_len, seg_dst, n_rows)
        ys = _ffn(xs, tile_e, tile_v, tile_src, w_exp_gate, w_exp_up, w_exp_down, l)
        xa = _combine(xa, ys, ri, seg_start, seg_len, seg_dst, mods, final_g, t0, last)
    return xa
```

```python
import functools
import math

import jax
import jax.numpy as jnp
from jax import lax
from jax.experimental import pallas as pl
from jax.experimental.pallas import tpu as pltpu

F32 = jnp.float32
BF16 = jnp.bfloat16
U32 = jnp.uint32
I32 = jnp.int32

D_MODEL = 1024
CTX_LEN = 256
GRID_W = 64
ROPE_BASE = 10000.0
EPS = 1e-6
LOG2E = 1.4426950408889634
HEAD_DIM = 64
WINDOW = 128
RT_CHUNK = 128
N_EXPERTS = 16
N_GROUPS = 4
D_EXPERT = 512
LANES = 128
TM = 256
DIFF_HEADS = 4
TMF = 512
SEG_ALIGN = 8
SEG_SIZES = (256, 128, 64, 32, 16, 8)
SORT_ROWS = 640
XS_COLS = D_MODEL // 2 + 128
D_IN = 6912
N_SRC_BLK = D_IN // LANES
N_OUT_BLK = N_SRC_BLK + 2
VMEM_LIMIT = 56 * 1024 * 1024

BLK_AQ, BLK_AK, BLK_AV = 0, 4, 8
BLK_BQ, BLK_BK, BLK_BV = 12, 16, 18
BLK_CQ, BLK_CK, BLK_CV, BLK_CG, BLK_GATE = 20, 22, 24, 28, 32


def _cparams(sem):
    return pltpu.CompilerParams(dimension_semantics=sem, vmem_limit_bytes=VMEM_LIMIT)


def _ada_kernel(c_ref, w_ref, b_ref, o_ref):
    c = c_ref[...]
    s = (c * jax.nn.sigmoid(c)).astype(BF16)
    o_ref[0] = jnp.dot(s, w_ref[0].astype(BF16), preferred_element_type=F32) + b_ref[0]


def _ada(c_all, w_ada, b_ada):
    depth = w_ada.shape[0]
    rows = c_all.shape[0]
    nj = w_ada.shape[2] // D_MODEL
    return pl.pallas_call(
        _ada_kernel,
        grid=(depth, nj),
        in_specs=[
            pl.BlockSpec((rows, D_MODEL), lambda l, j: (0, 0)),
            pl.BlockSpec((1, D_MODEL, D_MODEL), lambda l, j: (l, 0, j)),
            pl.BlockSpec((1, 1, D_MODEL), lambda l, j: (l, 0, j)),
        ],
        out_specs=pl.BlockSpec((1, rows, D_MODEL), lambda l, j: (l, 0, j)),
        out_shape=jax.ShapeDtypeStruct((depth, rows, w_ada.shape[2]), F32),
        compiler_params=_cparams(("arbitrary", "arbitrary")),
    )(c_all, w_ada, b_ada.reshape(depth, 1, -1))


def _rope(a, cos, sin_signed, first_half):
    rot = jnp.where(first_half, pltpu.roll(a, LANES - HEAD_DIM // 2, 1), pltpu.roll(a, HEAD_DIM // 2, 1))
    return a * cos + rot * sin_signed


def _token_tile(ctx_ref, x_ref, tile):
    if ctx_ref is None:
        return x_ref[0]
    return jnp.where(tile == 0, ctx_ref[0], x_ref[0])


def _token_specs(ctx, xa, t0):
    if ctx is None:
        return [xa], [pl.BlockSpec((1, TM, D_MODEL), lambda bi, t, *_: (bi, t + t0, 0))]
    return [ctx, xa], [pl.BlockSpec((1, TM, D_MODEL), lambda bi, t, *_: (bi, 0, 0)),
                       pl.BlockSpec((1, TM, D_MODEL), lambda bi, t, *_: (bi, jnp.maximum(t + t0 - 1, 0), 0))]


def _proj_kernel(*refs, split):
    ctx_ref, refs = (refs[0], refs[1:]) if split else (None, refs)
    x_ref, sh_ref, sc_ref, g_ref, cos_ref, sin_ref, w_ref, o_ref = refs
    x = _token_tile(ctx_ref, x_ref, pl.program_id(1))
    ms = jnp.mean(x * x, axis=-1, keepdims=True)
    h = x * lax.rsqrt(ms + EPS) * g_ref[...]
    h = h * (1.0 + sc_ref[0]) + sh_ref[0]
    hb = h.astype(BF16)
    cos = cos_ref[...]
    sin = sin_ref[...]
    lane = lax.broadcasted_iota(I32, (1, LANES), 1)
    first_half = (lane & (HEAD_DIM - 1)) < (HEAD_DIM // 2)
    low = lane < HEAD_DIM
    scale = HEAD_DIM ** -0.5
    for c in range(N_SRC_BLK // 2):
        acc = jnp.dot(hb, w_ref[:, c * 2 * LANES:(c + 1) * 2 * LANES], preferred_element_type=F32)
        for half in range(2):
            src = 2 * c + half
            a = acc[:, half * LANES:(half + 1) * LANES]
            if src < 8 or 12 <= src < 17:
                a = _rope(a, cos, sin, first_half)
            if src < 4 or 12 <= src < 16:
                a = a * (scale * LOG2E)
            if 20 <= src < 22:
                a = a * scale
            if src >= 26:
                a = a * 0.5
            if src in (16, 17):
                sw = pltpu.roll(a, HEAD_DIM, 1)
                dst = BLK_BK if src == 16 else BLK_BV
                o_ref[0, :, dst * LANES:(dst + 1) * LANES] = jnp.where(low, a, sw).astype(BF16)
                o_ref[0, :, (dst + 1) * LANES:(dst + 2) * LANES] = jnp.where(low, sw, a).astype(BF16)
            else:
                dst = src if src < 16 else src + 2
                o_ref[0, :, dst * LANES:(dst + 1) * LANES] = a.astype(BF16)


def _proj(ctx, xa, mods, norm_g, cos_t, sin_t, w_in_bf):
    b = xa.shape[0]
    ntok = xa.shape[1] + (0 if ctx is None else CTX_LEN)
    nt = ntok // TM
    mod_row = lambda bi, t: jnp.where(t == 0, b, bi)
    tok_args, tok_specs = _token_specs(ctx, xa, 0)
    return pl.pallas_call(
        functools.partial(_proj_kernel, split=ctx is not None),
        grid=(b, nt),
        in_specs=tok_specs + [
            pl.BlockSpec((1, 1, D_MODEL), lambda bi, t: (mod_row(bi, t), 0, 0)),
            pl.BlockSpec((1, 1, D_MODEL), lambda bi, t: (mod_row(bi, t), 0, 1)),
            pl.BlockSpec((1, D_MODEL), lambda bi, t: (0, 0)),
            pl.BlockSpec((TM, LANES), lambda bi, t: (t, 0)),
            pl.BlockSpec((TM, LANES), lambda bi, t: (t, 0)),
            pl.BlockSpec((D_MODEL, D_IN), lambda bi, t: (0, 0)),
        ],
        out_specs=pl.BlockSpec((1, TM, N_OUT_BLK * LANES), lambda bi, t: (bi, t, 0)),
        out_shape=jax.ShapeDtypeStruct((b, ntok, N_OUT_BLK * LANES), BF16),
        compiler_params=_cparams(("arbitrary", "arbitrary")),
    )(*tok_args, mods, mods, norm_g.reshape(1, -1), cos_t, sin_t, w_in_bf)


def _diff_kernel(lam_ref, g_ref, q_ref, k_ref, v_ref, o_ref, *, lam_init, t0, ntok):
    t = pl.program_id(2) + t0
    lv = lam_ref[...]
    lam = (jnp.exp(jnp.sum(lv[0:1] * lv[1:2], axis=-1, keepdims=True))
           - jnp.exp(jnp.sum(lv[2:3] * lv[3:4], axis=-1, keepdims=True)) + lam_init)
    lane = lax.broadcasted_iota(I32, (1, LANES), 1)
    nt_dims = (((1,), (1,)), ((), ()))

    def attend(nk):
        def scores(hd):
            cols = slice(hd * LANES, (hd + 1) * LANES)
            q = q_ref[0, :, cols]
            zero = jnp.zeros_like(q)
            q0 = jnp.where(lane < HEAD_DIM, q, zero)
            q1 = jnp.where(lane >= HEAD_DIM, q, zero)
            k = k_ref[0, 0:nk, cols]
            return (lax.dot_general(q0, k, nt_dims, preferred_element_type=F32),
                    lax.dot_general(q1, k, nt_dims, preferred_element_type=F32))

        ones_blk = jnp.where(lax.broadcasted_iota(I32, (nk, LANES), 1) == 0, 1.0, 0.0).astype(BF16)
        nxt = scores(0)
        for hd in range(DIFF_HEADS):
            cols = slice(hd * LANES, (hd + 1) * LANES)
            v1 = jnp.concatenate([v_ref[0, 0:nk, cols], ones_blk], axis=1)
            s0, s1 = nxt
            if hd + 1 < DIFF_HEADS:
                nxt = scores(hd + 1)
            e0 = jnp.exp2(s0 - jnp.max(s0, axis=-1, keepdims=True)).astype(BF16)
            e1 = jnp.exp2(s1 - jnp.max(s1, axis=-1, keepdims=True)).astype(BF16)
            ov0 = jnp.dot(e0, v1, preferred_element_type=F32)
            ov1 = jnp.dot(e1, v1, preferred_element_type=F32)
            r0 = 1.0 / ov0[:, LANES:LANES + 1]
            r1 = lam / ov1[:, LANES:LANES + 1]
            o = ov0[:, 0:LANES] * r0 - ov1[:, 0:LANES] * r1
            o = o * lax.rsqrt(jnp.mean(o * o, axis=-1, keepdims=True) + EPS) * g_ref[...] * (1.0 - lam_init)
            o_ref[0, :, cols] = o.astype(BF16)

    if t0 == 0:
        @pl.when(t == 0)
        def _():
            attend(CTX_LEN)

        @pl.when(t > 0)
        def _():
            attend(ntok)
    else:
        attend(ntok)


def _diff(p, lam_vec, subln_g, lam_init, t0):
    b, ntok, _ = p.shape
    nt = ntok // TM - t0
    heads = 4
    hw = DIFF_HEADS * LANES
    return pl.pallas_call(
        functools.partial(_diff_kernel, lam_init=lam_init, t0=t0, ntok=ntok),
        grid=(b, heads // DIFF_HEADS, nt),
        in_specs=[
            pl.BlockSpec((4, HEAD_DIM), lambda bi, h, t: (0, 0)),
            pl.BlockSpec((1, LANES), lambda bi, h, t: (0, 0)),
            pl.BlockSpec((1, TM, hw), lambda bi, h, t: (bi, t + t0, BLK_AQ // DIFF_HEADS + h)),
            pl.BlockSpec((1, ntok, hw), lambda bi, h, t: (bi, 0, BLK_AK // DIFF_HEADS + h)),
            pl.BlockSpec((1, ntok, hw), lambda bi, h, t: (bi, 0, BLK_AV // DIFF_HEADS + h)),
        ],
        out_specs=pl.BlockSpec((1, TM, hw), lambda bi, h, t: (bi, t, h)),
        out_shape=jax.ShapeDtypeStruct((b, nt * TM, heads * LANES), BF16),
        compiler_params=_cparams(("arbitrary", "arbitrary", "arbitrary")),
    )(lam_vec, subln_g.reshape(1, -1), p, p, p)


def _win_kernel(sink_ref, q_ref, k_ref, v_ref, o_ref, *, t0, ntok):
    t = pl.program_id(1) + t0
    lane = lax.broadcasted_iota(I32, (1, LANES), 1)
    low = lane < HEAD_DIM
    nt_dims = (((1,), (1,)), ((), ()))
    wk = 2 * TM
    start = jnp.clip(t * TM - WINDOW, CTX_LEN, ntok - wk)
    start = pl.multiple_of(start, WINDOW)
    qpos = t * TM + lax.broadcasted_iota(I32, (TM, wk), 0)
    kpos = start + lax.broadcasted_iota(I32, (TM, wk), 1)
    in_band = (jnp.abs(kpos - qpos) <= WINDOW) & (t > 0)
    bias = jnp.concatenate([jnp.zeros((TM, CTX_LEN), F32), jnp.where(in_band, 0.0, -jnp.inf).astype(F32)], axis=1)
    n_heads = 8
    group = n_heads // 2
    keys, vals = [], []
    ones_blk = jnp.where(lax.broadcasted_iota(I32, (CTX_LEN + wk, LANES), 1) == 0, 1.0, 0.0).astype(BF16)
    for g in range(2):
        cols = slice(g * LANES, (g + 1) * LANES)
        keys.append(jnp.concatenate([k_ref[0, 0:CTX_LEN, cols], k_ref[0, pl.ds(start, wk), cols]], axis=0))
        vg = jnp.concatenate([v_ref[0, 0:CTX_LEN, cols], v_ref[0, pl.ds(start, wk), cols]], axis=0)
        vals.append(jnp.concatenate([vg, ones_blk], axis=1))

    def scores(h):
        qb = q_ref[0, :, (h // 2) * LANES:(h // 2 + 1) * LANES]
        qm = jnp.where(low if h % 2 == 0 else jnp.logical_not(low), qb, jnp.zeros_like(qb))
        return lax.dot_general(qm, keys[h // group], nt_dims, preferred_element_type=F32)

    nxt = scores(0)
    outs = []
    for h in range(n_heads):
        s = nxt + bias
        if h + 1 < n_heads:
            nxt = scores(h + 1)
        sk = sink_ref[h] * LOG2E
        m = jnp.maximum(jnp.max(s, axis=-1, keepdims=True), sk)
        e = jnp.exp2(s - m).astype(BF16)
        ov = jnp.dot(e, vals[h // group], preferred_element_type=F32)
        l = ov[:, LANES:LANES + 1] + jnp.exp2(sk - m)
        outs.append(ov[:, 0:LANES] * (1.0 / l))
        if h % 2 == 1:
            o_ref[0, :, (h // 2) * LANES:(h // 2 + 1) * LANES] = jnp.where(low, outs[h - 1], outs[h]).astype(BF16)


def _win(p, sink, t0):
    b, ntok, _ = p.shape
    nt = ntok // TM - t0
    return pl.pallas_call(
        functools.partial(_win_kernel, t0=t0, ntok=ntok),
        grid=(b, nt),
        in_specs=[
            pl.BlockSpec(memory_space=pltpu.SMEM),
            pl.BlockSpec((1, TM, 4 * LANES), lambda bi, t: (bi, t + t0, BLK_BQ // 4)),
            pl.BlockSpec((1, ntok, 2 * LANES), lambda bi, t: (bi, 0, BLK_BK // 2)),
            pl.BlockSpec((1, ntok, 2 * LANES), lambda bi, t: (bi, 0, BLK_BV // 2)),
        ],
        out_specs=pl.BlockSpec((1, TM, 4 * LANES), lambda bi, t: (bi, t, 0)),
        out_shape=jax.ShapeDtypeStruct((b, nt * TM, 4 * LANES), BF16),
        compiler_params=_cparams(("arbitrary", "arbitrary")),
    )(sink, p, p, p)


def _ret_kernel(dec_ref, q_ref, k_ref, v_ref, o_ref, ob_ref, *, ntok):
    T = RT_CHUNK
    nchunk = ntok // T
    nctx = CTX_LEN // T
    x = dec_ref[...]
    lg_all = -(jnp.maximum(-x, 0.0) + jnp.log1p(jnp.exp(-jnp.abs(x))))
    lane = lax.broadcasted_iota(I32, (1, LANES), 1)
    low = lane < HEAD_DIM
    row_low = lax.broadcasted_iota(I32, (LANES, 1), 0) < HEAD_DIM
    pi = lax.broadcasted_iota(I32, (T, T), 0).astype(F32)
    pj = lax.broadcasted_iota(I32, (T, T), 1).astype(F32)
    pcol = lax.broadcasted_iota(I32, (T, 1), 0).astype(F32)
    nt_dims = (((1,), (1,)), ((), ()))
    tn_dims = (((0,), (0,)), ((), ()))

    decay, qdp, kdp, cdp = [], [], [], []
    for direction in range(2):
        lgs = [lg_all[direction:direction + 1, h:h + 1] for h in range(4)]
        if direction == 0:
            dist = pi - pj
            qd = [jnp.exp((pcol + 1.0) * lg) for lg in lgs]
            kd = [jnp.exp((T - 1.0 - pcol) * lg) for lg in lgs]
        else:
            dist = pj - pi
            qd = [jnp.exp((T - pcol) * lg) for lg in lgs]
            kd = [jnp.exp(pcol * lg) for lg in lgs]
        decay.append([jnp.where(dist >= 0, jnp.exp(jnp.maximum(dist, 0.0) * lg), 0.0) for lg in lgs])
        qdp.append([jnp.where(low, qd[2 * pr], qd[2 * pr + 1]) for pr in range(2)])
        kdp.append([jnp.where(low, kd[2 * pr], kd[2 * pr + 1]) for pr in range(2)])
        cdp.append([jnp.where(row_low, jnp.exp(T * lgs[2 * pr]), jnp.exp(T * lgs[2 * pr + 1])) for pr in range(2)])

    def step(i, states):
        chunks = (i, jnp.where(i < nctx, nctx - 1 - i, nchunk + nctx - 1 - i))
        halves = (low, jnp.logical_not(low))
        work = []
        for direction in range(2):
            r0 = pl.multiple_of(chunks[direction] * T, T)
            for pair in range(2):
                qb = q_ref[0, pl.ds(r0, T), pair * LANES:(pair + 1) * LANES]
                kb = k_ref[0, pl.ds(r0, T), pair * LANES:(pair + 1) * LANES]
                v2 = v_ref[0, pl.ds(r0, T), pair * 2 * LANES:(pair + 1) * 2 * LANES]
                kbd = (kb.astype(F32) * kdp[direction][pair]).astype(BF16)
                qsb = (qb.astype(F32) * qdp[direction][pair]).astype(BF16)
                zq = jnp.zeros_like(qb)
                q2 = jnp.concatenate([jnp.where(hm, qb, zq) for hm in halves], axis=0)
                inner2 = lax.dot_general(q2, kb, nt_dims, preferred_element_type=F32)
                kv2 = lax.dot_general(kbd, v2, tn_dims, preferred_element_type=F32)
                work.append((direction, pair, r0, inner2, kv2, qsb, v2))
        new_states = []
        for direction, pair, r0, inner2, kv2, qsb, v2 in work:
            out_ref = o_ref.at[0] if direction == 0 else ob_ref
            st = states[2 * direction + pair]
            stb = st.astype(BF16)
            zq = jnp.zeros_like(qsb)
            for sub in range(2):
                h = 2 * pair + sub
                inner = inner2[sub * T:(sub + 1) * T] * decay[direction][h]
                lhs = jnp.concatenate([inner.astype(BF16), jnp.where(halves[sub], qsb, zq)], axis=1)
                rhs = jnp.concatenate([v2[:, sub * LANES:(sub + 1) * LANES], stb], axis=0)
                out_ref[pl.ds(r0, T), h * LANES:(h + 1) * LANES] = jnp.dot(lhs, rhs, preferred_element_type=F32)
            new_states.append(st * cdp[direction][pair]
                              + jnp.where(row_low, kv2[:, 0:LANES], kv2[:, LANES:2 * LANES]))
        return tuple(new_states)

    zero = jnp.zeros((LANES, LANES), F32)
    lax.fori_loop(0, nchunk, step, (zero,) * 4, unroll=3)
    o_ref[0] = o_ref[0] + ob_ref[...]


def _ret(p, ret_decay):
    b, ntok, _ = p.shape
    return pl.pallas_call(
        functools.partial(_ret_kernel, ntok=ntok),
        grid=(b,),
        in_specs=[
            pl.BlockSpec((2, 4), lambda bi: (0, 0)),
            pl.BlockSpec((1, ntok, 2 * LANES), lambda bi: (bi, 0, BLK_CQ // 2)),
            pl.BlockSpec((1, ntok, 2 * LANES), lambda bi: (bi, 0, BLK_CK // 2)),
            pl.BlockSpec((1, ntok, 4 * LANES), lambda bi: (bi, 0, BLK_CV // 4)),
        ],
        out_specs=pl.BlockSpec((1, ntok, 4 * LANES), lambda bi: (bi, 0, 0)),
        out_shape=jax.ShapeDtypeStruct((b, ntok, 4 * LANES), F32),
        scratch_shapes=[pltpu.VMEM((ntok, 4 * LANES), F32)],
        compiler_params=_cparams(("arbitrary",)),
    )(ret_decay, p, p, p)


def _pack_rows(h):
    half = h.shape[1] // 2
    hi = lax.bitcast_convert_type(h[:, :half].astype(BF16).astype(F32), U32)
    lo = lax.bitcast_convert_type(h[:, half:].astype(BF16).astype(F32), U32)
    return hi | (lo >> 16)


def _unpack_rows(p):
    hi = lax.bitcast_convert_type(p & jnp.uint32(0xFFFF0000), F32)
    lo = lax.bitcast_convert_type(p << 16, F32)
    return hi, lo


def _merge_kernel(*refs, split, t0):
    ctx_ref, refs = (refs[0], refs[1:]) if split else (None, refs)
    (x_ref, oa_ref, ob_ref, oc_ref, cg_ref, ga_ref, gb_ref, gc_ref, g1_ref, sh2_ref, sc2_ref,
     gn_ref, nf_ref, wb_ref, wo_ref, rw_ref, rb_ref,
     xo_ref, hp_ref, ri_ref, rwt_ref, meta_ref, cnt_ref) = refs
    first = (pl.program_id(0) == 0) & (pl.program_id(1) == 0)

    @pl.when(first)
    def _():
        cnt_ref[...] = jnp.zeros_like(cnt_ref)

    da = jnp.dot(oa_ref[0], wb_ref[0], preferred_element_type=F32)
    db = jnp.dot(ob_ref[0], wb_ref[1], preferred_element_type=F32)

    oc = oc_ref[0]
    cgv = cg_ref[0].astype(F32)
    gn = gn_ref[...]
    parts = []
    for h in range(4):
        o = oc[:, h * LANES:(h + 1) * LANES]
        mu = jnp.mean(o, axis=-1, keepdims=True)
        dlt = o - mu
        var = jnp.mean(dlt * dlt, axis=-1, keepdims=True)
        y = dlt * lax.rsqrt(var + EPS) * gn[:, h * LANES:(h + 1) * LANES]
        cgh = cgv[:, h * LANES:(h + 1) * LANES]
        parts.append((y * (cgh * (1.0 + jnp.tanh(cgh)))).astype(BF16))
    ocn = jnp.concatenate(parts, axis=1)

    dc = jnp.dot(ocn, wb_ref[2], preferred_element_type=F32)
    y = (1.0 + jnp.tanh(ga_ref[0].astype(F32))) * da
    y += (1.0 + jnp.tanh(gb_ref[0].astype(F32))) * db
    y += (1.0 + jnp.tanh(gc_ref[0].astype(F32))) * dc
    y2 = jnp.dot(y.astype(BF16), wo_ref[...], preferred_element_type=F32)
    x = _token_tile(ctx_ref, x_ref, pl.program_id(1) + t0) + (0.5 * g1_ref[0]) * y2
    xo_ref[0] = x

    ms = jnp.mean(x * x, axis=-1, keepdims=True)
    h2 = x * lax.rsqrt(ms + EPS) * nf_ref[...]
    h2 = h2 * (1.0 + sc2_ref[0]) + sh2_ref[0]
    hp_ref[...] = _pack_rows(h2)

    nt_dims = (((1,), (1,)), ((), ()))
    logits = lax.dot_general(rw_ref[...], h2.astype(BF16), nt_dims, preferred_element_type=F32)
    sc = jax.nn.sigmoid(logits)
    bi = sc + rb_ref[...]
    ei_i = lax.broadcasted_iota(I32, (N_EXPERTS, 1), 0)
    ei = ei_i.astype(F32)
    epg = N_EXPERTS // N_GROUPS
    egroup = (ei_i >> 2).astype(F32)
    gsum = []
    for g in range(N_GROUPS):
        r = [bi[epg * g + j:epg * g + j + 1, :] for j in range(epg)]
        m = r[0] + r[1]
        for a in range(epg):
            for c in range(a + 1, epg):
                if (a, c) != (0, 1):
                    m = jnp.maximum(m, r[a] + r[c])
        gsum.append(m)
    best = jnp.zeros_like(gsum[0])
    bsc = gsum[0]
    for g in range(1, N_GROUPS):
        upd = gsum[g] > bsc
        best = jnp.where(upd, float(g), best)
        bsc = jnp.where(upd, gsum[g], bsc)
    neg = -jnp.inf
    masked = jnp.where(egroup == best, bi, neg)
    m1 = jnp.max(masked, axis=0, keepdims=True)
    i1 = jnp.min(jnp.where(masked == m1, ei, float(N_EXPERTS)), axis=0, keepdims=True)
    masked2 = jnp.where(ei == i1, neg, masked)
    m2 = jnp.max(masked2, axis=0, keepdims=True)
    i2 = jnp.min(jnp.where(masked2 == m2, ei, float(N_EXPERTS)), axis=0, keepdims=True)
    sel1 = ei == i1
    sel2 = ei == i2
    w1 = jnp.sum(jnp.where(sel1, sc, 0.0), axis=0, keepdims=True)
    w2 = jnp.sum(jnp.where(sel2, sc, 0.0), axis=0, keepdims=True)
    ws = w1 + w2
    rwt_ref[0, 0:1, :] = w1 / ws
    rwt_ref[0, 1:2, :] = w2 / ws

    oh = (sel1 | sel2).astype(F32)
    si = lax.broadcasted_iota(I32, (TM, TM), 0)
    ti = lax.broadcasted_iota(I32, (TM, TM), 1)
    upper = (si < ti).astype(BF16)
    prefix = jnp.dot(oh.astype(BF16), upper, preferred_element_type=F32)
    n_col = jnp.sum(oh, axis=1, keepdims=True)
    seg_len = jnp.floor((n_col + (SEG_ALIGN - 1.0)) * (1.0 / SEG_ALIGN)) * SEG_ALIGN
    seg_len_b = jnp.broadcast_to(seg_len, (N_EXPERTS, LANES))
    er = lax.broadcasted_iota(I32, (N_EXPERTS, N_EXPERTS), 0)
    ec = lax.broadcasted_iota(I32, (N_EXPERTS, N_EXPERTS), 1)
    lower = (ec < er).astype(BF16)
    seg_start_b = jnp.dot(lower, seg_len_b.astype(BF16), preferred_element_type=F32)
    local = prefix + seg_start_b[:, 0:1]
    pos1 = jnp.sum(jnp.where(sel1, local, 0.0), axis=0, keepdims=True)
    pos2 = jnp.sum(jnp.where(sel2, local, 0.0), axis=0, keepdims=True)
    ri_ref[0, 0:1, :] = i1.astype(I32)
    ri_ref[0, 1:2, :] = i2.astype(I32)
    ri_ref[0, 2:3, :] = pos1.astype(I32)
    ri_ref[0, 3:4, :] = pos2.astype(I32)
    meta_ref[0, 0:N_EXPERTS, :] = seg_start_b.astype(I32)
    meta_ref[0, N_EXPERTS:2 * N_EXPERTS, :] = seg_len_b.astype(I32)
    meta_ref[0, 2 * N_EXPERTS:, :] = cnt_ref[...].astype(I32)
    cnt_ref[...] = cnt_ref[...] + seg_len_b


def _merge(ctx, xa, p, oa, ob, oc, mods, gn_g, nf_g, wb_bf, wo_bf, rw_t, rb_col, t0):
    b, ntok, _ = p.shape
    nt = ntok // TM - t0
    ntiles = b * nt
    mod_row = lambda bi, t: jnp.where(t + t0 == 0, b, bi)
    tok = lambda bi, t: (bi, t + t0, 0)
    flat = lambda bi, t: (bi * nt + t, 0, 0)
    tok_args, tok_specs = _token_specs(ctx, xa, t0)
    outs = pl.pallas_call(
        functools.partial(_merge_kernel, split=ctx is not None, t0=t0),
        grid=(b, nt),
        in_specs=tok_specs + [
            pl.BlockSpec((1, TM, 4 * LANES), lambda bi, t: (bi, t, 0)),
            pl.BlockSpec((1, TM, 4 * LANES), lambda bi, t: (bi, t, 0)),
            pl.BlockSpec((1, TM, 4 * LANES), tok),
            pl.BlockSpec((1, TM, 4 * LANES), lambda bi, t: (bi, t + t0, BLK_CG // 4)),
            pl.BlockSpec((1, TM, D_MODEL), lambda bi, t: (bi, t + t0, BLK_GATE // 8)),
            pl.BlockSpec((1, TM, D_MODEL), lambda bi, t: (bi, t + t0, BLK_GATE // 8 + 1)),
            pl.BlockSpec((1, TM, D_MODEL), lambda bi, t: (bi, t + t0, BLK_GATE // 8 + 2)),
            pl.BlockSpec((1, 1, D_MODEL), lambda bi, t: (mod_row(bi, t), 0, 2)),
            pl.BlockSpec((1, 1, D_MODEL), lambda bi, t: (mod_row(bi, t), 0, 3)),
            pl.BlockSpec((1, 1, D_MODEL), lambda bi, t: (mod_row(bi, t), 0, 4)),
            pl.BlockSpec((1, 4 * LANES), lambda bi, t: (0, 0)),
            pl.BlockSpec((1, D_MODEL), lambda bi, t: (0, 0)),
            pl.BlockSpec((3, 4 * LANES, D_MODEL), lambda bi, t: (0, 0, 0)),
            pl.BlockSpec((D_MODEL, D_MODEL), lambda bi, t: (0, 0)),
            pl.BlockSpec((N_EXPERTS, D_MODEL), lambda bi, t: (0, 0)),
            pl.BlockSpec((N_EXPERTS, 1), lambda bi, t: (0, 0)),
        ],
        out_specs=[
            pl.BlockSpec((1, TM, D_MODEL), tok),
            pl.BlockSpec((TM, D_MODEL // 2), lambda bi, t: (bi * nt + t, 0)),
            pl.BlockSpec((1, 4, TM), flat),
            pl.BlockSpec((1, 2, TM), flat),
            pl.BlockSpec((1, 3 * N_EXPERTS, LANES), flat),
            pl.BlockSpec((N_EXPERTS, LANES), lambda bi, t: (0, 0)),
        ],
        out_shape=[
            jax.ShapeDtypeStruct((b, ntok, D_MODEL), F32),
            jax.ShapeDtypeStruct((ntiles * TM, D_MODEL // 2), U32),
            jax.ShapeDtypeStruct((ntiles, 4, TM), I32),
            jax.ShapeDtypeStruct((ntiles, 2, TM), F32),
            jax.ShapeDtypeStruct((ntiles, 3 * N_EXPERTS, LANES), I32),
            jax.ShapeDtypeStruct((N_EXPERTS, LANES), F32),
        ],
        input_output_aliases={} if ctx is not None else {0: 0},
        compiler_params=_cparams(("arbitrary", "arbitrary")),
    )(*tok_args, oa, ob, oc, p, p, p, p, mods, mods, mods, gn_g.reshape(1, -1), nf_g.reshape(1, -1),
      wb_bf, wo_bf, rw_t, rb_col)
    return outs


def _segment_copies(tile, start_s, len_s, dst_s, make_copy, wait):
    for e in range(N_EXPERTS):
        n = len_s[tile * N_EXPERTS + e]
        a = start_s[tile * N_EXPERTS + e]
        d = dst_s[tile * N_EXPERTS + e]
        for size in SEG_SIZES:
            @pl.when((n & size) != 0)
            def _(n=n, a=a, d=d, size=size):
                off = n & (-2 * size)
                cp = make_copy(pl.multiple_of(a + off, SEG_ALIGN), pl.multiple_of(d + off, SEG_ALIGN), size)
                if wait:
                    cp.wait()
                else:
                    cp.start()


def _wait_rows(total, make_copy):
    for size in (2 * SEG_SIZES[0],) + SEG_SIZES:
        @pl.when((total & size) != 0)
        def _(size=size):
            make_copy(size).wait()


def _sort_matrix(ri_ref):
    srow = lax.broadcasted_iota(I32, (SORT_ROWS, TM), 0)
    return srow == ri_ref[0, 2:3, :], srow == ri_ref[0, 3:4, :]


def _dispatch_kernel(start_s, len_s, dst_s, h_ref, ri_ref, rw_ref, xs_ref, sb, zb, sems):
    i = pl.program_id(0)
    last = pl.num_programs(0) - 1
    slot = lax.rem(i, 2)

    hi, lo = _unpack_rows(h_ref[...])
    hb = jnp.concatenate([hi.astype(BF16), lo.astype(BF16)], axis=1)
    m1, m2 = _sort_matrix(ri_ref)
    perm = jnp.where(m1 | m2, 1.0, 0.0).astype(BF16)
    rows = jnp.dot(perm, hb, preferred_element_type=F32)
    w = jnp.sum(jnp.where(m1, rw_ref[0, 0:1, :], 0.0) + jnp.where(m2, rw_ref[0, 1:2, :], 0.0),
                axis=1, keepdims=True)
    sb[slot, :, 0:D_MODEL // 2] = _pack_rows(rows)
    sb[slot, :, D_MODEL // 2:] = lax.bitcast_convert_type(jnp.broadcast_to(w, (SORT_ROWS, LANES)), U32)

    def copy_for(slot_k):
        def make_copy(local_row, global_row, size):
            return pltpu.make_async_copy(sb.at[slot_k, pl.ds(local_row, size)],
                                         xs_ref.at[pl.ds(global_row, size)], sems.at[slot_k])
        return make_copy

    def tile_rows(tile):
        j = tile * N_EXPERTS + N_EXPERTS - 1
        return start_s[j] + len_s[j]

    for k in range(2):
        @pl.when(slot == k)
        def _(k=k):
            _segment_copies(i, start_s, len_s, dst_s, copy_for(k), wait=False)

            @pl.when(i > 0)
            def _():
                _wait_rows(tile_rows(i - 1), lambda size: copy_for(1 - k)(0, 0, size))

            @pl.when(i == last)
            def _():
                _wait_rows(tile_rows(i), lambda size: copy_for(k)(0, 0, size))

    @pl.when(i == last)
    def _():
        zb[...] = jnp.zeros_like(zb)

        def tail_copy(local_row, global_row, size):
            del local_row
            return pltpu.make_async_copy(zb.at[pl.ds(0, size)], xs_ref.at[pl.ds(global_row, size)], sems.at[0])

        _segment_copies(last + 1, start_s, len_s, dst_s, tail_copy, wait=False)
        _segment_copies(last + 1, start_s, len_s, dst_s, tail_copy, wait=True)

        def fill(j, carry):
            cp = pltpu.make_async_copy(zb, xs_ref.at[pl.ds(pl.multiple_of(j * TMF, TMF), TMF)], sems.at[1])
            cp.start()
            cp.wait()
            return carry

        lax.fori_loop(dst_s[(last + 2) * N_EXPERTS], xs_ref.shape[0] // TMF, fill, 0)


def _dispatch(hp, ri, rwt, seg_start, seg_len, seg_dst, n_rows):
    ntiles = ri.shape[0]
    grid_spec = pltpu.PrefetchScalarGridSpec(
        num_scalar_prefetch=3,
        grid=(ntiles,),
        in_specs=[
            pl.BlockSpec((TM, D_MODEL // 2), lambda i, *_: (i, 0)),
            pl.BlockSpec((1, 4, TM), lambda i, *_: (i, 0, 0)),
            pl.BlockSpec((1, 2, TM), lambda i, *_: (i, 0, 0)),
        ],
        out_specs=pl.BlockSpec(memory_space=pl.ANY),
        scratch_shapes=[pltpu.VMEM((2, SORT_ROWS, XS_COLS), U32), pltpu.VMEM((TMF, XS_COLS), U32),
                        pltpu.SemaphoreType.DMA((2,))],
    )
    return pl.pallas_call(
        _dispatch_kernel,
        grid_spec=grid_spec,
        out_shape=jax.ShapeDtypeStruct((n_rows, XS_COLS), U32),
        compiler_params=_cparams(("arbitrary",)),
    )(seg_start, seg_len, seg_dst, hp, ri, rwt)


def _ffn_kernel(te_ref, tv_ref, ts_ref, xs_ref, wg_ref, wu_ref, wd_ref, ys_ref, wgb, wub, wdb):
    del ts_ref
    i = pl.program_id(0)
    prev = te_ref[jnp.maximum(i - 1, 0)]
    fresh = (i == 0) | (te_ref[i] != prev)

    @pl.when(fresh)
    def _():
        wgb[...] = wg_ref[0, 0].astype(BF16)
        wub[...] = wu_ref[0, 0].astype(BF16)
        wdb[...] = wd_ref[0, 0].astype(BF16)

    @pl.when(tv_ref[i] > 0)
    def _():
        hi, lo = _unpack_rows(xs_ref[:, 0:D_MODEL // 2])
        xb = jnp.concatenate([hi.astype(BF16), lo.astype(BF16)], axis=1)
        w = lax.bitcast_convert_type(xs_ref[:, D_MODEL // 2:D_MODEL // 2 + 1], F32)
        g = jnp.dot(xb, wgb[...], preferred_element_type=F32)
        u = jnp.dot(xb, wub[...], preferred_element_type=F32)
        he = ((g * (1.0 + jnp.tanh(0.5 * g))) * (u * (0.5 * w))).astype(BF16)
        ys_ref[...] = _pack_rows(jnp.dot(he, wdb[...], preferred_element_type=F32))

    @pl.when(tv_ref[i] == 0)
    def _():
        ys_ref[...] = jnp.zeros_like(ys_ref)


def _ffn(xs, tile_e, tile_v, tile_src, wg, wu, wd, layer):
    n_rows = xs.shape[0]
    n_tiles = n_rows // TMF
    grid_spec = pltpu.PrefetchScalarGridSpec(
        num_scalar_prefetch=3,
        grid=(n_tiles,),
        in_specs=[
            pl.BlockSpec((TMF, XS_COLS), lambda i, te, tv, ts: (ts[i], 0)),
            pl.BlockSpec((1, 1, D_MODEL, D_EXPERT), lambda i, te, tv, ts: (layer, te[i], 0, 0)),
            pl.BlockSpec((1, 1, D_MODEL, D_EXPERT), lambda i, te, tv, ts: (layer, te[i], 0, 0)),
            pl.BlockSpec((1, 1, D_EXPERT, D_MODEL), lambda i, te, tv, ts: (layer, te[i], 0, 0)),
        ],
        out_specs=pl.BlockSpec((TMF, D_MODEL // 2), lambda i, te, tv, ts: (i, 0)),
        scratch_shapes=[
            pltpu.VMEM((D_MODEL, D_EXPERT), BF16),
            pltpu.VMEM((D_MODEL, D_EXPERT), BF16),
            pltpu.VMEM((D_EXPERT, D_MODEL), BF16),
        ],
    )
    return pl.pallas_call(
        _ffn_kernel,
        grid_spec=grid_spec,
        out_shape=jax.ShapeDtypeStruct((n_rows, D_MODEL // 2), U32),
        compiler_params=_cparams(("arbitrary",)),
    )(tile_e, tile_v, tile_src, xs, wg, wu, wd)


def _combine_kernel(start_s, len_s, dst_s, x_ref, ri_ref, g2_ref, fg_ref, ys_ref, o_ref, yb, sems, *, final):
    nt = pl.num_programs(1)
    i = pl.program_id(0) * nt + pl.program_id(1)
    last = pl.num_programs(0) * nt - 1
    slot = lax.rem(i, 2)

    def copy_for(slot_k):
        def make_copy(local_row, global_row, size):
            return pltpu.make_async_copy(ys_ref.at[pl.ds(global_row, size)],
                                         yb.at[slot_k, pl.ds(local_row, size)], sems.at[slot_k])
        return make_copy

    @pl.when(i == 0)
    def _():
        yb[...] = jnp.zeros_like(yb)
        _segment_copies(i, start_s, len_s, dst_s, copy_for(0), wait=False)

    for k in range(2):
        @pl.when(slot == k)
        def _(k=k):
            @pl.when(i < last)
            def _():
                _segment_copies(i + 1, start_s, len_s, dst_s, copy_for(1 - k), wait=False)

            j = i * N_EXPERTS + N_EXPERTS - 1
            _wait_rows(start_s[j] + len_s[j], lambda size: copy_for(k)(0, 0, size))

    hi, lo = _unpack_rows(yb[slot])
    m1, m2 = _sort_matrix(ri_ref)
    perm = jnp.where(m1 | m2, 1.0, 0.0).astype(BF16)
    tn_dims = (((0,), (0,)), ((), ()))
    moe = jnp.concatenate(
        [lax.dot_general(perm, hi.astype(BF16), tn_dims, preferred_element_type=F32),
         lax.dot_general(perm, lo.astype(BF16), tn_dims, preferred_element_type=F32)], axis=1)
    x = x_ref[0] + g2_ref[0] * moe
    if final:
        ms = jnp.mean(x * x, axis=-1, keepdims=True)
        x = x * lax.rsqrt(ms + EPS) * fg_ref[...]
    o_ref[0] = x


def _combine(xa, ys, ri, seg_start, seg_len, seg_dst, mods, final_g, t0, final):
    b, ntok, _ = xa.shape
    nt = ntok // TM - t0
    mod_row = lambda bi, t: jnp.where(t + t0 == 0, b, bi)
    if final:
        out_shape = jax.ShapeDtypeStruct((b, nt * TM, D_MODEL), F32)
        out_spec = pl.BlockSpec((1, TM, D_MODEL), lambda bi, t, *_: (bi, t, 0))
        aliases = {}
    else:
        out_shape = jax.ShapeDtypeStruct(xa.shape, F32)
        out_spec = pl.BlockSpec((1, TM, D_MODEL), lambda bi, t, *_: (bi, t + t0, 0))
        aliases = {3: 0}
    grid_spec = pltpu.PrefetchScalarGridSpec(
        num_scalar_prefetch=3,
        grid=(b, nt),
        in_specs=[
            pl.BlockSpec((1, TM, D_MODEL), lambda bi, t, *_: (bi, t + t0, 0)),
            pl.BlockSpec((1, 4, TM), lambda bi, t, *_: (bi * nt + t, 0, 0)),
            pl.BlockSpec((1, 1, D_MODEL), lambda bi, t, *_: (mod_row(bi, t), 0, 5)),
            pl.BlockSpec((1, D_MODEL), lambda bi, t, *_: (0, 0)),
            pl.BlockSpec(memory_space=pl.ANY),
        ],
        out_specs=out_spec,
        scratch_shapes=[pltpu.VMEM((2, SORT_ROWS, D_MODEL // 2), U32), pltpu.SemaphoreType.DMA((2,))],
    )
    return pl.pallas_call(
        functools.partial(_combine_kernel, final=final),
        grid_spec=grid_spec,
        out_shape=out_shape,
        input_output_aliases=aliases,
        compiler_params=_cparams(("arbitrary", "arbitrary")),
    )(seg_start, seg_len, seg_dst, xa, ri, mods, final_g.reshape(1, -1), ys)


def _route_tables(meta, cnt):
    ntiles = meta.shape[0]
    m = meta[:, :, 0].reshape(ntiles, 3, N_EXPERTS)
    seg_start, seg_len, before = m[:, 0], m[:, 1], m[:, 2]
    totals = cnt[:, 0].astype(I32)
    region = ((totals + TMF - 1) // TMF) * TMF
    ends = jnp.cumsum(region)
    seg_dst = (ends - region)[None, :] + before
    seg_start = jnp.concatenate([seg_start, jnp.zeros((1, N_EXPERTS), I32)], axis=0)
    seg_len = jnp.concatenate([seg_len, (region - totals)[None, :]], axis=0)
    seg_dst = jnp.concatenate([seg_dst, (ends - region + totals)[None, :]], axis=0)
    max_rows = 2 * ntiles * TM + ntiles * N_EXPERTS * (SEG_ALIGN - 1)
    n_tiles = max_rows // TMF + 1 + N_EXPERTS
    tile_id = jnp.arange(n_tiles, dtype=I32)
    starts = tile_id * TMF
    tile_e = jnp.minimum(jnp.sum(starts[:, None] >= ends[None, :], axis=1), N_EXPERTS - 1).astype(I32)
    valid = starts < ends[-1]
    last_used = jnp.maximum(ends[-1] // TMF - 1, 0)
    tile_e = jnp.where(valid, tile_e, tile_e[last_used])
    tile_src = jnp.minimum(tile_id, last_used)
    flat = lambda v: v.reshape(-1).astype(I32)
    seg_dst = jnp.concatenate([flat(seg_dst), (ends[-1:] // TMF).astype(I32)])
    return flat(seg_start), flat(seg_len), seg_dst, tile_e, valid.astype(I32), tile_src, n_tiles * TMF


def _rope_tables(n_lat):
    quarter = HEAD_DIM // 4
    inv = ROPE_BASE ** (-jnp.arange(quarter, dtype=F32) / quarter)
    n_rows = n_lat // GRID_W
    rows = jnp.repeat(jnp.arange(n_rows, dtype=F32), GRID_W)
    cols = jnp.tile(jnp.arange(GRID_W, dtype=F32), n_rows)
    ang = jnp.concatenate([rows[:, None] * inv, cols[:, None] * inv], axis=-1)
    cos = jnp.cos(ang)
    sin = jnp.sin(ang)
    cos_t = jnp.tile(cos, (1, LANES // cos.shape[1]))
    sin_t = jnp.tile(jnp.concatenate([-sin, sin], axis=-1), (1, LANES // HEAD_DIM))
    cos_t = jnp.concatenate([jnp.ones((CTX_LEN, LANES), F32), cos_t], axis=0)
    sin_t = jnp.concatenate([jnp.zeros((CTX_LEN, LANES), F32), sin_t], axis=0)
    return cos_t, sin_t


def kernel(x, c, ctx, c_ctx, w_ada, b_ada, norm_mix_g, norm_ffn_g, w_in, diff_lambda, diff_subln_g, swa_sink,
           ret_decay, ret_gn_g, w_branch, w_out, router_w, router_b, w_exp_gate, w_exp_up, w_exp_down, final_g):
    b, n_lat, _ = x.shape
    depth = w_ada.shape[0]
    assert ctx.shape[1] == CTX_LEN == TM and n_lat % TM == 0 and n_lat >= 2 * TM

    pad_rows = (-(b + 1)) % 8
    c_all = jnp.concatenate([c, c_ctx[None, :], jnp.zeros((pad_rows, D_MODEL), F32)], axis=0)
    mods_all = _ada(c_all, w_ada, b_ada)
    cos_t, sin_t = _rope_tables(n_lat)
    xa = x
    rw_t = router_w.T.astype(BF16)
    rb_col = router_b.reshape(-1, 1).astype(F32)

    for l in range(depth):
        last = l == depth - 1
        t0 = 1 if last else 0
        lam_init = 0.8 - 0.6 * math.exp(-0.3 * l)
        mods = mods_all[l].reshape(mods_all.shape[1], 1, -1)
        ctx_in = ctx if l == 0 else None
        p = _proj(ctx_in, xa, mods, norm_mix_g[l], cos_t, sin_t, w_in[l].astype(BF16))
        oa = _diff(p, diff_lambda[l], diff_subln_g[l], lam_init, t0)
        ob = _win(p, swa_sink[l], t0)
        oc = _ret(p, ret_decay[l])
        xa, hp, ri, rwt, meta, cnt = _merge(ctx_in, xa, p, oa, ob, oc, mods, ret_gn_g[l], norm_ffn_g[l],
                                            w_branch[l].astype(BF16), w_out[l].astype(BF16), rw_t, rb_col, t0)
        seg_start, seg_len, seg_dst, tile_e, tile_v, tile_src, n_rows = _route_tables(meta, cnt)
        xs = _dispatch(hp, ri, rwt, seg_start, seg_len, seg_dst, n_rows)
        ys = _ffn(xs, tile_e, tile_v, tile_src, w_exp_gate, w_exp_up, w_exp_down, l)
        xa = _combine(xa, ys, ri, seg_start, seg_len, seg_dst, mods, final_g, t0, last)
    return xa
```

```python
import functools
import math

import jax
import jax.numpy as jnp
from jax import lax
from jax.experimental import pallas as pl
from jax.experimental.pallas import tpu as pltpu

F32 = jnp.float32
BF16 = jnp.bfloat16
U32 = jnp.uint32
I32 = jnp.int32

D_MODEL = 1024
CTX_LEN = 256
GRID_W = 64
ROPE_BASE = 10000.0
EPS = 1e-6
LOG2E = 1.4426950408889634
HEAD_DIM = 64
WINDOW = 128
RT_CHUNK = 128
N_EXPERTS = 16
N_GROUPS = 4
D_EXPERT = 512
LANES = 128
TM = 256
DIFF_HEADS = 4
TMF = 512
SEG_ALIGN = 8
SEG_SIZES = (256, 128, 64, 32, 16, 8)
SORT_ROWS = 640
XS_COLS = D_MODEL // 2 + 128
D_IN = 6912
N_SRC_BLK = D_IN // LANES
N_OUT_BLK = N_SRC_BLK + 2
VMEM_LIMIT = 56 * 1024 * 1024

BLK_AQ, BLK_AK, BLK_AV = 0, 4, 8
BLK_BQ, BLK_BK, BLK_BV = 12, 16, 18
BLK_CQ, BLK_CK, BLK_CV, BLK_CG, BLK_GATE = 20, 22, 24, 28, 32


def _cparams(sem):
    return pltpu.CompilerParams(dimension_semantics=sem, vmem_limit_bytes=VMEM_LIMIT)


def _ada_kernel(c_ref, w_ref, b_ref, o_ref):
    c = c_ref[...]
    s = (c * jax.nn.sigmoid(c)).astype(BF16)
    o_ref[0] = jnp.dot(s, w_ref[0].astype(BF16), preferred_element_type=F32) + b_ref[0]


def _ada(c_all, w_ada, b_ada):
    depth = w_ada.shape[0]
    rows = c_all.shape[0]
    nj = w_ada.shape[2] // D_MODEL
    return pl.pallas_call(
        _ada_kernel,
        grid=(depth, nj),
        in_specs=[
            pl.BlockSpec((rows, D_MODEL), lambda l, j: (0, 0)),
            pl.BlockSpec((1, D_MODEL, D_MODEL), lambda l, j: (l, 0, j)),
            pl.BlockSpec((1, 1, D_MODEL), lambda l, j: (l, 0, j)),
        ],
        out_specs=pl.BlockSpec((1, rows, D_MODEL), lambda l, j: (l, 0, j)),
        out_shape=jax.ShapeDtypeStruct((depth, rows, w_ada.shape[2]), F32),
        compiler_params=_cparams(("arbitrary", "arbitrary")),
    )(c_all, w_ada, b_ada.reshape(depth, 1, -1))


def _rope(a, cos, sin_signed, first_half):
    rot = jnp.where(first_half, pltpu.roll(a, LANES - HEAD_DIM // 2, 1), pltpu.roll(a, HEAD_DIM // 2, 1))
    return a * cos + rot * sin_signed


def _token_tile(ctx_ref, x_ref, tile):
    if ctx_ref is None:
        return x_ref[0]
    return jnp.where(tile == 0, ctx_ref[0], x_ref[0])


def _token_specs(ctx, xa, t0):
    if ctx is None:
        return [xa], [pl.BlockSpec((1, TM, D_MODEL), lambda bi, t, *_: (bi, t + t0, 0))]
    return [ctx, xa], [pl.BlockSpec((1, TM, D_MODEL), lambda bi, t, *_: (bi, 0, 0)),
                       pl.BlockSpec((1, TM, D_MODEL), lambda bi, t, *_: (bi, jnp.maximum(t + t0 - 1, 0), 0))]


CTX_KV_CHUNKS = (2, 3, 4, 5, 8, 10, 11, 12)


def _proj_kernel(*refs, split, ctx_kv_only):
    ctx_ref, refs = (refs[0], refs[1:]) if split else (None, refs)
    x_ref, sh_ref, sc_ref, g_ref, cos_ref, sin_ref, w_ref, o_ref = refs
    t = pl.program_id(1)
    x = _token_tile(ctx_ref, x_ref, t)
    ms = jnp.mean(x * x, axis=-1, keepdims=True)
    h = x * lax.rsqrt(ms + EPS) * g_ref[...]
    h = h * (1.0 + sc_ref[0]) + sh_ref[0]
    hb = h.astype(BF16)
    cos = cos_ref[...]
    sin = sin_ref[...]
    lane = lax.broadcasted_iota(I32, (1, LANES), 1)
    first_half = (lane & (HEAD_DIM - 1)) < (HEAD_DIM // 2)
    low = lane < HEAD_DIM
    scale = HEAD_DIM ** -0.5

    def project(chunks):
        for c in range(N_SRC_BLK // 2):
            if c not in chunks:
                for src in (2 * c, 2 * c + 1):
                    dst = src if src < 16 else src + 2
                    o_ref[0, :, dst * LANES:(dst + 1) * LANES] = jnp.zeros((TM, LANES), BF16)
                continue
            acc = jnp.dot(hb, w_ref[:, c * 2 * LANES:(c + 1) * 2 * LANES], preferred_element_type=F32)
            for half in range(2):
                src = 2 * c + half
                a = acc[:, half * LANES:(half + 1) * LANES]
                if src < 8 or 12 <= src < 17:
                    a = _rope(a, cos, sin, first_half)
                if src < 4 or 12 <= src < 16:
                    a = a * (scale * LOG2E)
                if 20 <= src < 22:
                    a = a * scale
                if src >= 26:
                    a = a * 0.5
                if src in (16, 17):
                    sw = pltpu.roll(a, HEAD_DIM, 1)
                    dst = BLK_BK if src == 16 else BLK_BV
                    o_ref[0, :, dst * LANES:(dst + 1) * LANES] = jnp.where(low, a, sw).astype(BF16)
                    o_ref[0, :, (dst + 1) * LANES:(dst + 2) * LANES] = jnp.where(low, sw, a).astype(BF16)
                else:
                    dst = src if src < 16 else src + 2
                    o_ref[0, :, dst * LANES:(dst + 1) * LANES] = a.astype(BF16)

    all_chunks = tuple(range(N_SRC_BLK // 2))
    if ctx_kv_only:
        @pl.when(t == 0)
        def _():
            project(CTX_KV_CHUNKS)

        @pl.when(t > 0)
        def _():
            project(all_chunks)
    else:
        project(all_chunks)


def _proj(ctx, xa, mods, norm_g, cos_t, sin_t, w_in_bf, ctx_kv_only):
    b = xa.shape[0]
    ntok = xa.shape[1] + (0 if ctx is None else CTX_LEN)
    nt = ntok // TM
    mod_row = lambda bi, t: jnp.where(t == 0, b, bi)
    tok_args, tok_specs = _token_specs(ctx, xa, 0)
    return pl.pallas_call(
        functools.partial(_proj_kernel, split=ctx is not None, ctx_kv_only=ctx_kv_only),
        grid=(b, nt),
        in_specs=tok_specs + [
            pl.BlockSpec((1, 1, D_MODEL), lambda bi, t: (mod_row(bi, t), 0, 0)),
            pl.BlockSpec((1, 1, D_MODEL), lambda bi, t: (mod_row(bi, t), 0, 1)),
            pl.BlockSpec((1, D_MODEL), lambda bi, t: (0, 0)),
            pl.BlockSpec((TM, LANES), lambda bi, t: (t, 0)),
            pl.BlockSpec((TM, LANES), lambda bi, t: (t, 0)),
            pl.BlockSpec((D_MODEL, D_IN), lambda bi, t: (0, 0)),
        ],
        out_specs=pl.BlockSpec((1, TM, N_OUT_BLK * LANES), lambda bi, t: (bi, t, 0)),
        out_shape=jax.ShapeDtypeStruct((b, ntok, N_OUT_BLK * LANES), BF16),
        compiler_params=_cparams(("arbitrary", "arbitrary")),
    )(*tok_args, mods, mods, norm_g.reshape(1, -1), cos_t, sin_t, w_in_bf)


def _diff_kernel(lam_ref, g_ref, q_ref, k_ref, v_ref, o_ref, *, lam_init, t0, ntok):
    t = pl.program_id(2) + t0
    lv = lam_ref[...]
    lam = (jnp.exp(jnp.sum(lv[0:1] * lv[1:2], axis=-1, keepdims=True))
           - jnp.exp(jnp.sum(lv[2:3] * lv[3:4], axis=-1, keepdims=True)) + lam_init)
    lane = lax.broadcasted_iota(I32, (1, LANES), 1)
    nt_dims = (((1,), (1,)), ((), ()))

    def attend(nk):
        def scores(hd):
            cols = slice(hd * LANES, (hd + 1) * LANES)
            q = q_ref[0, :, cols]
            zero = jnp.zeros_like(q)
            q0 = jnp.where(lane < HEAD_DIM, q, zero)
            q1 = jnp.where(lane >= HEAD_DIM, q, zero)
            k = k_ref[0, 0:nk, cols]
            return (lax.dot_general(q0, k, nt_dims, preferred_element_type=F32),
                    lax.dot_general(q1, k, nt_dims, preferred_element_type=F32))

        ones_blk = jnp.where(lax.broadcasted_iota(I32, (nk, LANES), 1) == 0, 1.0, 0.0).astype(BF16)
        nxt = scores(0)
        for hd in range(DIFF_HEADS):
            cols = slice(hd * LANES, (hd + 1) * LANES)
            v1 = jnp.concatenate([v_ref[0, 0:nk, cols], ones_blk], axis=1)
            s0, s1 = nxt
            if hd + 1 < DIFF_HEADS:
                nxt = scores(hd + 1)
            e0 = jnp.exp2(s0 - jnp.max(s0, axis=-1, keepdims=True)).astype(BF16)
            e1 = jnp.exp2(s1 - jnp.max(s1, axis=-1, keepdims=True)).astype(BF16)
            ov0 = jnp.dot(e0, v1, preferred_element_type=F32)
            ov1 = jnp.dot(e1, v1, preferred_element_type=F32)
            r0 = 1.0 / ov0[:, LANES:LANES + 1]
            r1 = lam / ov1[:, LANES:LANES + 1]
            o = ov0[:, 0:LANES] * r0 - ov1[:, 0:LANES] * r1
            o = o * lax.rsqrt(jnp.mean(o * o, axis=-1, keepdims=True) + EPS) * g_ref[...] * (1.0 - lam_init)
            o_ref[0, :, cols] = o.astype(BF16)

    if t0 == 0:
        @pl.when(t == 0)
        def _():
            attend(CTX_LEN)

        @pl.when(t > 0)
        def _():
            attend(ntok)
    else:
        attend(ntok)


def _diff(p, lam_vec, subln_g, lam_init, t0):
    b, ntok, _ = p.shape
    nt = ntok // TM - t0
    heads = 4
    hw = DIFF_HEADS * LANES
    return pl.pallas_call(
        functools.partial(_diff_kernel, lam_init=lam_init, t0=t0, ntok=ntok),
        grid=(b, heads // DIFF_HEADS, nt),
        in_specs=[
            pl.BlockSpec((4, HEAD_DIM), lambda bi, h, t: (0, 0)),
            pl.BlockSpec((1, LANES), lambda bi, h, t: (0, 0)),
            pl.BlockSpec((1, TM, hw), lambda bi, h, t: (bi, t + t0, BLK_AQ // DIFF_HEADS + h)),
            pl.BlockSpec((1, ntok, hw), lambda bi, h, t: (bi, 0, BLK_AK // DIFF_HEADS + h)),
            pl.BlockSpec((1, ntok, hw), lambda bi, h, t: (bi, 0, BLK_AV // DIFF_HEADS + h)),
        ],
        out_specs=pl.BlockSpec((1, TM, hw), lambda bi, h, t: (bi, t, h)),
        out_shape=jax.ShapeDtypeStruct((b, nt * TM, heads * LANES), BF16),
        compiler_params=_cparams(("arbitrary", "arbitrary", "arbitrary")),
    )(lam_vec, subln_g.reshape(1, -1), p, p, p)


def _win_kernel(sink_ref, q_ref, k_ref, v_ref, o_ref, *, t0, ntok):
    t = pl.program_id(1) + t0
    lane = lax.broadcasted_iota(I32, (1, LANES), 1)
    low = lane < HEAD_DIM
    nt_dims = (((1,), (1,)), ((), ()))
    wk = 2 * TM
    start = jnp.clip(t * TM - WINDOW, CTX_LEN, ntok - wk)
    start = pl.multiple_of(start, WINDOW)
    qpos = t * TM + lax.broadcasted_iota(I32, (TM, wk), 0)
    kpos = start + lax.broadcasted_iota(I32, (TM, wk), 1)
    in_band = (jnp.abs(kpos - qpos) <= WINDOW) & (t > 0)
    bias = jnp.concatenate([jnp.zeros((TM, CTX_LEN), F32), jnp.where(in_band, 0.0, -jnp.inf).astype(F32)], axis=1)
    n_heads = 8
    group = n_heads // 2
    keys, vals = [], []
    ones_blk = jnp.where(lax.broadcasted_iota(I32, (CTX_LEN + wk, LANES), 1) == 0, 1.0, 0.0).astype(BF16)
    for g in range(2):
        cols = slice(g * LANES, (g + 1) * LANES)
        keys.append(jnp.concatenate([k_ref[0, 0:CTX_LEN, cols], k_ref[0, pl.ds(start, wk), cols]], axis=0))
        vg = jnp.concatenate([v_ref[0, 0:CTX_LEN, cols], v_ref[0, pl.ds(start, wk), cols]], axis=0)
        vals.append(jnp.concatenate([vg, ones_blk], axis=1))

    def scores(h):
        qb = q_ref[0, :, (h // 2) * LANES:(h // 2 + 1) * LANES]
        qm = jnp.where(low if h % 2 == 0 else jnp.logical_not(low), qb, jnp.zeros_like(qb))
        return lax.dot_general(qm, keys[h // group], nt_dims, preferred_element_type=F32)

    nxt = scores(0)
    outs = []
    for h in range(n_heads):
        s = nxt + bias
        if h + 1 < n_heads:
            nxt = scores(h + 1)
        sk = sink_ref[h] * LOG2E
        m = jnp.maximum(jnp.max(s, axis=-1, keepdims=True), sk)
        e = jnp.exp2(s - m).astype(BF16)
        ov = jnp.dot(e, vals[h // group], preferred_element_type=F32)
        l = ov[:, LANES:LANES + 1] + jnp.exp2(sk - m)
        outs.append(ov[:, 0:LANES] * (1.0 / l))
        if h % 2 == 1:
            o_ref[0, :, (h // 2) * LANES:(h // 2 + 1) * LANES] = jnp.where(low, outs[h - 1], outs[h]).astype(BF16)


def _win(p, sink, t0):
    b, ntok, _ = p.shape
    nt = ntok // TM - t0
    return pl.pallas_call(
        functools.partial(_win_kernel, t0=t0, ntok=ntok),
        grid=(b, nt),
        in_specs=[
            pl.BlockSpec(memory_space=pltpu.SMEM),
            pl.BlockSpec((1, TM, 4 * LANES), lambda bi, t: (bi, t + t0, BLK_BQ // 4)),
            pl.BlockSpec((1, ntok, 2 * LANES), lambda bi, t: (bi, 0, BLK_BK // 2)),
            pl.BlockSpec((1, ntok, 2 * LANES), lambda bi, t: (bi, 0, BLK_BV // 2)),
        ],
        out_specs=pl.BlockSpec((1, TM, 4 * LANES), lambda bi, t: (bi, t, 0)),
        out_shape=jax.ShapeDtypeStruct((b, nt * TM, 4 * LANES), BF16),
        compiler_params=_cparams(("arbitrary", "arbitrary")),
    )(sink, p, p, p)


def _ret_kernel(dec_ref, q_ref, k_ref, v_ref, o_ref, ob_ref, *, ntok):
    T = RT_CHUNK
    nchunk = ntok // T
    nctx = CTX_LEN // T
    x = dec_ref[...]
    lg_all = -(jnp.maximum(-x, 0.0) + jnp.log1p(jnp.exp(-jnp.abs(x))))
    lane = lax.broadcasted_iota(I32, (1, LANES), 1)
    low = lane < HEAD_DIM
    row_low = lax.broadcasted_iota(I32, (LANES, 1), 0) < HEAD_DIM
    pi = lax.broadcasted_iota(I32, (T, T), 0).astype(F32)
    pj = lax.broadcasted_iota(I32, (T, T), 1).astype(F32)
    pcol = lax.broadcasted_iota(I32, (T, 1), 0).astype(F32)
    nt_dims = (((1,), (1,)), ((), ()))
    tn_dims = (((0,), (0,)), ((), ()))

    decay, qdp, kdp, cdp = [], [], [], []
    for direction in range(2):
        lgs = [lg_all[direction:direction + 1, h:h + 1] for h in range(4)]
        if direction == 0:
            dist = pi - pj
            qd = [jnp.exp((pcol + 1.0) * lg) for lg in lgs]
            kd = [jnp.exp((T - 1.0 - pcol) * lg) for lg in lgs]
        else:
            dist = pj - pi
            qd = [jnp.exp((T - pcol) * lg) for lg in lgs]
            kd = [jnp.exp(pcol * lg) for lg in lgs]
        decay.append([jnp.where(dist >= 0, jnp.exp(jnp.maximum(dist, 0.0) * lg), 0.0) for lg in lgs])
        qdp.append([jnp.where(low, qd[2 * pr], qd[2 * pr + 1]) for pr in range(2)])
        kdp.append([jnp.where(low, kd[2 * pr], kd[2 * pr + 1]) for pr in range(2)])
        cdp.append([jnp.where(row_low, jnp.exp(T * lgs[2 * pr]), jnp.exp(T * lgs[2 * pr + 1])) for pr in range(2)])

    def step(i, states):
        chunks = (i, jnp.where(i < nctx, nctx - 1 - i, nchunk + nctx - 1 - i))
        halves = (low, jnp.logical_not(low))
        work = []
        for direction in range(2):
            r0 = pl.multiple_of(chunks[direction] * T, T)
            for pair in range(2):
                qb = q_ref[0, pl.ds(r0, T), pair * LANES:(pair + 1) * LANES]
                kb = k_ref[0, pl.ds(r0, T), pair * LANES:(pair + 1) * LANES]
                v2 = v_ref[0, pl.ds(r0, T), pair * 2 * LANES:(pair + 1) * 2 * LANES]
                kbd = (kb.astype(F32) * kdp[direction][pair]).astype(BF16)
                qsb = (qb.astype(F32) * qdp[direction][pair]).astype(BF16)
                zq = jnp.zeros_like(qb)
                q2 = jnp.concatenate([jnp.where(hm, qb, zq) for hm in halves], axis=0)
                inner2 = lax.dot_general(q2, kb, nt_dims, preferred_element_type=F32)
                kv2 = lax.dot_general(kbd, v2, tn_dims, preferred_element_type=F32)
                work.append((direction, pair, r0, inner2, kv2, qsb, v2))
        new_states = []
        for direction, pair, r0, inner2, kv2, qsb, v2 in work:
            out_ref = o_ref.at[0] if direction == 0 else ob_ref
            st = states[2 * direction + pair]
            stb = st.astype(BF16)
            zq = jnp.zeros_like(qsb)
            for sub in range(2):
                h = 2 * pair + sub
                inner = inner2[sub * T:(sub + 1) * T] * decay[direction][h]
                lhs = jnp.concatenate([inner.astype(BF16), jnp.where(halves[sub], qsb, zq)], axis=1)
                rhs = jnp.concatenate([v2[:, sub * LANES:(sub + 1) * LANES], stb], axis=0)
                out_ref[pl.ds(r0, T), h * LANES:(h + 1) * LANES] = jnp.dot(lhs, rhs, preferred_element_type=F32)
            new_states.append(st * cdp[direction][pair]
                              + jnp.where(row_low, kv2[:, 0:LANES], kv2[:, LANES:2 * LANES]))
        return tuple(new_states)

    zero = jnp.zeros((LANES, LANES), F32)
    lax.fori_loop(0, nchunk, step, (zero,) * 4, unroll=3)
    o_ref[0] = o_ref[0] + ob_ref[...]


def _ret(p, ret_decay):
    b, ntok, _ = p.shape
    return pl.pallas_call(
        functools.partial(_ret_kernel, ntok=ntok),
        grid=(b,),
        in_specs=[
            pl.BlockSpec((2, 4), lambda bi: (0, 0)),
            pl.BlockSpec((1, ntok, 2 * LANES), lambda bi: (bi, 0, BLK_CQ // 2)),
            pl.BlockSpec((1, ntok, 2 * LANES), lambda bi: (bi, 0, BLK_CK // 2)),
            pl.BlockSpec((1, ntok, 4 * LANES), lambda bi: (bi, 0, BLK_CV // 4)),
        ],
        out_specs=pl.BlockSpec((1, ntok, 4 * LANES), lambda bi: (bi, 0, 0)),
        out_shape=jax.ShapeDtypeStruct((b, ntok, 4 * LANES), F32),
        scratch_shapes=[pltpu.VMEM((ntok, 4 * LANES), F32)],
        compiler_params=_cparams(("arbitrary",)),
    )(ret_decay, p, p, p)


def _pack_rows(h, is_bf16_valued=False):
    half = h.shape[1] // 2
    rnd = (lambda v: v) if is_bf16_valued else (lambda v: v.astype(BF16).astype(F32))
    hi = lax.bitcast_convert_type(rnd(h[:, :half]), U32)
    lo = lax.bitcast_convert_type(rnd(h[:, half:]), U32)
    return hi | (lo >> 16)


def _unpack_rows(p):
    hi = lax.bitcast_convert_type(p & jnp.uint32(0xFFFF0000), F32)
    lo = lax.bitcast_convert_type(p << 16, F32)
    return hi, lo


def _merge_kernel(*refs, split, t0):
    ctx_ref, refs = (refs[0], refs[1:]) if split else (None, refs)
    (x_ref, oa_ref, ob_ref, oc_ref, cg_ref, ga_ref, gb_ref, gc_ref, g1_ref, sh2_ref, sc2_ref,
     gn_ref, nf_ref, wb_ref, wo_ref, rw_ref, rb_ref,
     xo_ref, hp_ref, ri_ref, rwt_ref, meta_ref, cnt_ref) = refs
    first = (pl.program_id(0) == 0) & (pl.program_id(1) == 0)

    @pl.when(first)
    def _():
        cnt_ref[...] = jnp.zeros_like(cnt_ref)

    da = jnp.dot(oa_ref[0], wb_ref[0], preferred_element_type=F32)
    db = jnp.dot(ob_ref[0], wb_ref[1], preferred_element_type=F32)

    oc = oc_ref[0]
    cgv = cg_ref[0].astype(F32)
    gn = gn_ref[...]
    parts = []
    for h in range(4):
        o = oc[:, h * LANES:(h + 1) * LANES]
        mu = jnp.mean(o, axis=-1, keepdims=True)
        dlt = o - mu
        var = jnp.mean(dlt * dlt, axis=-1, keepdims=True)
        y = dlt * lax.rsqrt(var + EPS) * gn[:, h * LANES:(h + 1) * LANES]
        cgh = cgv[:, h * LANES:(h + 1) * LANES]
        parts.append((y * (cgh * (1.0 + jnp.tanh(cgh)))).astype(BF16))
    ocn = jnp.concatenate(parts, axis=1)

    dc = jnp.dot(ocn, wb_ref[2], preferred_element_type=F32)
    y = (1.0 + jnp.tanh(ga_ref[0].astype(F32))) * da
    y += (1.0 + jnp.tanh(gb_ref[0].astype(F32))) * db
    y += (1.0 + jnp.tanh(gc_ref[0].astype(F32))) * dc
    y2 = jnp.dot(y.astype(BF16), wo_ref[...], preferred_element_type=F32)
    x = _token_tile(ctx_ref, x_ref, pl.program_id(1) + t0) + (0.5 * g1_ref[0]) * y2
    xo_ref[0] = x

    ms = jnp.mean(x * x, axis=-1, keepdims=True)
    h2 = x * lax.rsqrt(ms + EPS) * nf_ref[...]
    h2 = h2 * (1.0 + sc2_ref[0]) + sh2_ref[0]
    h2b = h2.astype(BF16)
    hp_ref[...] = h2b

    nt_dims = (((1,), (1,)), ((), ()))
    logits = lax.dot_general(rw_ref[...], h2b, nt_dims, preferred_element_type=F32)
    sc = jax.nn.sigmoid(logits)
    bi = sc + rb_ref[...]
    ei_i = lax.broadcasted_iota(I32, (N_EXPERTS, 1), 0)
    ei = ei_i.astype(F32)
    epg = N_EXPERTS // N_GROUPS
    egroup = (ei_i >> 2).astype(F32)
    gsum = []
    for g in range(N_GROUPS):
        r = [bi[epg * g + j:epg * g + j + 1, :] for j in range(epg)]
        m = r[0] + r[1]
        for a in range(epg):
            for c in range(a + 1, epg):
                if (a, c) != (0, 1):
                    m = jnp.maximum(m, r[a] + r[c])
        gsum.append(m)
    best = jnp.zeros_like(gsum[0])
    bsc = gsum[0]
    for g in range(1, N_GROUPS):
        upd = gsum[g] > bsc
        best = jnp.where(upd, float(g), best)
        bsc = jnp.where(upd, gsum[g], bsc)
    neg = -jnp.inf
    masked = jnp.where(egroup == best, bi, neg)
    m1 = jnp.max(masked, axis=0, keepdims=True)
    i1 = jnp.min(jnp.where(masked == m1, ei, float(N_EXPERTS)), axis=0, keepdims=True)
    masked2 = jnp.where(ei == i1, neg, masked)
    m2 = jnp.max(masked2, axis=0, keepdims=True)
    i2 = jnp.min(jnp.where(masked2 == m2, ei, float(N_EXPERTS)), axis=0, keepdims=True)
    sel1 = ei == i1
    sel2 = ei == i2
    w1 = jnp.sum(jnp.where(sel1, sc, 0.0), axis=0, keepdims=True)
    w2 = jnp.sum(jnp.where(sel2, sc, 0.0), axis=0, keepdims=True)
    ws = w1 + w2
    rwt_ref[0, 0:1, :] = w1 / ws
    rwt_ref[0, 1:2, :] = w2 / ws

    oh = (sel1 | sel2).astype(F32)
    si = lax.broadcasted_iota(I32, (TM, TM), 0)
    ti = lax.broadcasted_iota(I32, (TM, TM), 1)
    upper = (si < ti).astype(BF16)
    prefix = jnp.dot(oh.astype(BF16), upper, preferred_element_type=F32)
    n_col = jnp.sum(oh, axis=1, keepdims=True)
    seg_len = jnp.floor((n_col + (SEG_ALIGN - 1.0)) * (1.0 / SEG_ALIGN)) * SEG_ALIGN
    seg_len_b = jnp.broadcast_to(seg_len, (N_EXPERTS, LANES))
    er = lax.broadcasted_iota(I32, (N_EXPERTS, N_EXPERTS), 0)
    ec = lax.broadcasted_iota(I32, (N_EXPERTS, N_EXPERTS), 1)
    lower = (ec < er).astype(BF16)
    seg_start_b = jnp.dot(lower, seg_len_b.astype(BF16), preferred_element_type=F32)
    local = prefix + seg_start_b[:, 0:1]
    pos1 = jnp.sum(jnp.where(sel1, local, 0.0), axis=0, keepdims=True)
    pos2 = jnp.sum(jnp.where(sel2, local, 0.0), axis=0, keepdims=True)
    ri_ref[0, 0:1, :] = i1.astype(I32)
    ri_ref[0, 1:2, :] = i2.astype(I32)
    ri_ref[0, 2:3, :] = pos1.astype(I32)
    ri_ref[0, 3:4, :] = pos2.astype(I32)
    meta_ref[0, 0:N_EXPERTS, :] = seg_start_b.astype(I32)
    meta_ref[0, N_EXPERTS:2 * N_EXPERTS, :] = seg_len_b.astype(I32)
    meta_ref[0, 2 * N_EXPERTS:, :] = cnt_ref[...].astype(I32)
    cnt_ref[...] = cnt_ref[...] + seg_len_b


def _merge(ctx, xa, p, oa, ob, oc, mods, gn_g, nf_g, wb_bf, wo_bf, rw_t, rb_col, t0):
    b, ntok, _ = p.shape
    nt = ntok // TM - t0
    ntiles = b * nt
    mod_row = lambda bi, t: jnp.where(t + t0 == 0, b, bi)
    tok = lambda bi, t: (bi, t + t0, 0)
    flat = lambda bi, t: (bi * nt + t, 0, 0)
    tok_args, tok_specs = _token_specs(ctx, xa, t0)
    outs = pl.pallas_call(
        functools.partial(_merge_kernel, split=ctx is not None, t0=t0),
        grid=(b, nt),
        in_specs=tok_specs + [
            pl.BlockSpec((1, TM, 4 * LANES), lambda bi, t: (bi, t, 0)),
            pl.BlockSpec((1, TM, 4 * LANES), lambda bi, t: (bi, t, 0)),
            pl.BlockSpec((1, TM, 4 * LANES), tok),
            pl.BlockSpec((1, TM, 4 * LANES), lambda bi, t: (bi, t + t0, BLK_CG // 4)),
            pl.BlockSpec((1, TM, D_MODEL), lambda bi, t: (bi, t + t0, BLK_GATE // 8)),
            pl.BlockSpec((1, TM, D_MODEL), lambda bi, t: (bi, t + t0, BLK_GATE // 8 + 1)),
            pl.BlockSpec((1, TM, D_MODEL), lambda bi, t: (bi, t + t0, BLK_GATE // 8 + 2)),
            pl.BlockSpec((1, 1, D_MODEL), lambda bi, t: (mod_row(bi, t), 0, 2)),
            pl.BlockSpec((1, 1, D_MODEL), lambda bi, t: (mod_row(bi, t), 0, 3)),
            pl.BlockSpec((1, 1, D_MODEL), lambda bi, t: (mod_row(bi, t), 0, 4)),
            pl.BlockSpec((1, 4 * LANES), lambda bi, t: (0, 0)),
            pl.BlockSpec((1, D_MODEL), lambda bi, t: (0, 0)),
            pl.BlockSpec((3, 4 * LANES, D_MODEL), lambda bi, t: (0, 0, 0)),
            pl.BlockSpec((D_MODEL, D_MODEL), lambda bi, t: (0, 0)),
            pl.BlockSpec((N_EXPERTS, D_MODEL), lambda bi, t: (0, 0)),
            pl.BlockSpec((N_EXPERTS, 1), lambda bi, t: (0, 0)),
        ],
        out_specs=[
            pl.BlockSpec((1, TM, D_MODEL), tok),
            pl.BlockSpec((TM, D_MODEL), lambda bi, t: (bi * nt + t, 0)),
            pl.BlockSpec((1, 4, TM), flat),
            pl.BlockSpec((1, 2, TM), flat),
            pl.BlockSpec((1, 3 * N_EXPERTS, LANES), flat),
            pl.BlockSpec((N_EXPERTS, LANES), lambda bi, t: (0, 0)),
        ],
        out_shape=[
            jax.ShapeDtypeStruct((b, ntok, D_MODEL), F32),
            jax.ShapeDtypeStruct((ntiles * TM, D_MODEL), BF16),
            jax.ShapeDtypeStruct((ntiles, 4, TM), I32),
            jax.ShapeDtypeStruct((ntiles, 2, TM), F32),
            jax.ShapeDtypeStruct((ntiles, 3 * N_EXPERTS, LANES), I32),
            jax.ShapeDtypeStruct((N_EXPERTS, LANES), F32),
        ],
        input_output_aliases={} if ctx is not None else {0: 0},
        compiler_params=_cparams(("arbitrary", "arbitrary")),
    )(*tok_args, oa, ob, oc, p, p, p, p, mods, mods, mods, gn_g.reshape(1, -1), nf_g.reshape(1, -1),
      wb_bf, wo_bf, rw_t, rb_col)
    return outs


def _segment_copies(tile, start_s, len_s, dst_s, make_copy, wait):
    for e in range(N_EXPERTS):
        n = len_s[tile * N_EXPERTS + e]
        a = start_s[tile * N_EXPERTS + e]
        d = dst_s[tile * N_EXPERTS + e]
        for size in SEG_SIZES:
            @pl.when((n & size) != 0)
            def _(n=n, a=a, d=d, size=size):
                off = n & (-2 * size)
                cp = make_copy(pl.multiple_of(a + off, SEG_ALIGN), pl.multiple_of(d + off, SEG_ALIGN), size)
                if wait:
                    cp.wait()
                else:
                    cp.start()


def _wait_rows(total, make_copy):
    for size in (2 * SEG_SIZES[0],) + SEG_SIZES:
        @pl.when((total & size) != 0)
        def _(size=size):
            make_copy(size).wait()


def _sort_matrix(ri_ref):
    srow = lax.broadcasted_iota(I32, (SORT_ROWS, TM), 0)
    return srow == ri_ref[0, 2:3, :], srow == ri_ref[0, 3:4, :]


def _dispatch_kernel(start_s, len_s, dst_s, h_ref, ri_ref, rw_ref, xs_ref, sb, zb, sems):
    i = pl.program_id(0)
    last = pl.num_programs(0) - 1
    slot = lax.rem(i, 2)

    hb = h_ref[...]
    m1, m2 = _sort_matrix(ri_ref)
    perm = jnp.where(m1 | m2, 1.0, 0.0).astype(BF16)
    rows = jnp.dot(perm, hb, preferred_element_type=F32)
    w = jnp.sum(jnp.where(m1, rw_ref[0, 0:1, :], 0.0) + jnp.where(m2, rw_ref[0, 1:2, :], 0.0),
                axis=1, keepdims=True)
    sb[slot, :, 0:D_MODEL // 2] = _pack_rows(rows, is_bf16_valued=True)
    sb[slot, :, D_MODEL // 2:] = lax.bitcast_convert_type(jnp.broadcast_to(w, (SORT_ROWS, LANES)), U32)

    def copy_for(slot_k):
        def make_copy(local_row, global_row, size):
            return pltpu.make_async_copy(sb.at[slot_k, pl.ds(local_row, size)],
                                         xs_ref.at[pl.ds(global_row, size)], sems.at[slot_k])
        return make_copy

    def tile_rows(tile):
        j = tile * N_EXPERTS + N_EXPERTS - 1
        return start_s[j] + len_s[j]

    for k in range(2):
        @pl.when(slot == k)
        def _(k=k):
            _segment_copies(i, start_s, len_s, dst_s, copy_for(k), wait=False)

            @pl.when(i > 0)
            def _():
                _wait_rows(tile_rows(i - 1), lambda size: copy_for(1 - k)(0, 0, size))

            @pl.when(i == last)
            def _():
                _wait_rows(tile_rows(i), lambda size: copy_for(k)(0, 0, size))

    @pl.when(i == last)
    def _():
        zb[...] = jnp.zeros_like(zb)

        def tail_copy(local_row, global_row, size):
            del local_row
            return pltpu.make_async_copy(zb.at[pl.ds(0, size)], xs_ref.at[pl.ds(global_row, size)], sems.at[0])

        _segment_copies(last + 1, start_s, len_s, dst_s, tail_copy, wait=False)
        _segment_copies(last + 1, start_s, len_s, dst_s, tail_copy, wait=True)

        def fill(j, carry):
            cp = pltpu.make_async_copy(zb, xs_ref.at[pl.ds(pl.multiple_of(j * TMF, TMF), TMF)], sems.at[1])
            cp.start()
            cp.wait()
            return carry

        lax.fori_loop(dst_s[(last + 2) * N_EXPERTS], xs_ref.shape[0] // TMF, fill, 0)


def _dispatch(hp, ri, rwt, seg_start, seg_len, seg_dst, n_rows):
    ntiles = ri.shape[0]
    grid_spec = pltpu.PrefetchScalarGridSpec(
        num_scalar_prefetch=3,
        grid=(ntiles,),
        in_specs=[
            pl.BlockSpec((TM, D_MODEL), lambda i, *_: (i, 0)),
            pl.BlockSpec((1, 4, TM), lambda i, *_: (i, 0, 0)),
            pl.BlockSpec((1, 2, TM), lambda i, *_: (i, 0, 0)),
        ],
        out_specs=pl.BlockSpec(memory_space=pl.ANY),
        scratch_shapes=[pltpu.VMEM((2, SORT_ROWS, XS_COLS), U32), pltpu.VMEM((TMF, XS_COLS), U32),
                        pltpu.SemaphoreType.DMA((2,))],
    )
    return pl.pallas_call(
        _dispatch_kernel,
        grid_spec=grid_spec,
        out_shape=jax.ShapeDtypeStruct((n_rows, XS_COLS), U32),
        compiler_params=_cparams(("arbitrary",)),
    )(seg_start, seg_len, seg_dst, hp, ri, rwt)


def _ffn_kernel(te_ref, tv_ref, ts_ref, xs_ref, wg_ref, wu_ref, wd_ref, ys_ref, wgb, wub, wdb):
    del ts_ref
    i = pl.program_id(0)
    prev = te_ref[jnp.maximum(i - 1, 0)]
    fresh = (i == 0) | (te_ref[i] != prev)

    @pl.when(fresh)
    def _():
        wgb[...] = wg_ref[0, 0].astype(BF16)
        wub[...] = wu_ref[0, 0].astype(BF16)
        wdb[...] = wd_ref[0, 0].astype(BF16)

    @pl.when(tv_ref[i] > 0)
    def _():
        hi, lo = _unpack_rows(xs_ref[:, 0:D_MODEL // 2])
        xb = jnp.concatenate([hi.astype(BF16), lo.astype(BF16)], axis=1)
        w = lax.bitcast_convert_type(xs_ref[:, D_MODEL // 2:D_MODEL // 2 + 1], F32)
        g = jnp.dot(xb, wgb[...], preferred_element_type=F32)
        u = jnp.dot(xb, wub[...], preferred_element_type=F32)
        he = ((g * (1.0 + jnp.tanh(0.5 * g))) * (u * (0.5 * w))).astype(BF16)
        ys_ref[...] = _pack_rows(jnp.dot(he, wdb[...], preferred_element_type=F32))

    @pl.when(tv_ref[i] == 0)
    def _():
        ys_ref[...] = jnp.zeros_like(ys_ref)


def _ffn(xs, tile_e, tile_v, tile_src, wg, wu, wd, layer):
    n_rows = xs.shape[0]
    n_tiles = n_rows // TMF
    grid_spec = pltpu.PrefetchScalarGridSpec(
        num_scalar_prefetch=3,
        grid=(n_tiles,),
        in_specs=[
            pl.BlockSpec((TMF, XS_COLS), lambda i, te, tv, ts: (ts[i], 0)),
            pl.BlockSpec((1, 1, D_MODEL, D_EXPERT), lambda i, te, tv, ts: (layer, te[i], 0, 0)),
            pl.BlockSpec((1, 1, D_MODEL, D_EXPERT), lambda i, te, tv, ts: (layer, te[i], 0, 0)),
            pl.BlockSpec((1, 1, D_EXPERT, D_MODEL), lambda i, te, tv, ts: (layer, te[i], 0, 0)),
        ],
        out_specs=pl.BlockSpec((TMF, D_MODEL // 2), lambda i, te, tv, ts: (i, 0)),
        scratch_shapes=[
            pltpu.VMEM((D_MODEL, D_EXPERT), BF16),
            pltpu.VMEM((D_MODEL, D_EXPERT), BF16),
            pltpu.VMEM((D_EXPERT, D_MODEL), BF16),
        ],
    )
    return pl.pallas_call(
        _ffn_kernel,
        grid_spec=grid_spec,
        out_shape=jax.ShapeDtypeStruct((n_rows, D_MODEL // 2), U32),
        compiler_params=_cparams(("arbitrary",)),
    )(tile_e, tile_v, tile_src, xs, wg, wu, wd)


def _combine_kernel(start_s, len_s, dst_s, x_ref, ri_ref, g2_ref, fg_ref, ys_ref, o_ref, yb, sems, *, final):
    nt = pl.num_programs(1)
    i = pl.program_id(0) * nt + pl.program_id(1)
    last = pl.num_programs(0) * nt - 1
    slot = lax.rem(i, 2)

    def copy_for(slot_k):
        def make_copy(local_row, global_row, size):
            return pltpu.make_async_copy(ys_ref.at[pl.ds(global_row, size)],
                                         yb.at[slot_k, pl.ds(local_row, size)], sems.at[slot_k])
        return make_copy

    @pl.when(i == 0)
    def _():
        yb[...] = jnp.zeros_like(yb)
        _segment_copies(i, start_s, len_s, dst_s, copy_for(0), wait=False)

    for k in range(2):
        @pl.when(slot == k)
        def _(k=k):
            @pl.when(i < last)
            def _():
                _segment_copies(i + 1, start_s, len_s, dst_s, copy_for(1 - k), wait=False)

            j = i * N_EXPERTS + N_EXPERTS - 1
            _wait_rows(start_s[j] + len_s[j], lambda size: copy_for(k)(0, 0, size))

    hi, lo = _unpack_rows(yb[slot])
    m1, m2 = _sort_matrix(ri_ref)
    perm = jnp.where(m1 | m2, 1.0, 0.0).astype(BF16)
    tn_dims = (((0,), (0,)), ((), ()))
    moe = jnp.concatenate(
        [lax.dot_general(perm, hi.astype(BF16), tn_dims, preferred_element_type=F32),
         lax.dot_general(perm, lo.astype(BF16), tn_dims, preferred_element_type=F32)], axis=1)
    x = x_ref[0] + g2_ref[0] * moe
    if final:
        ms = jnp.mean(x * x, axis=-1, keepdims=True)
        x = x * lax.rsqrt(ms + EPS) * fg_ref[...]
    o_ref[0] = x


def _combine(xa, ys, ri, seg_start, seg_len, seg_dst, mods, final_g, t0, final):
    b, ntok, _ = xa.shape
    nt = ntok // TM - t0
    mod_row = lambda bi, t: jnp.where(t + t0 == 0, b, bi)
    if final:
        out_shape = jax.ShapeDtypeStruct((b, nt * TM, D_MODEL), F32)
        out_spec = pl.BlockSpec((1, TM, D_MODEL), lambda bi, t, *_: (bi, t, 0))
        aliases = {}
    else:
        out_shape = jax.ShapeDtypeStruct(xa.shape, F32)
        out_spec = pl.BlockSpec((1, TM, D_MODEL), lambda bi, t, *_: (bi, t + t0, 0))
        aliases = {3: 0}
    grid_spec = pltpu.PrefetchScalarGridSpec(
        num_scalar_prefetch=3,
        grid=(b, nt),
        in_specs=[
            pl.BlockSpec((1, TM, D_MODEL), lambda bi, t, *_: (bi, t + t0, 0)),
            pl.BlockSpec((1, 4, TM), lambda bi, t, *_: (bi * nt + t, 0, 0)),
            pl.BlockSpec((1, 1, D_MODEL), lambda bi, t, *_: (mod_row(bi, t), 0, 5)),
            pl.BlockSpec((1, D_MODEL), lambda bi, t, *_: (0, 0)),
            pl.BlockSpec(memory_space=pl.ANY),
        ],
        out_specs=out_spec,
        scratch_shapes=[pltpu.VMEM((2, SORT_ROWS, D_MODEL // 2), U32), pltpu.SemaphoreType.DMA((2,))],
    )
    return pl.pallas_call(
        functools.partial(_combine_kernel, final=final),
        grid_spec=grid_spec,
        out_shape=out_shape,
        input_output_aliases=aliases,
        compiler_params=_cparams(("arbitrary", "arbitrary")),
    )(seg_start, seg_len, seg_dst, xa, ri, mods, final_g.reshape(1, -1), ys)


def _route_tables(meta, cnt):
    ntiles = meta.shape[0]
    m = meta[:, :, 0].reshape(ntiles, 3, N_EXPERTS)
    seg_start, seg_len, before = m[:, 0], m[:, 1], m[:, 2]
    totals = cnt[:, 0].astype(I32)
    region = ((totals + TMF - 1) // TMF) * TMF
    ends = jnp.cumsum(region)
    seg_dst = (ends - region)[None, :] + before
    seg_start = jnp.concatenate([seg_start, jnp.zeros((1, N_EXPERTS), I32)], axis=0)
    seg_len = jnp.concatenate([seg_len, (region - totals)[None, :]], axis=0)
    seg_dst = jnp.concatenate([seg_dst, (ends - region + totals)[None, :]], axis=0)
    max_rows = 2 * ntiles * TM + ntiles * N_EXPERTS * (SEG_ALIGN - 1)
    n_tiles = max_rows // TMF + 1 + N_EXPERTS
    tile_id = jnp.arange(n_tiles, dtype=I32)
    starts = tile_id * TMF
    tile_e = jnp.minimum(jnp.sum(starts[:, None] >= ends[None, :], axis=1), N_EXPERTS - 1).astype(I32)
    valid = starts < ends[-1]
    last_used = jnp.maximum(ends[-1] // TMF - 1, 0)
    tile_e = jnp.where(valid, tile_e, tile_e[last_used])
    tile_src = jnp.minimum(tile_id, last_used)
    flat = lambda v: v.reshape(-1).astype(I32)
    seg_dst = jnp.concatenate([flat(seg_dst), (ends[-1:] // TMF).astype(I32)])
    return flat(seg_start), flat(seg_len), seg_dst, tile_e, valid.astype(I32), tile_src, n_tiles * TMF


def _rope_tables(n_lat):
    quarter = HEAD_DIM // 4
    inv = ROPE_BASE ** (-jnp.arange(quarter, dtype=F32) / quarter)
    n_rows = n_lat // GRID_W
    rows = jnp.repeat(jnp.arange(n_rows, dtype=F32), GRID_W)
    cols = jnp.tile(jnp.arange(GRID_W, dtype=F32), n_rows)
    ang = jnp.concatenate([rows[:, None] * inv, cols[:, None] * inv], axis=-1)
    cos = jnp.cos(ang)
    sin = jnp.sin(ang)
    cos_t = jnp.tile(cos, (1, LANES // cos.shape[1]))
    sin_t = jnp.tile(jnp.concatenate([-sin, sin], axis=-1), (1, LANES // HEAD_DIM))
    cos_t = jnp.concatenate([jnp.ones((CTX_LEN, LANES), F32), cos_t], axis=0)
    sin_t = jnp.concatenate([jnp.zeros((CTX_LEN, LANES), F32), sin_t], axis=0)
    return cos_t, sin_t


def kernel(x, c, ctx, c_ctx, w_ada, b_ada, norm_mix_g, norm_ffn_g, w_in, diff_lambda, diff_subln_g, swa_sink,
           ret_decay, ret_gn_g, w_branch, w_out, router_w, router_b, w_exp_gate, w_exp_up, w_exp_down, final_g):
    b, n_lat, _ = x.shape
    depth = w_ada.shape[0]
    assert ctx.shape[1] == CTX_LEN == TM and n_lat % TM == 0 and n_lat >= 2 * TM

    pad_rows = (-(b + 1)) % 8
    c_all = jnp.concatenate([c, c_ctx[None, :], jnp.zeros((pad_rows, D_MODEL), F32)], axis=0)
    mods_all = _ada(c_all, w_ada, b_ada)
    cos_t, sin_t = _rope_tables(n_lat)
    xa = x
    rw_t = router_w.T.astype(BF16)
    rb_col = router_b.reshape(-1, 1).astype(F32)

    for l in range(depth):
        last = l == depth - 1
        t0 = 1 if last else 0
        lam_init = 0.8 - 0.6 * math.exp(-0.3 * l)
        mods = mods_all[l].reshape(mods_all.shape[1], 1, -1)
        ctx_in = ctx if l == 0 else None
        p = _proj(ctx_in, xa, mods, norm_mix_g[l], cos_t, sin_t, w_in[l].astype(BF16), ctx_kv_only=last)
        oa = _diff(p, diff_lambda[l], diff_subln_g[l], lam_init, t0)
        ob = _win(p, swa_sink[l], t0)
        oc = _ret(p, ret_decay[l])
        xa, hp, ri, rwt, meta, cnt = _merge(ctx_in, xa, p, oa, ob, oc, mods, ret_gn_g[l], norm_ffn_g[l],
                                            w_branch[l].astype(BF16), w_out[l].astype(BF16), rw_t, rb_col, t0)
        seg_start, seg_len, seg_dst, tile_e, tile_v, tile_src, n_rows = _route_tables(meta, cnt)
        xs = _dispatch(hp, ri, rwt, seg_start, seg_len, seg_dst, n_rows)
        ys = _ffn(xs, tile_e, tile_v, tile_src, w_exp_gate, w_exp_up, w_exp_down, l)
        xa = _combine(xa, ys, ri, seg_start, seg_len, seg_dst, mods, final_g, t0, last)
    return xa
```

```python
import functools
import math

import jax
import jax.numpy as jnp
from jax import lax
from jax.experimental import pallas as pl
from jax.experimental.pallas import tpu as pltpu

F32 = jnp.float32
BF16 = jnp.bfloat16
U32 = jnp.uint32
I32 = jnp.int32

D_MODEL = 1024
CTX_LEN = 256
GRID_W = 64
ROPE_BASE = 10000.0
EPS = 1e-6
LOG2E = 1.4426950408889634
HEAD_DIM = 64
WINDOW = 128
RT_CHUNK = 128
N_EXPERTS = 16
N_GROUPS = 4
D_EXPERT = 512
LANES = 128
TM = 256
DIFF_HEADS = 4
QK_AHEAD = 2
TMF = 512
SEG_ALIGN = 8
SEG_SIZES = (256, 128, 64, 32, 16, 8)
SEG_BIG = 64
SORT_ROWS = 640
XS_COLS = D_MODEL // 2 + 128
D_IN = 6912
N_SRC_BLK = D_IN // LANES
N_OUT_BLK = N_SRC_BLK + 2
VMEM_LIMIT = 56 * 1024 * 1024

BLK_AQ, BLK_AK, BLK_AV = 0, 4, 8
BLK_BQ, BLK_BK, BLK_BV = 12, 16, 18
BLK_CQ, BLK_CK, BLK_CV, BLK_CG, BLK_GATE = 20, 22, 24, 28, 32


def _cparams(sem):
    return pltpu.CompilerParams(dimension_semantics=sem, vmem_limit_bytes=VMEM_LIMIT)


def _ada_kernel(c_ref, w_ref, b_ref, o_ref):
    c = c_ref[...]
    s = (c * jax.nn.sigmoid(c)).astype(BF16)
    o_ref[0] = jnp.dot(s, w_ref[0].astype(BF16), preferred_element_type=F32) + b_ref[0]


def _ada(c_all, w_ada, b_ada):
    depth = w_ada.shape[0]
    rows = c_all.shape[0]
    nj = w_ada.shape[2] // D_MODEL
    return pl.pallas_call(
        _ada_kernel,
        grid=(depth, nj),
        in_specs=[
            pl.BlockSpec((rows, D_MODEL), lambda l, j: (0, 0)),
            pl.BlockSpec((1, D_MODEL, D_MODEL), lambda l, j: (l, 0, j)),
            pl.BlockSpec((1, 1, D_MODEL), lambda l, j: (l, 0, j)),
        ],
        out_specs=pl.BlockSpec((1, rows, D_MODEL), lambda l, j: (l, 0, j)),
        out_shape=jax.ShapeDtypeStruct((depth, rows, w_ada.shape[2]), F32),
        compiler_params=_cparams(("arbitrary", "arbitrary")),
    )(c_all, w_ada, b_ada.reshape(depth, 1, -1))


def _rope(a, cos, sin_signed, first_half):
    rot = jnp.where(first_half, pltpu.roll(a, LANES - HEAD_DIM // 2, 1), pltpu.roll(a, HEAD_DIM // 2, 1))
    return a * cos + rot * sin_signed


def _token_tile(ctx_ref, x_ref, tile):
    if ctx_ref is None:
        return x_ref[0]
    return jnp.where(tile == 0, ctx_ref[0], x_ref[0])


def _token_specs(ctx, xa, t0):
    if ctx is None:
        return [xa], [pl.BlockSpec((1, TM, D_MODEL), lambda bi, t, *_: (bi, t + t0, 0))]
    return [ctx, xa], [pl.BlockSpec((1, TM, D_MODEL), lambda bi, t, *_: (bi, 0, 0)),
                       pl.BlockSpec((1, TM, D_MODEL), lambda bi, t, *_: (bi, jnp.maximum(t + t0 - 1, 0), 0))]


CTX_KV_CHUNKS = (2, 3, 4, 5, 8, 10, 11, 12)


def _proj_kernel(*refs, split, ctx_kv_only):
    ctx_ref, refs = (refs[0], refs[1:]) if split else (None, refs)
    x_ref, sh_ref, sc_ref, g_ref, cos_ref, sin_ref, w_ref, o_ref = refs
    t = pl.program_id(1)
    x = _token_tile(ctx_ref, x_ref, t)
    ms = jnp.mean(x * x, axis=-1, keepdims=True)
    h = x * lax.rsqrt(ms + EPS) * g_ref[...]
    h = h * (1.0 + sc_ref[0]) + sh_ref[0]
    hb = h.astype(BF16)
    cos = cos_ref[...]
    sin = sin_ref[...]
    lane = lax.broadcasted_iota(I32, (1, LANES), 1)
    first_half = (lane & (HEAD_DIM - 1)) < (HEAD_DIM // 2)
    low = lane < HEAD_DIM
    scale = HEAD_DIM ** -0.5

    def project(chunks):
        for c in range(N_SRC_BLK // 2):
            if c not in chunks:
                for src in (2 * c, 2 * c + 1):
                    dst = src if src < 16 else src + 2
                    o_ref[0, :, dst * LANES:(dst + 1) * LANES] = jnp.zeros((TM, LANES), BF16)
                continue
            acc = jnp.dot(hb, w_ref[:, c * 2 * LANES:(c + 1) * 2 * LANES], preferred_element_type=F32)
            for half in range(2):
                src = 2 * c + half
                a = acc[:, half * LANES:(half + 1) * LANES]
                if src < 8 or 12 <= src < 17:
                    a = _rope(a, cos, sin, first_half)
                if src < 4 or 12 <= src < 16:
                    a = a * (scale * LOG2E)
                if 20 <= src < 22:
                    a = a * scale
                if src >= 26:
                    a = a * 0.5
                if src in (16, 17):
                    sw = pltpu.roll(a, HEAD_DIM, 1)
                    dst = BLK_BK if src == 16 else BLK_BV
                    o_ref[0, :, dst * LANES:(dst + 1) * LANES] = jnp.where(low, a, sw).astype(BF16)
                    o_ref[0, :, (dst + 1) * LANES:(dst + 2) * LANES] = jnp.where(low, sw, a).astype(BF16)
                else:
                    dst = src if src < 16 else src + 2
                    o_ref[0, :, dst * LANES:(dst + 1) * LANES] = a.astype(BF16)

    all_chunks = tuple(range(N_SRC_BLK // 2))
    if ctx_kv_only:
        @pl.when(t == 0)
        def _():
            project(CTX_KV_CHUNKS)

        @pl.when(t > 0)
        def _():
            project(all_chunks)
    else:
        project(all_chunks)


def _proj(ctx, xa, mods, norm_g, cos_t, sin_t, w_in_bf, ctx_kv_only):
    b = xa.shape[0]
    ntok = xa.shape[1] + (0 if ctx is None else CTX_LEN)
    nt = ntok // TM
    mod_row = lambda bi, t: jnp.where(t == 0, b, bi)
    tok_args, tok_specs = _token_specs(ctx, xa, 0)
    return pl.pallas_call(
        functools.partial(_proj_kernel, split=ctx is not None, ctx_kv_only=ctx_kv_only),
        grid=(b, nt),
        in_specs=tok_specs + [
            pl.BlockSpec((1, 1, D_MODEL), lambda bi, t: (mod_row(bi, t), 0, 0)),
            pl.BlockSpec((1, 1, D_MODEL), lambda bi, t: (mod_row(bi, t), 0, 1)),
            pl.BlockSpec((1, D_MODEL), lambda bi, t: (0, 0)),
            pl.BlockSpec((TM, LANES), lambda bi, t: (t, 0)),
            pl.BlockSpec((TM, LANES), lambda bi, t: (t, 0)),
            pl.BlockSpec((D_MODEL, D_IN), lambda bi, t: (0, 0)),
        ],
        out_specs=pl.BlockSpec((1, TM, N_OUT_BLK * LANES), lambda bi, t: (bi, t, 0)),
        out_shape=jax.ShapeDtypeStruct((b, ntok, N_OUT_BLK * LANES), BF16),
        compiler_params=_cparams(("arbitrary", "arbitrary")),
    )(*tok_args, mods, mods, norm_g.reshape(1, -1), cos_t, sin_t, w_in_bf)


def _diff_kernel(lam_ref, g_ref, q_ref, k_ref, v_ref, o_ref, *, lam_init, t0, ntok):
    t = pl.program_id(2) + t0
    lv = lam_ref[...]
    lam = (jnp.exp(jnp.sum(lv[0:1] * lv[1:2], axis=-1, keepdims=True))
           - jnp.exp(jnp.sum(lv[2:3] * lv[3:4], axis=-1, keepdims=True)) + lam_init)
    lane = lax.broadcasted_iota(I32, (1, LANES), 1)
    nt_dims = (((1,), (1,)), ((), ()))

    def attend(nk):
        def scores(hd):
            cols = slice(hd * LANES, (hd + 1) * LANES)
            q = q_ref[0, :, cols]
            zero = jnp.zeros_like(q)
            q0 = jnp.where(lane < HEAD_DIM, q, zero)
            q1 = jnp.where(lane >= HEAD_DIM, q, zero)
            k = k_ref[0, 0:nk, cols]
            return (lax.dot_general(q0, k, nt_dims, preferred_element_type=F32),
                    lax.dot_general(q1, k, nt_dims, preferred_element_type=F32))

        ones_blk = jnp.where(lax.broadcasted_iota(I32, (nk, LANES), 1) == 0, 1.0, 0.0).astype(BF16)
        queue = [scores(hd) for hd in range(min(QK_AHEAD, DIFF_HEADS))]
        for hd in range(DIFF_HEADS):
            cols = slice(hd * LANES, (hd + 1) * LANES)
            v1 = jnp.concatenate([v_ref[0, 0:nk, cols], ones_blk], axis=1)
            s0, s1 = queue.pop(0)
            if hd + QK_AHEAD < DIFF_HEADS:
                queue.append(scores(hd + QK_AHEAD))
            e0 = jnp.exp2(s0 - jnp.max(s0, axis=-1, keepdims=True)).astype(BF16)
            e1 = jnp.exp2(s1 - jnp.max(s1, axis=-1, keepdims=True)).astype(BF16)
            ov0 = jnp.dot(e0, v1, preferred_element_type=F32)
            ov1 = jnp.dot(e1, v1, preferred_element_type=F32)
            r0 = 1.0 / ov0[:, LANES:LANES + 1]
            r1 = lam / ov1[:, LANES:LANES + 1]
            o = ov0[:, 0:LANES] * r0 - ov1[:, 0:LANES] * r1
            o = o * lax.rsqrt(jnp.mean(o * o, axis=-1, keepdims=True) + EPS) * g_ref[...] * (1.0 - lam_init)
            o_ref[0, :, cols] = o.astype(BF16)

    if t0 == 0:
        @pl.when(t == 0)
        def _():
            attend(CTX_LEN)

        @pl.when(t > 0)
        def _():
            attend(ntok)
    else:
        attend(ntok)


def _diff(p, lam_vec, subln_g, lam_init, t0):
    b, ntok, _ = p.shape
    nt = ntok // TM - t0
    heads = 4
    hw = DIFF_HEADS * LANES
    return pl.pallas_call(
        functools.partial(_diff_kernel, lam_init=lam_init, t0=t0, ntok=ntok),
        grid=(b, heads // DIFF_HEADS, nt),
        in_specs=[
            pl.BlockSpec((4, HEAD_DIM), lambda bi, h, t: (0, 0)),
            pl.BlockSpec((1, LANES), lambda bi, h, t: (0, 0)),
            pl.BlockSpec((1, TM, hw), lambda bi, h, t: (bi, t + t0, BLK_AQ // DIFF_HEADS + h)),
            pl.BlockSpec((1, ntok, hw), lambda bi, h, t: (bi, 0, BLK_AK // DIFF_HEADS + h)),
            pl.BlockSpec((1, ntok, hw), lambda bi, h, t: (bi, 0, BLK_AV // DIFF_HEADS + h)),
        ],
        out_specs=pl.BlockSpec((1, TM, hw), lambda bi, h, t: (bi, t, h)),
        out_shape=jax.ShapeDtypeStruct((b, nt * TM, heads * LANES), BF16),
        compiler_params=_cparams(("arbitrary", "arbitrary", "arbitrary")),
    )(lam_vec, subln_g.reshape(1, -1), p, p, p)


def _win_kernel(sink_ref, q_ref, k_ref, v_ref, o_ref, *, t0, ntok):
    t = pl.program_id(1) + t0
    lane = lax.broadcasted_iota(I32, (1, LANES), 1)
    low = lane < HEAD_DIM
    nt_dims = (((1,), (1,)), ((), ()))
    wk = 2 * TM
    start = jnp.clip(t * TM - WINDOW, CTX_LEN, ntok - wk)
    start = pl.multiple_of(start, WINDOW)
    qpos = t * TM + lax.broadcasted_iota(I32, (TM, wk), 0)
    kpos = start + lax.broadcasted_iota(I32, (TM, wk), 1)
    in_band = (jnp.abs(kpos - qpos) <= WINDOW) & (t > 0)
    bias = jnp.concatenate([jnp.zeros((TM, CTX_LEN), F32), jnp.where(in_band, 0.0, -jnp.inf).astype(F32)], axis=1)
    n_heads = 8
    group = n_heads // 2
    keys, vals = [], []
    ones_blk = jnp.where(lax.broadcasted_iota(I32, (CTX_LEN + wk, LANES), 1) == 0, 1.0, 0.0).astype(BF16)
    for g in range(2):
        cols = slice(g * LANES, (g + 1) * LANES)
        keys.append(jnp.concatenate([k_ref[0, 0:CTX_LEN, cols], k_ref[0, pl.ds(start, wk), cols]], axis=0))
        vg = jnp.concatenate([v_ref[0, 0:CTX_LEN, cols], v_ref[0, pl.ds(start, wk), cols]], axis=0)
        vals.append(jnp.concatenate([vg, ones_blk], axis=1))

    def scores(h):
        qb = q_ref[0, :, (h // 2) * LANES:(h // 2 + 1) * LANES]
        qm = jnp.where(low if h % 2 == 0 else jnp.logical_not(low), qb, jnp.zeros_like(qb))
        return lax.dot_general(qm, keys[h // group], nt_dims, preferred_element_type=F32)

    queue = [scores(h) for h in range(QK_AHEAD)]
    outs = []
    for h in range(n_heads):
        s = queue.pop(0) + bias
        if h + QK_AHEAD < n_heads:
            queue.append(scores(h + QK_AHEAD))
        sk = sink_ref[h] * LOG2E
        m = jnp.maximum(jnp.max(s, axis=-1, keepdims=True), sk)
        e = jnp.exp2(s - m).astype(BF16)
        ov = jnp.dot(e, vals[h // group], preferred_element_type=F32)
        l = ov[:, LANES:LANES + 1] + jnp.exp2(sk - m)
        outs.append(ov[:, 0:LANES] * (1.0 / l))
        if h % 2 == 1:
            o_ref[0, :, (h // 2) * LANES:(h // 2 + 1) * LANES] = jnp.where(low, outs[h - 1], outs[h]).astype(BF16)


def _win(p, sink, t0):
    b, ntok, _ = p.shape
    nt = ntok // TM - t0
    return pl.pallas_call(
        functools.partial(_win_kernel, t0=t0, ntok=ntok),
        grid=(b, nt),
        in_specs=[
            pl.BlockSpec(memory_space=pltpu.SMEM),
            pl.BlockSpec((1, TM, 4 * LANES), lambda bi, t: (bi, t + t0, BLK_BQ // 4)),
            pl.BlockSpec((1, ntok, 2 * LANES), lambda bi, t: (bi, 0, BLK_BK // 2)),
            pl.BlockSpec((1, ntok, 2 * LANES), lambda bi, t: (bi, 0, BLK_BV // 2)),
        ],
        out_specs=pl.BlockSpec((1, TM, 4 * LANES), lambda bi, t: (bi, t, 0)),
        out_shape=jax.ShapeDtypeStruct((b, nt * TM, 4 * LANES), BF16),
        compiler_params=_cparams(("arbitrary", "arbitrary")),
    )(sink, p, p, p)


def _ret_kernel(dec_ref, q_ref, k_ref, v_ref, o_ref, ob_ref, *, ntok):
    T = RT_CHUNK
    nchunk = ntok // T
    nctx = CTX_LEN // T
    x = dec_ref[...]
    lg_all = -(jnp.maximum(-x, 0.0) + jnp.log1p(jnp.exp(-jnp.abs(x))))
    lane = lax.broadcasted_iota(I32, (1, LANES), 1)
    low = lane < HEAD_DIM
    row_low = lax.broadcasted_iota(I32, (LANES, 1), 0) < HEAD_DIM
    pi = lax.broadcasted_iota(I32, (T, T), 0).astype(F32)
    pj = lax.broadcasted_iota(I32, (T, T), 1).astype(F32)
    pcol = lax.broadcasted_iota(I32, (T, 1), 0).astype(F32)
    nt_dims = (((1,), (1,)), ((), ()))
    tn_dims = (((0,), (0,)), ((), ()))

    decay, qdp, kdp, cdp = [], [], [], []
    for direction in range(2):
        lgs = [lg_all[direction:direction + 1, h:h + 1] for h in range(4)]
        if direction == 0:
            dist = pi - pj
            qd = [jnp.exp((pcol + 1.0) * lg) for lg in lgs]
            kd = [jnp.exp((T - 1.0 - pcol) * lg) for lg in lgs]
        else:
            dist = pj - pi
            qd = [jnp.exp((T - pcol) * lg) for lg in lgs]
            kd = [jnp.exp(pcol * lg) for lg in lgs]
        decay.append([jnp.where(dist >= 0, jnp.exp(jnp.maximum(dist, 0.0) * lg), 0.0) for lg in lgs])
        qdp.append([jnp.where(low, qd[2 * pr], qd[2 * pr + 1]) for pr in range(2)])
        kdp.append([jnp.where(low, kd[2 * pr], kd[2 * pr + 1]) for pr in range(2)])
        cdp.append([jnp.where(row_low, jnp.exp(T * lgs[2 * pr]), jnp.exp(T * lgs[2 * pr + 1])) for pr in range(2)])

    def step(i, states):
        chunks = (i, jnp.where(i < nctx, nctx - 1 - i, nchunk + nctx - 1 - i))
        halves = (low, jnp.logical_not(low))
        work = []
        for direction in range(2):
            r0 = pl.multiple_of(chunks[direction] * T, T)
            for pair in range(2):
                qb = q_ref[0, pl.ds(r0, T), pair * LANES:(pair + 1) * LANES]
                kb = k_ref[0, pl.ds(r0, T), pair * LANES:(pair + 1) * LANES]
                v2 = v_ref[0, pl.ds(r0, T), pair * 2 * LANES:(pair + 1) * 2 * LANES]
                kbd = (kb.astype(F32) * kdp[direction][pair]).astype(BF16)
                qsb = (qb.astype(F32) * qdp[direction][pair]).astype(BF16)
                zq = jnp.zeros_like(qb)
                q2 = jnp.concatenate([jnp.where(hm, qb, zq) for hm in halves], axis=0)
                inner2 = lax.dot_general(q2, kb, nt_dims, preferred_element_type=F32)
                kv2 = lax.dot_general(kbd, v2, tn_dims, preferred_element_type=F32)
                work.append((direction, pair, r0, inner2, kv2, qsb, v2))
        new_states = []
        for direction, pair, r0, inner2, kv2, qsb, v2 in work:
            out_ref = o_ref.at[0] if direction == 0 else ob_ref
            st = states[2 * direction + pair]
            stb = st.astype(BF16)
            zq = jnp.zeros_like(qsb)
            for sub in range(2):
                h = 2 * pair + sub
                inner = inner2[sub * T:(sub + 1) * T] * decay[direction][h]
                lhs = jnp.concatenate([inner.astype(BF16), jnp.where(halves[sub], qsb, zq)], axis=1)
                rhs = jnp.concatenate([v2[:, sub * LANES:(sub + 1) * LANES], stb], axis=0)
                out_ref[pl.ds(r0, T), h * LANES:(h + 1) * LANES] = jnp.dot(lhs, rhs, preferred_element_type=F32)
            new_states.append(st * cdp[direction][pair]
                              + jnp.where(row_low, kv2[:, 0:LANES], kv2[:, LANES:2 * LANES]))
        return tuple(new_states)

    zero = jnp.zeros((LANES, LANES), F32)
    lax.fori_loop(0, nchunk, step, (zero,) * 4, unroll=3)
    o_ref[0] = o_ref[0] + ob_ref[...]


def _ret(p, ret_decay):
    b, ntok, _ = p.shape
    return pl.pallas_call(
        functools.partial(_ret_kernel, ntok=ntok),
        grid=(b,),
        in_specs=[
            pl.BlockSpec((2, 4), lambda bi: (0, 0)),
            pl.BlockSpec((1, ntok, 2 * LANES), lambda bi: (bi, 0, BLK_CQ // 2)),
            pl.BlockSpec((1, ntok, 2 * LANES), lambda bi: (bi, 0, BLK_CK // 2)),
            pl.BlockSpec((1, ntok, 4 * LANES), lambda bi: (bi, 0, BLK_CV // 4)),
        ],
        out_specs=pl.BlockSpec((1, ntok, 4 * LANES), lambda bi: (bi, 0, 0)),
        out_shape=jax.ShapeDtypeStruct((b, ntok, 4 * LANES), F32),
        scratch_shapes=[pltpu.VMEM((ntok, 4 * LANES), F32)],
        compiler_params=_cparams(("arbitrary",)),
    )(ret_decay, p, p, p)


def _pack_rows(h, is_bf16_valued=False):
    half = h.shape[1] // 2
    rnd = (lambda v: v) if is_bf16_valued else (lambda v: v.astype(BF16).astype(F32))
    hi = lax.bitcast_convert_type(rnd(h[:, :half]), U32)
    lo = lax.bitcast_convert_type(rnd(h[:, half:]), U32)
    return hi | (lo >> 16)


def _unpack_rows(p):
    hi = lax.bitcast_convert_type(p & jnp.uint32(0xFFFF0000), F32)
    lo = lax.bitcast_convert_type(p << 16, F32)
    return hi, lo


def _merge_kernel(*refs, split, t0):
    ctx_ref, refs = (refs[0], refs[1:]) if split else (None, refs)
    (x_ref, oa_ref, ob_ref, oc_ref, cg_ref, ga_ref, gb_ref, gc_ref, g1_ref, sh2_ref, sc2_ref,
     gn_ref, nf_ref, wb_ref, wo_ref, rw_ref, rb_ref,
     xo_ref, hp_ref, ri_ref, rwt_ref, meta_ref, cnt_ref) = refs
    first = (pl.program_id(0) == 0) & (pl.program_id(1) == 0)

    @pl.when(first)
    def _():
        cnt_ref[...] = jnp.zeros_like(cnt_ref)

    da = jnp.dot(oa_ref[0], wb_ref[0], preferred_element_type=F32)
    db = jnp.dot(ob_ref[0], wb_ref[1], preferred_element_type=F32)

    oc = oc_ref[0]
    cgv = cg_ref[0].astype(F32)
    gn = gn_ref[...]
    parts = []
    for h in range(4):
        o = oc[:, h * LANES:(h + 1) * LANES]
        mu = jnp.mean(o, axis=-1, keepdims=True)
        dlt = o - mu
        var = jnp.mean(dlt * dlt, axis=-1, keepdims=True)
        y = dlt * lax.rsqrt(var + EPS) * gn[:, h * LANES:(h + 1) * LANES]
        cgh = cgv[:, h * LANES:(h + 1) * LANES]
        parts.append((y * (cgh * (1.0 + jnp.tanh(cgh)))).astype(BF16))
    ocn = jnp.concatenate(parts, axis=1)

    dc = jnp.dot(ocn, wb_ref[2], preferred_element_type=F32)
    y = (1.0 + jnp.tanh(ga_ref[0].astype(F32))) * da
    y += (1.0 + jnp.tanh(gb_ref[0].astype(F32))) * db
    y += (1.0 + jnp.tanh(gc_ref[0].astype(F32))) * dc
    y2 = jnp.dot(y.astype(BF16), wo_ref[...], preferred_element_type=F32)
    x = _token_tile(ctx_ref, x_ref, pl.program_id(1) + t0) + (0.5 * g1_ref[0]) * y2
    xo_ref[0] = x

    ms = jnp.mean(x * x, axis=-1, keepdims=True)
    h2 = x * lax.rsqrt(ms + EPS) * nf_ref[...]
    h2 = h2 * (1.0 + sc2_ref[0]) + sh2_ref[0]
    h2b = h2.astype(BF16)
    hp_ref[...] = h2b

    nt_dims = (((1,), (1,)), ((), ()))
    logits = lax.dot_general(rw_ref[...], h2b, nt_dims, preferred_element_type=F32)
    sc = jax.nn.sigmoid(logits)
    bi = sc + rb_ref[...]
    ei_i = lax.broadcasted_iota(I32, (N_EXPERTS, 1), 0)
    ei = ei_i.astype(F32)
    epg = N_EXPERTS // N_GROUPS
    egroup = (ei_i >> 2).astype(F32)
    gsum = []
    for g in range(N_GROUPS):
        r = [bi[epg * g + j:epg * g + j + 1, :] for j in range(epg)]
        m = r[0] + r[1]
        for a in range(epg):
            for c in range(a + 1, epg):
                if (a, c) != (0, 1):
                    m = jnp.maximum(m, r[a] + r[c])
        gsum.append(m)
    best = jnp.zeros_like(gsum[0])
    bsc = gsum[0]
    for g in range(1, N_GROUPS):
        upd = gsum[g] > bsc
        best = jnp.where(upd, float(g), best)
        bsc = jnp.where(upd, gsum[g], bsc)
    neg = -jnp.inf
    masked = jnp.where(egroup == best, bi, neg)
    m1 = jnp.max(masked, axis=0, keepdims=True)
    i1 = jnp.min(jnp.where(masked == m1, ei, float(N_EXPERTS)), axis=0, keepdims=True)
    masked2 = jnp.where(ei == i1, neg, masked)
    m2 = jnp.max(masked2, axis=0, keepdims=True)
    i2 = jnp.min(jnp.where(masked2 == m2, ei, float(N_EXPERTS)), axis=0, keepdims=True)
    sel1 = ei == i1
    sel2 = ei == i2
    w1 = jnp.sum(jnp.where(sel1, sc, 0.0), axis=0, keepdims=True)
    w2 = jnp.sum(jnp.where(sel2, sc, 0.0), axis=0, keepdims=True)
    ws = w1 + w2
    rwt_ref[0, 0:1, :] = w1 / ws
    rwt_ref[0, 1:2, :] = w2 / ws

    oh = (sel1 | sel2).astype(F32)
    si = lax.broadcasted_iota(I32, (TM, TM), 0)
    ti = lax.broadcasted_iota(I32, (TM, TM), 1)
    upper = (si < ti).astype(BF16)
    prefix = jnp.dot(oh.astype(BF16), upper, preferred_element_type=F32)
    n_col = jnp.sum(oh, axis=1, keepdims=True)
    seg_len = jnp.floor((n_col + (SEG_ALIGN - 1.0)) * (1.0 / SEG_ALIGN)) * SEG_ALIGN
    seg_len_b = jnp.broadcast_to(seg_len, (N_EXPERTS, LANES))
    er = lax.broadcasted_iota(I32, (N_EXPERTS, N_EXPERTS), 0)
    ec = lax.broadcasted_iota(I32, (N_EXPERTS, N_EXPERTS), 1)
    lower = (ec < er).astype(BF16)
    seg_start_b = jnp.dot(lower, seg_len_b.astype(BF16), preferred_element_type=F32)
    local = prefix + seg_start_b[:, 0:1]
    pos1 = jnp.sum(jnp.where(sel1, local, 0.0), axis=0, keepdims=True)
    pos2 = jnp.sum(jnp.where(sel2, local, 0.0), axis=0, keepdims=True)
    ri_ref[0, 0:1, :] = i1.astype(I32)
    ri_ref[0, 1:2, :] = i2.astype(I32)
    ri_ref[0, 2:3, :] = pos1.astype(I32)
    ri_ref[0, 3:4, :] = pos2.astype(I32)
    meta_ref[0, 0:N_EXPERTS, :] = seg_start_b.astype(I32)
    meta_ref[0, N_EXPERTS:2 * N_EXPERTS, :] = seg_len_b.astype(I32)
    meta_ref[0, 2 * N_EXPERTS:, :] = cnt_ref[...].astype(I32)
    cnt_ref[...] = cnt_ref[...] + seg_len_b


def _merge(ctx, xa, p, oa, ob, oc, mods, gn_g, nf_g, wb_bf, wo_bf, rw_t, rb_col, t0):
    b, ntok, _ = p.shape
    nt = ntok // TM - t0
    ntiles = b * nt
    mod_row = lambda bi, t: jnp.where(t + t0 == 0, b, bi)
    tok = lambda bi, t: (bi, t + t0, 0)
    flat = lambda bi, t: (bi * nt + t, 0, 0)
    tok_args, tok_specs = _token_specs(ctx, xa, t0)
    outs = pl.pallas_call(
        functools.partial(_merge_kernel, split=ctx is not None, t0=t0),
        grid=(b, nt),
        in_specs=tok_specs + [
            pl.BlockSpec((1, TM, 4 * LANES), lambda bi, t: (bi, t, 0)),
            pl.BlockSpec((1, TM, 4 * LANES), lambda bi, t: (bi, t, 0)),
            pl.BlockSpec((1, TM, 4 * LANES), tok),
            pl.BlockSpec((1, TM, 4 * LANES), lambda bi, t: (bi, t + t0, BLK_CG // 4)),
            pl.BlockSpec((1, TM, D_MODEL), lambda bi, t: (bi, t + t0, BLK_GATE // 8)),
            pl.BlockSpec((1, TM, D_MODEL), lambda bi, t: (bi, t + t0, BLK_GATE // 8 + 1)),
            pl.BlockSpec((1, TM, D_MODEL), lambda bi, t: (bi, t + t0, BLK_GATE // 8 + 2)),
            pl.BlockSpec((1, 1, D_MODEL), lambda bi, t: (mod_row(bi, t), 0, 2)),
            pl.BlockSpec((1, 1, D_MODEL), lambda bi, t: (mod_row(bi, t), 0, 3)),
            pl.BlockSpec((1, 1, D_MODEL), lambda bi, t: (mod_row(bi, t), 0, 4)),
            pl.BlockSpec((1, 4 * LANES), lambda bi, t: (0, 0)),
            pl.BlockSpec((1, D_MODEL), lambda bi, t: (0, 0)),
            pl.BlockSpec((3, 4 * LANES, D_MODEL), lambda bi, t: (0, 0, 0)),
            pl.BlockSpec((D_MODEL, D_MODEL), lambda bi, t: (0, 0)),
            pl.BlockSpec((N_EXPERTS, D_MODEL), lambda bi, t: (0, 0)),
            pl.BlockSpec((N_EXPERTS, 1), lambda bi, t: (0, 0)),
        ],
        out_specs=[
            pl.BlockSpec((1, TM, D_MODEL), tok),
            pl.BlockSpec((TM, D_MODEL), lambda bi, t: (bi * nt + t, 0)),
            pl.BlockSpec((1, 4, TM), flat),
            pl.BlockSpec((1, 2, TM), flat),
            pl.BlockSpec((1, 3 * N_EXPERTS, LANES), flat),
            pl.BlockSpec((N_EXPERTS, LANES), lambda bi, t: (0, 0)),
        ],
        out_shape=[
            jax.ShapeDtypeStruct((b, ntok, D_MODEL), F32),
            jax.ShapeDtypeStruct((ntiles * TM, D_MODEL), BF16),
            jax.ShapeDtypeStruct((ntiles, 4, TM), I32),
            jax.ShapeDtypeStruct((ntiles, 2, TM), F32),
            jax.ShapeDtypeStruct((ntiles, 3 * N_EXPERTS, LANES), I32),
            jax.ShapeDtypeStruct((N_EXPERTS, LANES), F32),
        ],
        input_output_aliases={} if ctx is not None else {0: 0},
        compiler_params=_cparams(("arbitrary", "arbitrary")),
    )(*tok_args, oa, ob, oc, p, p, p, p, mods, mods, mods, gn_g.reshape(1, -1), nf_g.reshape(1, -1),
      wb_bf, wo_bf, rw_t, rb_col)
    return outs


def _segment_copies(tile, start_s, len_s, dst_s, make_copy, wait):
    for e in range(N_EXPERTS):
        n = len_s[tile * N_EXPERTS + e]
        a = start_s[tile * N_EXPERTS + e]
        d = dst_s[tile * N_EXPERTS + e]
        def chunks(sizes, n=n, a=a, d=d):
            for size in sizes:
                @pl.when((n & size) != 0)
                def _(size=size):
                    off = n & (-2 * size)
                    cp = make_copy(pl.multiple_of(a + off, SEG_ALIGN), pl.multiple_of(d + off, SEG_ALIGN), size)
                    if wait:
                        cp.wait()
                    else:
                        cp.start()

        big = tuple(s for s in SEG_SIZES if s >= SEG_BIG)
        pl.when(n >= SEG_BIG)(functools.partial(chunks, big))
        chunks(tuple(s for s in SEG_SIZES if s < SEG_BIG))


def _wait_rows(total, make_copy):
    for size in (2 * SEG_SIZES[0],) + SEG_SIZES:
        @pl.when((total & size) != 0)
        def _(size=size):
            make_copy(size).wait()


def _sort_matrix(ri_ref):
    srow = lax.broadcasted_iota(I32, (SORT_ROWS, TM), 0)
    return srow == ri_ref[0, 2:3, :], srow == ri_ref[0, 3:4, :]


def _dispatch_kernel(start_s, len_s, dst_s, h_ref, ri_ref, rw_ref, xs_ref, sb, zb, sems):
    i = pl.program_id(0)
    last = pl.num_programs(0) - 1
    slot = lax.rem(i, 2)

    hb = h_ref[...]
    m1, m2 = _sort_matrix(ri_ref)
    perm = jnp.where(m1 | m2, 1.0, 0.0).astype(BF16)
    rows = jnp.dot(perm, hb, preferred_element_type=F32)
    w = jnp.sum(jnp.where(m1, rw_ref[0, 0:1, :], 0.0) + jnp.where(m2, rw_ref[0, 1:2, :], 0.0),
                axis=1, keepdims=True)
    sb[slot, :, 0:D_MODEL // 2] = _pack_rows(rows, is_bf16_valued=True)
    sb[slot, :, D_MODEL // 2:] = lax.bitcast_convert_type(jnp.broadcast_to(w, (SORT_ROWS, LANES)), U32)

    def copy_for(slot_k):
        def make_copy(local_row, global_row, size):
            return pltpu.make_async_copy(sb.at[slot_k, pl.ds(local_row, size)],
                                         xs_ref.at[pl.ds(global_row, size)], sems.at[slot_k])
        return make_copy

    def tile_rows(tile):
        j = tile * N_EXPERTS + N_EXPERTS - 1
        return start_s[j] + len_s[j]

    for k in range(2):
        @pl.when(slot == k)
        def _(k=k):
            _segment_copies(i, start_s, len_s, dst_s, copy_for(k), wait=False)

            @pl.when(i > 0)
            def _():
                _wait_rows(tile_rows(i - 1), lambda size: copy_for(1 - k)(0, 0, size))

            @pl.when(i == last)
            def _():
                _wait_rows(tile_rows(i), lambda size: copy_for(k)(0, 0, size))

    @pl.when(i == last)
    def _():
        zb[...] = jnp.zeros_like(zb)

        def tail_copy(local_row, global_row, size):
            del local_row
            return pltpu.make_async_copy(zb.at[pl.ds(0, size)], xs_ref.at[pl.ds(global_row, size)], sems.at[0])

        _segment_copies(last + 1, start_s, len_s, dst_s, tail_copy, wait=False)
        _segment_copies(last + 1, start_s, len_s, dst_s, tail_copy, wait=True)

        def fill(j, carry):
            cp = pltpu.make_async_copy(zb, xs_ref.at[pl.ds(pl.multiple_of(j * TMF, TMF), TMF)], sems.at[1])
            cp.start()
            cp.wait()
            return carry

        lax.fori_loop(dst_s[(last + 2) * N_EXPERTS], xs_ref.shape[0] // TMF, fill, 0)


def _dispatch(hp, ri, rwt, seg_start, seg_len, seg_dst, n_rows):
    ntiles = ri.shape[0]
    grid_spec = pltpu.PrefetchScalarGridSpec(
        num_scalar_prefetch=3,
        grid=(ntiles,),
        in_specs=[
            pl.BlockSpec((TM, D_MODEL), lambda i, *_: (i, 0)),
            pl.BlockSpec((1, 4, TM), lambda i, *_: (i, 0, 0)),
            pl.BlockSpec((1, 2, TM), lambda i, *_: (i, 0, 0)),
        ],
        out_specs=pl.BlockSpec(memory_space=pl.ANY),
        scratch_shapes=[pltpu.VMEM((2, SORT_ROWS, XS_COLS), U32), pltpu.VMEM((TMF, XS_COLS), U32),
                        pltpu.SemaphoreType.DMA((2,))],
    )
    return pl.pallas_call(
        _dispatch_kernel,
        grid_spec=grid_spec,
        out_shape=jax.ShapeDtypeStruct((n_rows, XS_COLS), U32),
        compiler_params=_cparams(("arbitrary",)),
    )(seg_start, seg_len, seg_dst, hp, ri, rwt)


def _ffn_kernel(te_ref, tv_ref, ts_ref, xs_ref, wg_ref, wu_ref, wd_ref, ys_ref, wgb, wub, wdb):
    del ts_ref
    i = pl.program_id(0)
    prev = te_ref[jnp.maximum(i - 1, 0)]
    fresh = (i == 0) | (te_ref[i] != prev)

    @pl.when(fresh)
    def _():
        wgb[...] = wg_ref[0, 0].astype(BF16)
        wub[...] = wu_ref[0, 0].astype(BF16)
        wdb[...] = wd_ref[0, 0].astype(BF16)

    @pl.when(tv_ref[i] > 0)
    def _():
        hi, lo = _unpack_rows(xs_ref[:, 0:D_MODEL // 2])
        xb = jnp.concatenate([hi.astype(BF16), lo.astype(BF16)], axis=1)
        w = lax.bitcast_convert_type(xs_ref[:, D_MODEL // 2:D_MODEL // 2 + 1], F32)
        g = jnp.dot(xb, wgb[...], preferred_element_type=F32)
        u = jnp.dot(xb, wub[...], preferred_element_type=F32)
        he = ((g * (1.0 + jnp.tanh(0.5 * g))) * (u * (0.5 * w))).astype(BF16)
        ys_ref[...] = _pack_rows(jnp.dot(he, wdb[...], preferred_element_type=F32))

    @pl.when(tv_ref[i] == 0)
    def _():
        ys_ref[...] = jnp.zeros_like(ys_ref)


def _ffn(xs, tile_e, tile_v, tile_src, wg, wu, wd, layer):
    n_rows = xs.shape[0]
    n_tiles = n_rows // TMF
    grid_spec = pltpu.PrefetchScalarGridSpec(
        num_scalar_prefetch=3,
        grid=(n_tiles,),
        in_specs=[
            pl.BlockSpec((TMF, XS_COLS), lambda i, te, tv, ts: (ts[i], 0)),
            pl.BlockSpec((1, 1, D_MODEL, D_EXPERT), lambda i, te, tv, ts: (layer, te[i], 0, 0)),
            pl.BlockSpec((1, 1, D_MODEL, D_EXPERT), lambda i, te, tv, ts: (layer, te[i], 0, 0)),
            pl.BlockSpec((1, 1, D_EXPERT, D_MODEL), lambda i, te, tv, ts: (layer, te[i], 0, 0)),
        ],
        out_specs=pl.BlockSpec((TMF, D_MODEL // 2), lambda i, te, tv, ts: (i, 0)),
        scratch_shapes=[
            pltpu.VMEM((D_MODEL, D_EXPERT), BF16),
            pltpu.VMEM((D_MODEL, D_EXPERT), BF16),
            pltpu.VMEM((D_EXPERT, D_MODEL), BF16),
        ],
    )
    return pl.pallas_call(
        _ffn_kernel,
        grid_spec=grid_spec,
        out_shape=jax.ShapeDtypeStruct((n_rows, D_MODEL // 2), U32),
        compiler_params=_cparams(("arbitrary",)),
    )(tile_e, tile_v, tile_src, xs, wg, wu, wd)


def _combine_kernel(start_s, len_s, dst_s, x_ref, ri_ref, g2_ref, fg_ref, ys_ref, o_ref, yb, sems, *, final):
    nt = pl.num_programs(1)
    i = pl.program_id(0) * nt + pl.program_id(1)
    last = pl.num_programs(0) * nt - 1
    slot = lax.rem(i, 2)

    def copy_for(slot_k):
        def make_copy(local_row, global_row, size):
            return pltpu.make_async_copy(ys_ref.at[pl.ds(global_row, size)],
                                         yb.at[slot_k, pl.ds(local_row, size)], sems.at[slot_k])
        return make_copy

    @pl.when(i == 0)
    def _():
        yb[...] = jnp.zeros_like(yb)
        _segment_copies(i, start_s, len_s, dst_s, copy_for(0), wait=False)

    for k in range(2):
        @pl.when(slot == k)
        def _(k=k):
            @pl.when(i < last)
            def _():
                _segment_copies(i + 1, start_s, len_s, dst_s, copy_for(1 - k), wait=False)

            j = i * N_EXPERTS + N_EXPERTS - 1
            _wait_rows(start_s[j] + len_s[j], lambda size: copy_for(k)(0, 0, size))

    hi, lo = _unpack_rows(yb[slot])
    m1, m2 = _sort_matrix(ri_ref)
    perm = jnp.where(m1 | m2, 1.0, 0.0).astype(BF16)
    tn_dims = (((0,), (0,)), ((), ()))
    moe = jnp.concatenate(
        [lax.dot_general(perm, hi.astype(BF16), tn_dims, preferred_element_type=F32),
         lax.dot_general(perm, lo.astype(BF16), tn_dims, preferred_element_type=F32)], axis=1)
    x = x_ref[0] + g2_ref[0] * moe
    if final:
        ms = jnp.mean(x * x, axis=-1, keepdims=True)
        x = x * lax.rsqrt(ms + EPS) * fg_ref[...]
    o_ref[0] = x


def _combine(xa, ys, ri, seg_start, seg_len, seg_dst, mods, final_g, t0, final):
    b, ntok, _ = xa.shape
    nt = ntok // TM - t0
    mod_row = lambda bi, t: jnp.where(t + t0 == 0, b, bi)
    if final:
        out_shape = jax.ShapeDtypeStruct((b, nt * TM, D_MODEL), F32)
        out_spec = pl.BlockSpec((1, TM, D_MODEL), lambda bi, t, *_: (bi, t, 0))
        aliases = {}
    else:
        out_shape = jax.ShapeDtypeStruct(xa.shape, F32)
        out_spec = pl.BlockSpec((1, TM, D_MODEL), lambda bi, t, *_: (bi, t + t0, 0))
        aliases = {3: 0}
    grid_spec = pltpu.PrefetchScalarGridSpec(
        num_scalar_prefetch=3,
        grid=(b, nt),
        in_specs=[
            pl.BlockSpec((1, TM, D_MODEL), lambda bi, t, *_: (bi, t + t0, 0)),
            pl.BlockSpec((1, 4, TM), lambda bi, t, *_: (bi * nt + t, 0, 0)),
            pl.BlockSpec((1, 1, D_MODEL), lambda bi, t, *_: (mod_row(bi, t), 0, 5)),
            pl.BlockSpec((1, D_MODEL), lambda bi, t, *_: (0, 0)),
            pl.BlockSpec(memory_space=pl.ANY),
        ],
        out_specs=out_spec,
        scratch_shapes=[pltpu.VMEM((2, SORT_ROWS, D_MODEL // 2), U32), pltpu.SemaphoreType.DMA((2,))],
    )
    return pl.pallas_call(
        functools.partial(_combine_kernel, final=final),
        grid_spec=grid_spec,
        out_shape=out_shape,
        input_output_aliases=aliases,
        compiler_params=_cparams(("arbitrary", "arbitrary")),
    )(seg_start, seg_len, seg_dst, xa, ri, mods, final_g.reshape(1, -1), ys)


def _route_tables(meta, cnt):
    ntiles = meta.shape[0]
    m = meta[:, :, 0].reshape(ntiles, 3, N_EXPERTS)
    seg_start, seg_len, before = m[:, 0], m[:, 1], m[:, 2]
    totals = cnt[:, 0].astype(I32)
    region = ((totals + TMF - 1) // TMF) * TMF
    ends = jnp.cumsum(region)
    seg_dst = (ends - region)[None, :] + before
    seg_start = jnp.concatenate([seg_start, jnp.zeros((1, N_EXPERTS), I32)], axis=0)
    seg_len = jnp.concatenate([seg_len, (region - totals)[None, :]], axis=0)
    seg_dst = jnp.concatenate([seg_dst, (ends - region + totals)[None, :]], axis=0)
    max_rows = 2 * ntiles * TM + ntiles * N_EXPERTS * (SEG_ALIGN - 1)
    n_tiles = max_rows // TMF + 1 + N_EXPERTS
    tile_id = jnp.arange(n_tiles, dtype=I32)
    starts = tile_id * TMF
    tile_e = jnp.minimum(jnp.sum(starts[:, None] >= ends[None, :], axis=1), N_EXPERTS - 1).astype(I32)
    valid = starts < ends[-1]
    last_used = jnp.maximum(ends[-1] // TMF - 1, 0)
    tile_e = jnp.where(valid, tile_e, tile_e[last_used])
    tile_src = jnp.minimum(tile_id, last_used)
    flat = lambda v: v.reshape(-1).astype(I32)
    seg_dst = jnp.concatenate([flat(seg_dst), (ends[-1:] // TMF).astype(I32)])
    return flat(seg_start), flat(seg_len), seg_dst, tile_e, valid.astype(I32), tile_src, n_tiles * TMF


def _rope_tables(n_lat):
    quarter = HEAD_DIM // 4
    inv = ROPE_BASE ** (-jnp.arange(quarter, dtype=F32) / quarter)
    n_rows = n_lat // GRID_W
    rows = jnp.repeat(jnp.arange(n_rows, dtype=F32), GRID_W)
    cols = jnp.tile(jnp.arange(GRID_W, dtype=F32), n_rows)
    ang = jnp.concatenate([rows[:, None] * inv, cols[:, None] * inv], axis=-1)
    cos = jnp.cos(ang)
    sin = jnp.sin(ang)
    cos_t = jnp.tile(cos, (1, LANES // cos.shape[1]))
    sin_t = jnp.tile(jnp.concatenate([-sin, sin], axis=-1), (1, LANES // HEAD_DIM))
    cos_t = jnp.concatenate([jnp.ones((CTX_LEN, LANES), F32), cos_t], axis=0)
    sin_t = jnp.concatenate([jnp.zeros((CTX_LEN, LANES), F32), sin_t], axis=0)
    return cos_t, sin_t


def kernel(x, c, ctx, c_ctx, w_ada, b_ada, norm_mix_g, norm_ffn_g, w_in, diff_lambda, diff_subln_g, swa_sink,
           ret_decay, ret_gn_g, w_branch, w_out, router_w, router_b, w_exp_gate, w_exp_up, w_exp_down, final_g):
    b, n_lat, _ = x.shape
    depth = w_ada.shape[0]
    assert ctx.shape[1] == CTX_LEN == TM and n_lat % TM == 0 and n_lat >= 2 * TM

    pad_rows = (-(b + 1)) % 8
    c_all = jnp.concatenate([c, c_ctx[None, :], jnp.zeros((pad_rows, D_MODEL), F32)], axis=0)
    mods_all = _ada(c_all, w_ada, b_ada)
    cos_t, sin_t = _rope_tables(n_lat)
    xa = x
    rw_t = router_w.T.astype(BF16)
    rb_col = router_b.reshape(-1, 1).astype(F32)

    for l in range(depth):
        last = l == depth - 1
        t0 = 1 if last else 0
        lam_init = 0.8 - 0.6 * math.exp(-0.3 * l)
        mods = mods_all[l].reshape(mods_all.shape[1], 1, -1)
        ctx_in = ctx if l == 0 else None
        p = _proj(ctx_in, xa, mods, norm_mix_g[l], cos_t, sin_t, w_in[l].astype(BF16), ctx_kv_only=last)
        oa = _diff(p, diff_lambda[l], diff_subln_g[l], lam_init, t0)
        ob = _win(p, swa_sink[l], t0)
        oc = _ret(p, ret_decay[l])
        xa, hp, ri, rwt, meta, cnt = _merge(ctx_in, xa, p, oa, ob, oc, mods, ret_gn_g[l], norm_ffn_g[l],
                                            w_branch[l].astype(BF16), w_out[l].astype(BF16), rw_t, rb_col, t0)
        seg_start, seg_len, seg_dst, tile_e, tile_v, tile_src, n_rows = _route_tables(meta, cnt)
        xs = _dispatch(hp, ri, rwt, seg_start, seg_len, seg_dst, n_rows)
        ys = _ffn(xs, tile_e, tile_v, tile_src, w_exp_gate, w_exp_up, w_exp_down, l)
        xa = _combine(xa, ys, ri, seg_start, seg_len, seg_dst, mods, final_g, t0, last)
    return xa
```

```python
import functools
import math

import jax
import jax.numpy as jnp
from jax import lax
from jax.experimental import pallas as pl
from jax.experimental.pallas import tpu as pltpu

F32 = jnp.float32
BF16 = jnp.bfloat16
U32 = jnp.uint32
I32 = jnp.int32

D_MODEL = 1024
CTX_LEN = 256
GRID_W = 64
ROPE_BASE = 10000.0
EPS = 1e-6
LOG2E = 1.4426950408889634
HEAD_DIM = 64
WINDOW = 128
RT_CHUNK = 128
N_EXPERTS = 16
N_GROUPS = 4
D_EXPERT = 512
LANES = 128
TM = 256
DIFF_HEADS = 4
QK_AHEAD = 2
TMF = 512
SEG_ALIGN = 8
SEG_SIZES = (256, 128, 64, 32, 16, 8)
SEG_BIG = 64
SORT_ROWS = 640
XS_COLS = D_MODEL // 2 + 128
D_IN = 6912
N_SRC_BLK = D_IN // LANES
N_OUT_BLK = N_SRC_BLK + 2
VMEM_LIMIT = 56 * 1024 * 1024

BLK_AQ, BLK_AK, BLK_AV = 0, 4, 8
BLK_BQ, BLK_BK, BLK_BV = 12, 16, 18
BLK_CQ, BLK_CK, BLK_CV, BLK_CG, BLK_GATE = 20, 22, 24, 28, 32


def _cparams(sem):
    return pltpu.CompilerParams(dimension_semantics=sem, vmem_limit_bytes=VMEM_LIMIT)


def _ada_kernel(c_ref, w_ref, b_ref, o_ref):
    c = c_ref[...]
    s = (c * jax.nn.sigmoid(c)).astype(BF16)
    o_ref[0] = jnp.dot(s, w_ref[0].astype(BF16), preferred_element_type=F32) + b_ref[0]


def _ada(c_all, w_ada, b_ada):
    depth = w_ada.shape[0]
    rows = c_all.shape[0]
    nj = w_ada.shape[2] // D_MODEL
    return pl.pallas_call(
        _ada_kernel,
        grid=(depth, nj),
        in_specs=[
            pl.BlockSpec((rows, D_MODEL), lambda l, j: (0, 0)),
            pl.BlockSpec((1, D_MODEL, D_MODEL), lambda l, j: (l, 0, j)),
            pl.BlockSpec((1, 1, D_MODEL), lambda l, j: (l, 0, j)),
        ],
        out_specs=pl.BlockSpec((1, rows, D_MODEL), lambda l, j: (l, 0, j)),
        out_shape=jax.ShapeDtypeStruct((depth, rows, w_ada.shape[2]), F32),
        compiler_params=_cparams(("arbitrary", "arbitrary")),
    )(c_all, w_ada, b_ada.reshape(depth, 1, -1))


def _rope(a, cos, sin_signed, first_half):
    rot = jnp.where(first_half, pltpu.roll(a, LANES - HEAD_DIM // 2, 1), pltpu.roll(a, HEAD_DIM // 2, 1))
    return a * cos + rot * sin_signed


def _token_tile(ctx_ref, x_ref, tile):
    if ctx_ref is None:
        return x_ref[0]
    return jnp.where(tile == 0, ctx_ref[0], x_ref[0])


def _token_specs(ctx, xa, t0):
    if ctx is None:
        return [xa], [pl.BlockSpec((1, TM, D_MODEL), lambda bi, t, *_: (bi, t + t0, 0))]
    return [ctx, xa], [pl.BlockSpec((1, TM, D_MODEL), lambda bi, t, *_: (bi, 0, 0)),
                       pl.BlockSpec((1, TM, D_MODEL), lambda bi, t, *_: (bi, jnp.maximum(t + t0 - 1, 0), 0))]


CTX_KV_CHUNKS = (2, 3, 4, 5, 8, 10, 11, 12)


def _proj_kernel(*refs, split, ctx_kv_only):
    ctx_ref, refs = (refs[0], refs[1:]) if split else (None, refs)
    x_ref, sh_ref, sc_ref, g_ref, cos_ref, sin_ref, w_ref, o_ref = refs
    t = pl.program_id(1)
    x = _token_tile(ctx_ref, x_ref, t)
    ms = jnp.mean(x * x, axis=-1, keepdims=True)
    h = x * lax.rsqrt(ms + EPS) * g_ref[...]
    h = h * (1.0 + sc_ref[0]) + sh_ref[0]
    hb = h.astype(BF16)
    cos = cos_ref[...]
    sin = sin_ref[...]
    lane = lax.broadcasted_iota(I32, (1, LANES), 1)
    first_half = (lane & (HEAD_DIM - 1)) < (HEAD_DIM // 2)
    low = lane < HEAD_DIM
    scale = HEAD_DIM ** -0.5

    def project(chunks):
        for c in range(N_SRC_BLK // 2):
            if c not in chunks:
                for src in (2 * c, 2 * c + 1):
                    dst = src if src < 16 else src + 2
                    o_ref[0, :, dst * LANES:(dst + 1) * LANES] = jnp.zeros((TM, LANES), BF16)
                continue
            acc = jnp.dot(hb, w_ref[:, c * 2 * LANES:(c + 1) * 2 * LANES], preferred_element_type=F32)
            for half in range(2):
                src = 2 * c + half
                a = acc[:, half * LANES:(half + 1) * LANES]
                if src < 8 or 12 <= src < 17:
                    a = _rope(a, cos, sin, first_half)
                if src < 4 or 12 <= src < 16:
                    a = a * (scale * LOG2E)
                if 20 <= src < 22:
                    a = a * scale
                if src >= 26:
                    a = a * 0.5
                if src in (16, 17):
                    sw = pltpu.roll(a, HEAD_DIM, 1)
                    dst = BLK_BK if src == 16 else BLK_BV
                    o_ref[0, :, dst * LANES:(dst + 1) * LANES] = jnp.where(low, a, sw).astype(BF16)
                    o_ref[0, :, (dst + 1) * LANES:(dst + 2) * LANES] = jnp.where(low, sw, a).astype(BF16)
                else:
                    dst = src if src < 16 else src + 2
                    o_ref[0, :, dst * LANES:(dst + 1) * LANES] = a.astype(BF16)

    all_chunks = tuple(range(N_SRC_BLK // 2))
    if ctx_kv_only:
        @pl.when(t == 0)
        def _():
            project(CTX_KV_CHUNKS)

        @pl.when(t > 0)
        def _():
            project(all_chunks)
    else:
        project(all_chunks)


def _proj(ctx, xa, mods, norm_g, cos_t, sin_t, w_in_bf, ctx_kv_only):
    b = xa.shape[0]
    ntok = xa.shape[1] + (0 if ctx is None else CTX_LEN)
    nt = ntok // TM
    mod_row = lambda bi, t: jnp.where(t == 0, b, bi)
    tok_args, tok_specs = _token_specs(ctx, xa, 0)
    return pl.pallas_call(
        functools.partial(_proj_kernel, split=ctx is not None, ctx_kv_only=ctx_kv_only),
        grid=(b, nt),
        in_specs=tok_specs + [
            pl.BlockSpec((1, 1, D_MODEL), lambda bi, t: (mod_row(bi, t), 0, 0)),
            pl.BlockSpec((1, 1, D_MODEL), lambda bi, t: (mod_row(bi, t), 0, 1)),
            pl.BlockSpec((1, D_MODEL), lambda bi, t: (0, 0)),
            pl.BlockSpec((TM, LANES), lambda bi, t: (t, 0)),
            pl.BlockSpec((TM, LANES), lambda bi, t: (t, 0)),
            pl.BlockSpec((D_MODEL, D_IN), lambda bi, t: (0, 0)),
        ],
        out_specs=pl.BlockSpec((1, TM, N_OUT_BLK * LANES), lambda bi, t: (bi, t, 0)),
        out_shape=jax.ShapeDtypeStruct((b, ntok, N_OUT_BLK * LANES), BF16),
        compiler_params=_cparams(("arbitrary", "arbitrary")),
    )(*tok_args, mods, mods, norm_g.reshape(1, -1), cos_t, sin_t, w_in_bf)


def _diff_kernel(lam_ref, g_ref, q_ref, k_ref, v_ref, o_ref, *, lam_init, t0, ntok):
    t = pl.program_id(2) + t0
    lv = lam_ref[...]
    lam = (jnp.exp(jnp.sum(lv[0:1] * lv[1:2], axis=-1, keepdims=True))
           - jnp.exp(jnp.sum(lv[2:3] * lv[3:4], axis=-1, keepdims=True)) + lam_init)
    lane = lax.broadcasted_iota(I32, (1, LANES), 1)
    nt_dims = (((1,), (1,)), ((), ()))

    def attend(nk):
        def scores(hd):
            cols = slice(hd * LANES, (hd + 1) * LANES)
            q = q_ref[0, :, cols]
            zero = jnp.zeros_like(q)
            q0 = jnp.where(lane < HEAD_DIM, q, zero)
            q1 = jnp.where(lane >= HEAD_DIM, q, zero)
            k = k_ref[0, 0:nk, cols]
            return (lax.dot_general(q0, k, nt_dims, preferred_element_type=F32),
                    lax.dot_general(q1, k, nt_dims, preferred_element_type=F32))

        ones_blk = jnp.where(lax.broadcasted_iota(I32, (nk, LANES), 1) == 0, 1.0, 0.0).astype(BF16)
        queue = [scores(hd) for hd in range(min(QK_AHEAD, DIFF_HEADS))]
        for hd in range(DIFF_HEADS):
            cols = slice(hd * LANES, (hd + 1) * LANES)
            v1 = jnp.concatenate([v_ref[0, 0:nk, cols], ones_blk], axis=1)
            s0, s1 = queue.pop(0)
            if hd + QK_AHEAD < DIFF_HEADS:
                queue.append(scores(hd + QK_AHEAD))
            e0 = jnp.exp2(s0 - jnp.max(s0, axis=-1, keepdims=True)).astype(BF16)
            e1 = jnp.exp2(s1 - jnp.max(s1, axis=-1, keepdims=True)).astype(BF16)
            ov0 = jnp.dot(e0, v1, preferred_element_type=F32)
            ov1 = jnp.dot(e1, v1, preferred_element_type=F32)
            r0 = 1.0 / ov0[:, LANES:LANES + 1]
            r1 = lam / ov1[:, LANES:LANES + 1]
            o = ov0[:, 0:LANES] * r0 - ov1[:, 0:LANES] * r1
            o = o * lax.rsqrt(jnp.mean(o * o, axis=-1, keepdims=True) + EPS) * g_ref[...] * (1.0 - lam_init)
            o_ref[0, :, cols] = o.astype(BF16)

    if t0 == 0:
        @pl.when(t == 0)
        def _():
            attend(CTX_LEN)

        @pl.when(t > 0)
        def _():
            attend(ntok)
    else:
        attend(ntok)


def _diff(p, lam_vec, subln_g, lam_init, t0):
    b, ntok, _ = p.shape
    nt = ntok // TM - t0
    heads = 4
    hw = DIFF_HEADS * LANES
    return pl.pallas_call(
        functools.partial(_diff_kernel, lam_init=lam_init, t0=t0, ntok=ntok),
        grid=(b, heads // DIFF_HEADS, nt),
        in_specs=[
            pl.BlockSpec((4, HEAD_DIM), lambda bi, h, t: (0, 0)),
            pl.BlockSpec((1, LANES), lambda bi, h, t: (0, 0)),
            pl.BlockSpec((1, TM, hw), lambda bi, h, t: (bi, t + t0, BLK_AQ // DIFF_HEADS + h)),
            pl.BlockSpec((1, ntok, hw), lambda bi, h, t: (bi, 0, BLK_AK // DIFF_HEADS + h)),
            pl.BlockSpec((1, ntok, hw), lambda bi, h, t: (bi, 0, BLK_AV // DIFF_HEADS + h)),
        ],
        out_specs=pl.BlockSpec((1, TM, hw), lambda bi, h, t: (bi, t, h)),
        out_shape=jax.ShapeDtypeStruct((b, nt * TM, heads * LANES), BF16),
        compiler_params=_cparams(("arbitrary", "arbitrary", "arbitrary")),
    )(lam_vec, subln_g.reshape(1, -1), p, p, p)


def _win_kernel(sink_ref, *refs, nsub, ntok, ctx_queries):
    q_refs, (k_ref, v_ref), o_ref = refs[:nsub], refs[nsub:nsub + 2], refs[-1]
    lane = lax.broadcasted_iota(I32, (1, LANES), 1)
    low = lane < HEAD_DIM
    nt_dims = (((1,), (1,)), ((), ()))
    wk = 2 * TM
    nkeys = CTX_LEN if ctx_queries else CTX_LEN + wk
    n_heads = 8
    group = n_heads // 2
    ones_blk = jnp.where(lax.broadcasted_iota(I32, (nkeys, LANES), 1) == 0, 1.0, 0.0).astype(BF16)

    biases, keys, vals = [], [], []
    for sub in range(nsub):
        if ctx_queries:
            biases.append(None)
            window = lambda ref, cols: ref[0, :, cols]
        else:
            t = 1 + nsub * pl.program_id(1) + sub
            start = pl.multiple_of(jnp.clip(t * TM - WINDOW, CTX_LEN, ntok - wk), WINDOW)
            qpos = t * TM + lax.broadcasted_iota(I32, (TM, wk), 0)
            kpos = start + lax.broadcasted_iota(I32, (TM, wk), 1)
            in_band = jnp.abs(kpos - qpos) <= WINDOW
            biases.append(jnp.concatenate([jnp.zeros((TM, CTX_LEN), F32),
                                           jnp.where(in_band, 0.0, -jnp.inf).astype(F32)], axis=1))
            window = lambda ref, cols, start=start: jnp.concatenate(
                [ref[0, 0:CTX_LEN, cols], ref[0, pl.ds(start, wk), cols]], axis=0)
        keys.append([window(k_ref, slice(g * LANES, (g + 1) * LANES)) for g in range(2)])
        vals.append([jnp.concatenate([window(v_ref, slice(g * LANES, (g + 1) * LANES)), ones_blk], axis=1)
                     for g in range(2)])

    def scores(item):
        h, sub = item
        qb = q_refs[sub][0, :, (h // 2) * LANES:(h // 2 + 1) * LANES]
        qm = jnp.where(low if h % 2 == 0 else jnp.logical_not(low), qb, jnp.zeros_like(qb))
        return lax.dot_general(qm, keys[sub][h // group], nt_dims, preferred_element_type=F32)

    items = [(h, sub) for h in range(n_heads) for sub in range(nsub)]
    ahead = QK_AHEAD * nsub
    queue = [scores(it) for it in items[:ahead]]
    outs = {}
    for i, (h, sub) in enumerate(items):
        s = queue.pop(0)
        if biases[sub] is not None:
            s = s + biases[sub]
        if i + ahead < len(items):
            queue.append(scores(items[i + ahead]))
        sk = sink_ref[h] * LOG2E
        m = jnp.maximum(jnp.max(s, axis=-1, keepdims=True), sk)
        e = jnp.exp2(s - m).astype(BF16)
        ov = jnp.dot(e, vals[sub][h // group], preferred_element_type=F32)
        l = ov[:, LANES:LANES + 1] + jnp.exp2(sk - m)
        outs[h, sub] = ov[:, 0:LANES] * (1.0 / l)
        if h % 2 == 1:
            o_ref[0, sub * TM:(sub + 1) * TM, (h // 2) * LANES:(h // 2 + 1) * LANES] = (
                jnp.where(low, outs.pop((h - 1, sub)), outs.pop((h, sub))).astype(BF16))


WIN_TILES = 4


def _win(p, sink, with_ctx):
    b, ntok, _ = p.shape
    n_lat = ntok // TM - 1
    assert n_lat % WIN_TILES == 0
    steps = n_lat // WIN_TILES
    q_spec = lambda sub: pl.BlockSpec((1, TM, 4 * LANES),
                                      lambda bi, j: (bi, 1 + WIN_TILES * j + sub, BLK_BQ // 4))
    out = pl.pallas_call(
        functools.partial(_win_kernel, nsub=WIN_TILES, ntok=ntok, ctx_queries=False),
        grid=(b, steps),
        in_specs=[pl.BlockSpec(memory_space=pltpu.SMEM)] + [q_spec(sub) for sub in range(WIN_TILES)] + [
            pl.BlockSpec((1, ntok, 2 * LANES), lambda bi, j: (bi, 0, BLK_BK // 2)),
            pl.BlockSpec((1, ntok, 2 * LANES), lambda bi, j: (bi, 0, BLK_BV // 2)),
        ],
        out_specs=pl.BlockSpec((1, WIN_TILES * TM, 4 * LANES), lambda bi, j: (bi, j, 0)),
        out_shape=jax.ShapeDtypeStruct((b, n_lat * TM, 4 * LANES), BF16),
        compiler_params=_cparams(("arbitrary", "arbitrary")),
    )(sink, *([p] * WIN_TILES), p, p)
    if not with_ctx:
        return None, out
    out_ctx = pl.pallas_call(
        functools.partial(_win_kernel, nsub=1, ntok=ntok, ctx_queries=True),
        grid=(b, 1),
        in_specs=[
            pl.BlockSpec(memory_space=pltpu.SMEM),
            pl.BlockSpec((1, TM, 4 * LANES), lambda bi, j: (bi, 0, BLK_BQ // 4)),
            pl.BlockSpec((1, CTX_LEN, 2 * LANES), lambda bi, j: (bi, 0, BLK_BK // 2)),
            pl.BlockSpec((1, CTX_LEN, 2 * LANES), lambda bi, j: (bi, 0, BLK_BV // 2)),
        ],
        out_specs=pl.BlockSpec((1, TM, 4 * LANES), lambda bi, j: (bi, 0, 0)),
        out_shape=jax.ShapeDtypeStruct((b, TM, 4 * LANES), BF16),
        compiler_params=_cparams(("arbitrary", "arbitrary")),
    )(sink, p, p, p)
    return out_ctx, out


def _ret_kernel(dec_ref, q_ref, k_ref, v_ref, o_ref, ob_ref, *, ntok):
    T = RT_CHUNK
    nchunk = ntok // T
    nctx = CTX_LEN // T
    x = dec_ref[...]
    lg_all = -(jnp.maximum(-x, 0.0) + jnp.log1p(jnp.exp(-jnp.abs(x))))
    lane = lax.broadcasted_iota(I32, (1, LANES), 1)
    low = lane < HEAD_DIM
    row_low = lax.broadcasted_iota(I32, (LANES, 1), 0) < HEAD_DIM
    pi = lax.broadcasted_iota(I32, (T, T), 0).astype(F32)
    pj = lax.broadcasted_iota(I32, (T, T), 1).astype(F32)
    pcol = lax.broadcasted_iota(I32, (T, 1), 0).astype(F32)
    nt_dims = (((1,), (1,)), ((), ()))
    tn_dims = (((0,), (0,)), ((), ()))

    decay, qdp, kdp, cdp = [], [], [], []
    for direction in range(2):
        lgs = [lg_all[direction:direction + 1, h:h + 1] for h in range(4)]
        if direction == 0:
            dist = pi - pj
            qd = [jnp.exp((pcol + 1.0) * lg) for lg in lgs]
            kd = [jnp.exp((T - 1.0 - pcol) * lg) for lg in lgs]
        else:
            dist = pj - pi
            qd = [jnp.exp((T - pcol) * lg) for lg in lgs]
            kd = [jnp.exp(pcol * lg) for lg in lgs]
        decay.append([jnp.where(dist >= 0, jnp.exp(jnp.maximum(dist, 0.0) * lg), 0.0) for lg in lgs])
        qdp.append([jnp.where(low, qd[2 * pr], qd[2 * pr + 1]) for pr in range(2)])
        kdp.append([jnp.where(low, kd[2 * pr], kd[2 * pr + 1]) for pr in range(2)])
        cdp.append([jnp.where(row_low, jnp.exp(T * lgs[2 * pr]), jnp.exp(T * lgs[2 * pr + 1])) for pr in range(2)])

    def step(i, states):
        chunks = (i, jnp.where(i < nctx, nctx - 1 - i, nchunk + nctx - 1 - i))
        halves = (low, jnp.logical_not(low))
        work = []
        for direction in range(2):
            r0 = pl.multiple_of(chunks[direction] * T, T)
            for pair in range(2):
                qb = q_ref[0, pl.ds(r0, T), pair * LANES:(pair + 1) * LANES]
                kb = k_ref[0, pl.ds(r0, T), pair * LANES:(pair + 1) * LANES]
                v2 = v_ref[0, pl.ds(r0, T), pair * 2 * LANES:(pair + 1) * 2 * LANES]
                kbd = (kb.astype(F32) * kdp[direction][pair]).astype(BF16)
                qsb = (qb.astype(F32) * qdp[direction][pair]).astype(BF16)
                zq = jnp.zeros_like(qb)
                q2 = jnp.concatenate([jnp.where(hm, qb, zq) for hm in halves], axis=0)
                inner2 = lax.dot_general(q2, kb, nt_dims, preferred_element_type=F32)
                kv2 = lax.dot_general(kbd, v2, tn_dims, preferred_element_type=F32)
                work.append((direction, pair, r0, inner2, kv2, qsb, v2))
        new_states = []
        for direction, pair, r0, inner2, kv2, qsb, v2 in work:
            out_ref = o_ref.at[0] if direction == 0 else ob_ref
            st = states[2 * direction + pair]
            stb = st.astype(BF16)
            zq = jnp.zeros_like(qsb)
            for sub in range(2):
                h = 2 * pair + sub
                inner = inner2[sub * T:(sub + 1) * T] * decay[direction][h]
                lhs = jnp.concatenate([inner.astype(BF16), jnp.where(halves[sub], qsb, zq)], axis=1)
                rhs = jnp.concatenate([v2[:, sub * LANES:(sub + 1) * LANES], stb], axis=0)
                out_ref[pl.ds(r0, T), h * LANES:(h + 1) * LANES] = jnp.dot(lhs, rhs, preferred_element_type=F32)
            new_states.append(st * cdp[direction][pair]
                              + jnp.where(row_low, kv2[:, 0:LANES], kv2[:, LANES:2 * LANES]))
        return tuple(new_states)

    zero = jnp.zeros((LANES, LANES), F32)
    lax.fori_loop(0, nchunk, step, (zero,) * 4, unroll=3)
    o_ref[0] = o_ref[0] + ob_ref[...]


def _ret(p, ret_decay):
    b, ntok, _ = p.shape
    return pl.pallas_call(
        functools.partial(_ret_kernel, ntok=ntok),
        grid=(b,),
        in_specs=[
            pl.BlockSpec((2, 4), lambda bi: (0, 0)),
            pl.BlockSpec((1, ntok, 2 * LANES), lambda bi: (bi, 0, BLK_CQ // 2)),
            pl.BlockSpec((1, ntok, 2 * LANES), lambda bi: (bi, 0, BLK_CK // 2)),
            pl.BlockSpec((1, ntok, 4 * LANES), lambda bi: (bi, 0, BLK_CV // 4)),
        ],
        out_specs=pl.BlockSpec((1, ntok, 4 * LANES), lambda bi: (bi, 0, 0)),
        out_shape=jax.ShapeDtypeStruct((b, ntok, 4 * LANES), F32),
        scratch_shapes=[pltpu.VMEM((ntok, 4 * LANES), F32)],
        compiler_params=_cparams(("arbitrary",)),
    )(ret_decay, p, p, p)


def _pack_rows(h, is_bf16_valued=False):
    half = h.shape[1] // 2
    rnd = (lambda v: v) if is_bf16_valued else (lambda v: v.astype(BF16).astype(F32))
    hi = lax.bitcast_convert_type(rnd(h[:, :half]), U32)
    lo = lax.bitcast_convert_type(rnd(h[:, half:]), U32)
    return hi | (lo >> 16)


def _unpack_rows(p):
    hi = lax.bitcast_convert_type(p & jnp.uint32(0xFFFF0000), F32)
    lo = lax.bitcast_convert_type(p << 16, F32)
    return hi, lo


def _merge_kernel(*refs, split, ob_split, t0):
    ctx_ref, refs = (refs[0], refs[1:]) if split else (None, refs)
    obc_ref, refs = (refs[0], refs[1:]) if ob_split else (None, refs)
    (x_ref, oa_ref, ob_ref, oc_ref, cg_ref, ga_ref, gb_ref, gc_ref, g1_ref, sh2_ref, sc2_ref,
     gn_ref, nf_ref, wb_ref, wo_ref, rw_ref, rb_ref,
     xo_ref, hp_ref, ri_ref, rwt_ref, meta_ref, cnt_ref) = refs
    first = (pl.program_id(0) == 0) & (pl.program_id(1) == 0)

    @pl.when(first)
    def _():
        cnt_ref[...] = jnp.zeros_like(cnt_ref)

    da = jnp.dot(oa_ref[0], wb_ref[0], preferred_element_type=F32)
    db = jnp.dot(_token_tile(obc_ref, ob_ref, pl.program_id(1) + t0), wb_ref[1], preferred_element_type=F32)

    oc = oc_ref[0]
    cgv = cg_ref[0].astype(F32)
    gn = gn_ref[...]
    parts = []
    for h in range(4):
        o = oc[:, h * LANES:(h + 1) * LANES]
        mu = jnp.mean(o, axis=-1, keepdims=True)
        dlt = o - mu
        var = jnp.mean(dlt * dlt, axis=-1, keepdims=True)
        y = dlt * lax.rsqrt(var + EPS) * gn[:, h * LANES:(h + 1) * LANES]
        cgh = cgv[:, h * LANES:(h + 1) * LANES]
        parts.append((y * (cgh * (1.0 + jnp.tanh(cgh)))).astype(BF16))
    ocn = jnp.concatenate(parts, axis=1)

    dc = jnp.dot(ocn, wb_ref[2], preferred_element_type=F32)
    y = (1.0 + jnp.tanh(ga_ref[0].astype(F32))) * da
    y += (1.0 + jnp.tanh(gb_ref[0].astype(F32))) * db
    y += (1.0 + jnp.tanh(gc_ref[0].astype(F32))) * dc
    y2 = jnp.dot(y.astype(BF16), wo_ref[...], preferred_element_type=F32)
    x = _token_tile(ctx_ref, x_ref, pl.program_id(1) + t0) + (0.5 * g1_ref[0]) * y2
    xo_ref[0] = x

    ms = jnp.mean(x * x, axis=-1, keepdims=True)
    h2 = x * lax.rsqrt(ms + EPS) * nf_ref[...]
    h2 = h2 * (1.0 + sc2_ref[0]) + sh2_ref[0]
    h2b = h2.astype(BF16)
    hp_ref[...] = h2b

    nt_dims = (((1,), (1,)), ((), ()))
    logits = lax.dot_general(rw_ref[...], h2b, nt_dims, preferred_element_type=F32)
    sc = jax.nn.sigmoid(logits)
    bi = sc + rb_ref[...]
    ei_i = lax.broadcasted_iota(I32, (N_EXPERTS, 1), 0)
    ei = ei_i.astype(F32)
    epg = N_EXPERTS // N_GROUPS
    egroup = (ei_i >> 2).astype(F32)
    gsum = []
    for g in range(N_GROUPS):
        r = [bi[epg * g + j:epg * g + j + 1, :] for j in range(epg)]
        m = r[0] + r[1]
        for a in range(epg):
            for c in range(a + 1, epg):
                if (a, c) != (0, 1):
                    m = jnp.maximum(m, r[a] + r[c])
        gsum.append(m)
    best = jnp.zeros_like(gsum[0])
    bsc = gsum[0]
    for g in range(1, N_GROUPS):
        upd = gsum[g] > bsc
        best = jnp.where(upd, float(g), best)
        bsc = jnp.where(upd, gsum[g], bsc)
    neg = -jnp.inf
    masked = jnp.where(egroup == best, bi, neg)
    m1 = jnp.max(masked, axis=0, keepdims=True)
    i1 = jnp.min(jnp.where(masked == m1, ei, float(N_EXPERTS)), axis=0, keepdims=True)
    masked2 = jnp.where(ei == i1, neg, masked)
    m2 = jnp.max(masked2, axis=0, keepdims=True)
    i2 = jnp.min(jnp.where(masked2 == m2, ei, float(N_EXPERTS)), axis=0, keepdims=True)
    sel1 = ei == i1
    sel2 = ei == i2
    w1 = jnp.sum(jnp.where(sel1, sc, 0.0), axis=0, keepdims=True)
    w2 = jnp.sum(jnp.where(sel2, sc, 0.0), axis=0, keepdims=True)
    ws = w1 + w2
    rwt_ref[0, 0:1, :] = w1 / ws
    rwt_ref[0, 1:2, :] = w2 / ws

    oh = (sel1 | sel2).astype(F32)
    si = lax.broadcasted_iota(I32, (TM, TM), 0)
    ti = lax.broadcasted_iota(I32, (TM, TM), 1)
    upper = (si < ti).astype(BF16)
    prefix = jnp.dot(oh.astype(BF16), upper, preferred_element_type=F32)
    n_col = jnp.sum(oh, axis=1, keepdims=True)
    seg_len = jnp.floor((n_col + (SEG_ALIGN - 1.0)) * (1.0 / SEG_ALIGN)) * SEG_ALIGN
    seg_len_b = jnp.broadcast_to(seg_len, (N_EXPERTS, LANES))
    er = lax.broadcasted_iota(I32, (N_EXPERTS, N_EXPERTS), 0)
    ec = lax.broadcasted_iota(I32, (N_EXPERTS, N_EXPERTS), 1)
    lower = (ec < er).astype(BF16)
    seg_start_b = jnp.dot(lower, seg_len_b.astype(BF16), preferred_element_type=F32)
    local = prefix + seg_start_b[:, 0:1]
    pos1 = jnp.sum(jnp.where(sel1, local, 0.0), axis=0, keepdims=True)
    pos2 = jnp.sum(jnp.where(sel2, local, 0.0), axis=0, keepdims=True)
    ri_ref[0, 0:1, :] = i1.astype(I32)
    ri_ref[0, 1:2, :] = i2.astype(I32)
    ri_ref[0, 2:3, :] = pos1.astype(I32)
    ri_ref[0, 3:4, :] = pos2.astype(I32)
    meta_ref[0, 0:N_EXPERTS, :] = seg_start_b.astype(I32)
    meta_ref[0, N_EXPERTS:2 * N_EXPERTS, :] = seg_len_b.astype(I32)
    meta_ref[0, 2 * N_EXPERTS:, :] = cnt_ref[...].astype(I32)
    cnt_ref[...] = cnt_ref[...] + seg_len_b


def _merge(ctx, xa, p, oa, ob, oc, mods, gn_g, nf_g, wb_bf, wo_bf, rw_t, rb_col, t0):
    b, ntok, _ = p.shape
    nt = ntok // TM - t0
    ntiles = b * nt
    mod_row = lambda bi, t: jnp.where(t + t0 == 0, b, bi)
    tok = lambda bi, t: (bi, t + t0, 0)
    flat = lambda bi, t: (bi * nt + t, 0, 0)
    tok_args, tok_specs = _token_specs(ctx, xa, t0)
    ob_ctx, ob_lat = ob
    ctx_args, ctx_specs = [], []
    if ctx is not None:
        ctx_args, ctx_specs, tok_args, tok_specs = tok_args[:1], tok_specs[:1], tok_args[1:], tok_specs[1:]
    if ob_ctx is not None:
        ctx_args = ctx_args + [ob_ctx]
        ctx_specs = ctx_specs + [pl.BlockSpec((1, TM, 4 * LANES), lambda bi, t: (bi, 0, 0))]
    outs = pl.pallas_call(
        functools.partial(_merge_kernel, split=ctx is not None, ob_split=ob_ctx is not None, t0=t0),
        grid=(b, nt),
        in_specs=ctx_specs + tok_specs + [
            pl.BlockSpec((1, TM, 4 * LANES), lambda bi, t: (bi, t, 0)),
            pl.BlockSpec((1, TM, 4 * LANES), lambda bi, t: (bi, jnp.maximum(t + t0 - 1, 0), 0)),
            pl.BlockSpec((1, TM, 4 * LANES), tok),
            pl.BlockSpec((1, TM, 4 * LANES), lambda bi, t: (bi, t + t0, BLK_CG // 4)),
            pl.BlockSpec((1, TM, D_MODEL), lambda bi, t: (bi, t + t0, BLK_GATE // 8)),
            pl.BlockSpec((1, TM, D_MODEL), lambda bi, t: (bi, t + t0, BLK_GATE // 8 + 1)),
            pl.BlockSpec((1, TM, D_MODEL), lambda bi, t: (bi, t + t0, BLK_GATE // 8 + 2)),
            pl.BlockSpec((1, 1, D_MODEL), lambda bi, t: (mod_row(bi, t), 0, 2)),
            pl.BlockSpec((1, 1, D_MODEL), lambda bi, t: (mod_row(bi, t), 0, 3)),
            pl.BlockSpec((1, 1, D_MODEL), lambda bi, t: (mod_row(bi, t), 0, 4)),
            pl.BlockSpec((1, 4 * LANES), lambda bi, t: (0, 0)),
            pl.BlockSpec((1, D_MODEL), lambda bi, t: (0, 0)),
            pl.BlockSpec((3, 4 * LANES, D_MODEL), lambda bi, t: (0, 0, 0)),
            pl.BlockSpec((D_MODEL, D_MODEL), lambda bi, t: (0, 0)),
            pl.BlockSpec((N_EXPERTS, D_MODEL), lambda bi, t: (0, 0)),
            pl.BlockSpec((N_EXPERTS, 1), lambda bi, t: (0, 0)),
        ],
        out_specs=[
            pl.BlockSpec((1, TM, D_MODEL), tok),
            pl.BlockSpec((TM, D_MODEL), lambda bi, t: (bi * nt + t, 0)),
            pl.BlockSpec((1, 4, TM), flat),
            pl.BlockSpec((1, 2, TM), flat),
            pl.BlockSpec((1, 3 * N_EXPERTS, LANES), flat),
            pl.BlockSpec((N_EXPERTS, LANES), lambda bi, t: (0, 0)),
        ],
        out_shape=[
            jax.ShapeDtypeStruct((b, ntok, D_MODEL), F32),
            jax.ShapeDtypeStruct((ntiles * TM, D_MODEL), BF16),
            jax.ShapeDtypeStruct((ntiles, 4, TM), I32),
            jax.ShapeDtypeStruct((ntiles, 2, TM), F32),
            jax.ShapeDtypeStruct((ntiles, 3 * N_EXPERTS, LANES), I32),
            jax.ShapeDtypeStruct((N_EXPERTS, LANES), F32),
        ],
        input_output_aliases={} if ctx is not None else {len(ctx_args): 0},
        compiler_params=_cparams(("arbitrary", "arbitrary")),
    )(*ctx_args, *tok_args, oa, ob_lat, oc, p, p, p, p, mods, mods, mods, gn_g.reshape(1, -1), nf_g.reshape(1, -1),
      wb_bf, wo_bf, rw_t, rb_col)
    return outs


def _segment_copies(tile, start_s, len_s, dst_s, make_copy, wait):
    for e in range(N_EXPERTS):
        n = len_s[tile * N_EXPERTS + e]
        a = start_s[tile * N_EXPERTS + e]
        d = dst_s[tile * N_EXPERTS + e]
        def chunks(sizes, n=n, a=a, d=d):
            for size in sizes:
                @pl.when((n & size) != 0)
                def _(size=size):
                    off = n & (-2 * size)
                    cp = make_copy(pl.multiple_of(a + off, SEG_ALIGN), pl.multiple_of(d + off, SEG_ALIGN), size)
                    if wait:
                        cp.wait()
                    else:
                        cp.start()

        big = tuple(s for s in SEG_SIZES if s >= SEG_BIG)
        pl.when(n >= SEG_BIG)(functools.partial(chunks, big))
        chunks(tuple(s for s in SEG_SIZES if s < SEG_BIG))


def _wait_rows(total, make_copy):
    for size in (2 * SEG_SIZES[0],) + SEG_SIZES:
        @pl.when((total & size) != 0)
        def _(size=size):
            make_copy(size).wait()


def _sort_matrix(ri_ref):
    srow = lax.broadcasted_iota(I32, (SORT_ROWS, TM), 0)
    return srow == ri_ref[0, 2:3, :], srow == ri_ref[0, 3:4, :]


def _dispatch_kernel(start_s, len_s, dst_s, h_ref, ri_ref, rw_ref, xs_ref, sb, zb, sems):
    i = pl.program_id(0)
    last = pl.num_programs(0) - 1
    slot = lax.rem(i, 2)

    hb = h_ref[...]
    m1, m2 = _sort_matrix(ri_ref)
    perm = jnp.where(m1 | m2, 1.0, 0.0).astype(BF16)
    rows = jnp.dot(perm, hb, preferred_element_type=F32)
    w = jnp.sum(jnp.where(m1, rw_ref[0, 0:1, :], 0.0) + jnp.where(m2, rw_ref[0, 1:2, :], 0.0),
                axis=1, keepdims=True)
    sb[slot, :, 0:D_MODEL // 2] = _pack_rows(rows, is_bf16_valued=True)
    sb[slot, :, D_MODEL // 2:] = lax.bitcast_convert_type(jnp.broadcast_to(w, (SORT_ROWS, LANES)), U32)

    def copy_for(slot_k):
        def make_copy(local_row, global_row, size):
            return pltpu.make_async_copy(sb.at[slot_k, pl.ds(local_row, size)],
                                         xs_ref.at[pl.ds(global_row, size)], sems.at[slot_k])
        return make_copy

    def tile_rows(tile):
        j = tile * N_EXPERTS + N_EXPERTS - 1
        return start_s[j] + len_s[j]

    for k in range(2):
        @pl.when(slot == k)
        def _(k=k):
            _segment_copies(i, start_s, len_s, dst_s, copy_for(k), wait=False)

            @pl.when(i > 0)
            def _():
                _wait_rows(tile_rows(i - 1), lambda size: copy_for(1 - k)(0, 0, size))

            @pl.when(i == last)
            def _():
                _wait_rows(tile_rows(i), lambda size: copy_for(k)(0, 0, size))

    @pl.when(i == last)
    def _():
        zb[...] = jnp.zeros_like(zb)

        def tail_copy(local_row, global_row, size):
            del local_row
            return pltpu.make_async_copy(zb.at[pl.ds(0, size)], xs_ref.at[pl.ds(global_row, size)], sems.at[0])

        _segment_copies(last + 1, start_s, len_s, dst_s, tail_copy, wait=False)
        _segment_copies(last + 1, start_s, len_s, dst_s, tail_copy, wait=True)

        def fill(j, carry):
            cp = pltpu.make_async_copy(zb, xs_ref.at[pl.ds(pl.multiple_of(j * TMF, TMF), TMF)], sems.at[1])
            cp.start()
            cp.wait()
            return carry

        lax.fori_loop(dst_s[(last + 2) * N_EXPERTS], xs_ref.shape[0] // TMF, fill, 0)


def _dispatch(hp, ri, rwt, seg_start, seg_len, seg_dst, n_rows):
    ntiles = ri.shape[0]
    grid_spec = pltpu.PrefetchScalarGridSpec(
        num_scalar_prefetch=3,
        grid=(ntiles,),
        in_specs=[
            pl.BlockSpec((TM, D_MODEL), lambda i, *_: (i, 0)),
            pl.BlockSpec((1, 4, TM), lambda i, *_: (i, 0, 0)),
            pl.BlockSpec((1, 2, TM), lambda i, *_: (i, 0, 0)),
        ],
        out_specs=pl.BlockSpec(memory_space=pl.ANY),
        scratch_shapes=[pltpu.VMEM((2, SORT_ROWS, XS_COLS), U32), pltpu.VMEM((TMF, XS_COLS), U32),
                        pltpu.SemaphoreType.DMA((2,))],
    )
    return pl.pallas_call(
        _dispatch_kernel,
        grid_spec=grid_spec,
        out_shape=jax.ShapeDtypeStruct((n_rows, XS_COLS), U32),
        compiler_params=_cparams(("arbitrary",)),
    )(seg_start, seg_len, seg_dst, hp, ri, rwt)


def _ffn_kernel(te_ref, tv_ref, ts_ref, xs_ref, wg_ref, wu_ref, wd_ref, ys_ref, wgb, wub, wdb):
    del ts_ref
    i = pl.program_id(0)
    prev = te_ref[jnp.maximum(i - 1, 0)]
    fresh = (i == 0) | (te_ref[i] != prev)

    @pl.when(fresh)
    def _():
        wgb[...] = wg_ref[0, 0].astype(BF16)
        wub[...] = wu_ref[0, 0].astype(BF16)
        wdb[...] = wd_ref[0, 0].astype(BF16)

    @pl.when(tv_ref[i] > 0)
    def _():
        hi, lo = _unpack_rows(xs_ref[:, 0:D_MODEL // 2])
        xb = jnp.concatenate([hi.astype(BF16), lo.astype(BF16)], axis=1)
        w = lax.bitcast_convert_type(xs_ref[:, D_MODEL // 2:D_MODEL // 2 + 1], F32)
        g = jnp.dot(xb, wgb[...], preferred_element_type=F32)
        u = jnp.dot(xb, wub[...], preferred_element_type=F32)
        he = ((g * (1.0 + jnp.tanh(0.5 * g))) * (u * (0.5 * w))).astype(BF16)
        ys_ref[...] = _pack_rows(jnp.dot(he, wdb[...], preferred_element_type=F32))

    @pl.when(tv_ref[i] == 0)
    def _():
        ys_ref[...] = jnp.zeros_like(ys_ref)


def _ffn(xs, tile_e, tile_v, tile_src, wg, wu, wd, layer):
    n_rows = xs.shape[0]
    n_tiles = n_rows // TMF
    grid_spec = pltpu.PrefetchScalarGridSpec(
        num_scalar_prefetch=3,
        grid=(n_tiles,),
        in_specs=[
            pl.BlockSpec((TMF, XS_COLS), lambda i, te, tv, ts: (ts[i], 0)),
            pl.BlockSpec((1, 1, D_MODEL, D_EXPERT), lambda i, te, tv, ts: (layer, te[i], 0, 0)),
            pl.BlockSpec((1, 1, D_MODEL, D_EXPERT), lambda i, te, tv, ts: (layer, te[i], 0, 0)),
            pl.BlockSpec((1, 1, D_EXPERT, D_MODEL), lambda i, te, tv, ts: (layer, te[i], 0, 0)),
        ],
        out_specs=pl.BlockSpec((TMF, D_MODEL // 2), lambda i, te, tv, ts: (i, 0)),
        scratch_shapes=[
            pltpu.VMEM((D_MODEL, D_EXPERT), BF16),
            pltpu.VMEM((D_MODEL, D_EXPERT), BF16),
            pltpu.VMEM((D_EXPERT, D_MODEL), BF16),
        ],
    )
    return pl.pallas_call(
        _ffn_kernel,
        grid_spec=grid_spec,
        out_shape=jax.ShapeDtypeStruct((n_rows, D_MODEL // 2), U32),
        compiler_params=_cparams(("arbitrary",)),
    )(tile_e, tile_v, tile_src, xs, wg, wu, wd)


def _combine_kernel(start_s, len_s, dst_s, x_ref, ri_ref, g2_ref, fg_ref, ys_ref, o_ref, yb, sems, *, final):
    nt = pl.num_programs(1)
    i = pl.program_id(0) * nt + pl.program_id(1)
    last = pl.num_programs(0) * nt - 1
    slot = lax.rem(i, 2)

    def copy_for(slot_k):
        def make_copy(local_row, global_row, size):
            return pltpu.make_async_copy(ys_ref.at[pl.ds(global_row, size)],
                                         yb.at[slot_k, pl.ds(local_row, size)], sems.at[slot_k])
        return make_copy

    @pl.when(i == 0)
    def _():
        yb[...] = jnp.zeros_like(yb)
        _segment_copies(i, start_s, len_s, dst_s, copy_for(0), wait=False)

    for k in range(2):
        @pl.when(slot == k)
        def _(k=k):
            @pl.when(i < last)
            def _():
                _segment_copies(i + 1, start_s, len_s, dst_s, copy_for(1 - k), wait=False)

            j = i * N_EXPERTS + N_EXPERTS - 1
            _wait_rows(start_s[j] + len_s[j], lambda size: copy_for(k)(0, 0, size))

    hi, lo = _unpack_rows(yb[slot])
    m1, m2 = _sort_matrix(ri_ref)
    perm = jnp.where(m1 | m2, 1.0, 0.0).astype(BF16)
    tn_dims = (((0,), (0,)), ((), ()))
    moe = jnp.concatenate(
        [lax.dot_general(perm, hi.astype(BF16), tn_dims, preferred_element_type=F32),
         lax.dot_general(perm, lo.astype(BF16), tn_dims, preferred_element_type=F32)], axis=1)
    x = x_ref[0] + g2_ref[0] * moe
    if final:
        ms = jnp.mean(x * x, axis=-1, keepdims=True)
        x = x * lax.rsqrt(ms + EPS) * fg_ref[...]
    o_ref[0] = x


def _combine(xa, ys, ri, seg_start, seg_len, seg_dst, mods, final_g, t0, final):
    b, ntok, _ = xa.shape
    nt = ntok // TM - t0
    mod_row = lambda bi, t: jnp.where(t + t0 == 0, b, bi)
    if final:
        out_shape = jax.ShapeDtypeStruct((b, nt * TM, D_MODEL), F32)
        out_spec = pl.BlockSpec((1, TM, D_MODEL), lambda bi, t, *_: (bi, t, 0))
        aliases = {}
    else:
        out_shape = jax.ShapeDtypeStruct(xa.shape, F32)
        out_spec = pl.BlockSpec((1, TM, D_MODEL), lambda bi, t, *_: (bi, t + t0, 0))
        aliases = {3: 0}
    grid_spec = pltpu.PrefetchScalarGridSpec(
        num_scalar_prefetch=3,
        grid=(b, nt),
        in_specs=[
            pl.BlockSpec((1, TM, D_MODEL), lambda bi, t, *_: (bi, t + t0, 0)),
            pl.BlockSpec((1, 4, TM), lambda bi, t, *_: (bi * nt + t, 0, 0)),
            pl.BlockSpec((1, 1, D_MODEL), lambda bi, t, *_: (mod_row(bi, t), 0, 5)),
            pl.BlockSpec((1, D_MODEL), lambda bi, t, *_: (0, 0)),
            pl.BlockSpec(memory_space=pl.ANY),
        ],
        out_specs=out_spec,
        scratch_shapes=[pltpu.VMEM((2, SORT_ROWS, D_MODEL // 2), U32), pltpu.SemaphoreType.DMA((2,))],
    )
    return pl.pallas_call(
        functools.partial(_combine_kernel, final=final),
        grid_spec=grid_spec,
        out_shape=out_shape,
        input_output_aliases=aliases,
        compiler_params=_cparams(("arbitrary", "arbitrary")),
    )(seg_start, seg_len, seg_dst, xa, ri, mods, final_g.reshape(1, -1), ys)


def _route_tables(meta, cnt):
    ntiles = meta.shape[0]
    m = meta[:, :, 0].reshape(ntiles, 3, N_EXPERTS)
    seg_start, seg_len, before = m[:, 0], m[:, 1], m[:, 2]
    totals = cnt[:, 0].astype(I32)
    region = ((totals + TMF - 1) // TMF) * TMF
    ends = jnp.cumsum(region)
    seg_dst = (ends - region)[None, :] + before
    seg_start = jnp.concatenate([seg_start, jnp.zeros((1, N_EXPERTS), I32)], axis=0)
    seg_len = jnp.concatenate([seg_len, (region - totals)[None, :]], axis=0)
    seg_dst = jnp.concatenate([seg_dst, (ends - region + totals)[None, :]], axis=0)
    max_rows = 2 * ntiles * TM + ntiles * N_EXPERTS * (SEG_ALIGN - 1)
    n_tiles = max_rows // TMF + 1 + N_EXPERTS
    tile_id = jnp.arange(n_tiles, dtype=I32)
    starts = tile_id * TMF
    tile_e = jnp.minimum(jnp.sum(starts[:, None] >= ends[None, :], axis=1), N_EXPERTS - 1).astype(I32)
    valid = starts < ends[-1]
    last_used = jnp.maximum(ends[-1] // TMF - 1, 0)
    tile_e = jnp.where(valid, tile_e, tile_e[last_used])
    tile_src = jnp.minimum(tile_id, last_used)
    flat = lambda v: v.reshape(-1).astype(I32)
    seg_dst = jnp.concatenate([flat(seg_dst), (ends[-1:] // TMF).astype(I32)])
    return flat(seg_start), flat(seg_len), seg_dst, tile_e, valid.astype(I32), tile_src, n_tiles * TMF


def _rope_tables(n_lat):
    quarter = HEAD_DIM // 4
    inv = ROPE_BASE ** (-jnp.arange(quarter, dtype=F32) / quarter)
    n_rows = n_lat // GRID_W
    rows = jnp.repeat(jnp.arange(n_rows, dtype=F32), GRID_W)
    cols = jnp.tile(jnp.arange(GRID_W, dtype=F32), n_rows)
    ang = jnp.concatenate([rows[:, None] * inv, cols[:, None] * inv], axis=-1)
    cos = jnp.cos(ang)
    sin = jnp.sin(ang)
    cos_t = jnp.tile(cos, (1, LANES // cos.shape[1]))
    sin_t = jnp.tile(jnp.concatenate([-sin, sin], axis=-1), (1, LANES // HEAD_DIM))
    cos_t = jnp.concatenate([jnp.ones((CTX_LEN, LANES), F32), cos_t], axis=0)
    sin_t = jnp.concatenate([jnp.zeros((CTX_LEN, LANES), F32), sin_t], axis=0)
    return cos_t, sin_t


def kernel(x, c, ctx, c_ctx, w_ada, b_ada, norm_mix_g, norm_ffn_g, w_in, diff_lambda, diff_subln_g, swa_sink,
           ret_decay, ret_gn_g, w_branch, w_out, router_w, router_b, w_exp_gate, w_exp_up, w_exp_down, final_g):
    b, n_lat, _ = x.shape
    depth = w_ada.shape[0]
    assert ctx.shape[1] == CTX_LEN == TM and n_lat % TM == 0 and n_lat >= 2 * TM

    pad_rows = (-(b + 1)) % 8
    c_all = jnp.concatenate([c, c_ctx[None, :], jnp.zeros((pad_rows, D_MODEL), F32)], axis=0)
    mods_all = _ada(c_all, w_ada, b_ada)
    cos_t, sin_t = _rope_tables(n_lat)
    xa = x
    rw_t = router_w.T.astype(BF16)
    rb_col = router_b.reshape(-1, 1).astype(F32)

    for l in range(depth):
        last = l == depth - 1
        t0 = 1 if last else 0
        lam_init = 0.8 - 0.6 * math.exp(-0.3 * l)
        mods = mods_all[l].reshape(mods_all.shape[1], 1, -1)
        ctx_in = ctx if l == 0 else None
        p = _proj(ctx_in, xa, mods, norm_mix_g[l], cos_t, sin_t, w_in[l].astype(BF16), ctx_kv_only=last)
        oa = _diff(p, diff_lambda[l], diff_subln_g[l], lam_init, t0)
        ob = _win(p, swa_sink[l], with_ctx=not last)
        oc = _ret(p, ret_decay[l])
        xa, hp, ri, rwt, meta, cnt = _merge(ctx_in, xa, p, oa, ob, oc, mods, ret_gn_g[l], norm_ffn_g[l],
                                            w_branch[l].astype(BF16), w_out[l].astype(BF16), rw_t, rb_col, t0)
        seg_start, seg_len, seg_dst, tile_e, tile_v, tile_src, n_rows = _route_tables(meta, cnt)
        xs = _dispatch(hp, ri, rwt, seg_start, seg_len, seg_dst, n_rows)
        ys = _ffn(xs, tile_e, tile_v, tile_src, w_exp_gate, w_exp_up, w_exp_down, l)
        xa = _combine(xa, ys, ri, seg_start, seg_len, seg_dst, mods, final_g, t0, last)
    return xa
```

```python
import functools
import math

import jax
import jax.numpy as jnp
from jax import lax
from jax.experimental import pallas as pl
from jax.experimental.pallas import tpu as pltpu

F32 = jnp.float32
BF16 = jnp.bfloat16
U32 = jnp.uint32
I32 = jnp.int32

D_MODEL = 1024
CTX_LEN = 256
GRID_W = 64
ROPE_BASE = 10000.0
EPS = 1e-6
LOG2E = 1.4426950408889634
HEAD_DIM = 64
WINDOW = 128
RT_CHUNK = 128
N_EXPERTS = 16
N_GROUPS = 4
D_EXPERT = 512
LANES = 128
TM = 256
DIFF_HEADS = 4
QK_AHEAD = 2
TMF = 512
SEG_ALIGN = 8
SEG_SIZES = (256, 128, 64, 32, 16, 8)
SEG_BIG = 64
SORT_ROWS = 640
XS_COLS = D_MODEL // 2 + 128
D_IN = 6912
N_SRC_BLK = D_IN // LANES
N_OUT_BLK = N_SRC_BLK + 2
VMEM_LIMIT = 56 * 1024 * 1024

BLK_AQ, BLK_AK, BLK_AV = 0, 4, 8
BLK_BQ, BLK_BK, BLK_BV = 12, 16, 18
BLK_CQ, BLK_CK, BLK_CV, BLK_CG, BLK_GATE = 20, 22, 24, 28, 32


def _cparams(sem):
    return pltpu.CompilerParams(dimension_semantics=sem, vmem_limit_bytes=VMEM_LIMIT)


def _ada_kernel(c_ref, w_ref, b_ref, o_ref):
    c = c_ref[...]
    s = (c * jax.nn.sigmoid(c)).astype(BF16)
    o_ref[0] = jnp.dot(s, w_ref[0].astype(BF16), preferred_element_type=F32) + b_ref[0]


def _ada(c_all, w_ada, b_ada):
    depth = w_ada.shape[0]
    rows = c_all.shape[0]
    nj = w_ada.shape[2] // D_MODEL
    return pl.pallas_call(
        _ada_kernel,
        grid=(depth, nj),
        in_specs=[
            pl.BlockSpec((rows, D_MODEL), lambda l, j: (0, 0)),
            pl.BlockSpec((1, D_MODEL, D_MODEL), lambda l, j: (l, 0, j)),
            pl.BlockSpec((1, 1, D_MODEL), lambda l, j: (l, 0, j)),
        ],
        out_specs=pl.BlockSpec((1, rows, D_MODEL), lambda l, j: (l, 0, j)),
        out_shape=jax.ShapeDtypeStruct((depth, rows, w_ada.shape[2]), F32),
        compiler_params=_cparams(("arbitrary", "arbitrary")),
    )(c_all, w_ada, b_ada.reshape(depth, 1, -1))


def _rope(a, cos, sin_signed, first_half):
    rot = jnp.where(first_half, pltpu.roll(a, LANES - HEAD_DIM // 2, 1), pltpu.roll(a, HEAD_DIM // 2, 1))
    return a * cos + rot * sin_signed


def _token_tile(ctx_ref, x_ref, tile):
    if ctx_ref is None:
        return x_ref[0]
    return jnp.where(tile == 0, ctx_ref[0], x_ref[0])


def _token_specs(ctx, xa, t0):
    if ctx is None:
        return [xa], [pl.BlockSpec((1, TM, D_MODEL), lambda bi, t, *_: (bi, t + t0, 0))]
    return [ctx, xa], [pl.BlockSpec((1, TM, D_MODEL), lambda bi, t, *_: (bi, 0, 0)),
                       pl.BlockSpec((1, TM, D_MODEL), lambda bi, t, *_: (bi, jnp.maximum(t + t0 - 1, 0), 0))]


CTX_KV_CHUNKS = (2, 3, 4, 5, 8, 10, 11, 12)


def _proj_kernel(*refs, split, ctx_kv_only):
    ctx_ref, refs = (refs[0], refs[1:]) if split else (None, refs)
    x_ref, sh_ref, sc_ref, g_ref, cos_ref, sin_ref, w_ref, o_ref = refs
    t = pl.program_id(1)
    x = _token_tile(ctx_ref, x_ref, t)
    ms = jnp.mean(x * x, axis=-1, keepdims=True)
    h = x * lax.rsqrt(ms + EPS) * g_ref[...]
    h = h * (1.0 + sc_ref[0]) + sh_ref[0]
    hb = h.astype(BF16)
    cos = cos_ref[...]
    sin = sin_ref[...]
    lane = lax.broadcasted_iota(I32, (1, LANES), 1)
    first_half = (lane & (HEAD_DIM - 1)) < (HEAD_DIM // 2)
    low = lane < HEAD_DIM
    scale = HEAD_DIM ** -0.5

    def project(chunks):
        for c in range(N_SRC_BLK // 2):
            if c not in chunks:
                for src in (2 * c, 2 * c + 1):
                    dst = src if src < 16 else src + 2
                    o_ref[0, :, dst * LANES:(dst + 1) * LANES] = jnp.zeros((TM, LANES), BF16)
                continue
            acc = jnp.dot(hb, w_ref[:, c * 2 * LANES:(c + 1) * 2 * LANES], preferred_element_type=F32)
            for half in range(2):
                src = 2 * c + half
                a = acc[:, half * LANES:(half + 1) * LANES]
                if src < 8 or 12 <= src < 17:
                    a = _rope(a, cos, sin, first_half)
                if src < 4 or 12 <= src < 16:
                    a = a * (scale * LOG2E)
                if 20 <= src < 22:
                    a = a * scale
                if src >= 26:
                    a = a * 0.5
                if src in (16, 17):
                    sw = pltpu.roll(a, HEAD_DIM, 1)
                    dst = BLK_BK if src == 16 else BLK_BV
                    o_ref[0, :, dst * LANES:(dst + 1) * LANES] = jnp.where(low, a, sw).astype(BF16)
                    o_ref[0, :, (dst + 1) * LANES:(dst + 2) * LANES] = jnp.where(low, sw, a).astype(BF16)
                else:
                    dst = src if src < 16 else src + 2
                    o_ref[0, :, dst * LANES:(dst + 1) * LANES] = a.astype(BF16)

    all_chunks = tuple(range(N_SRC_BLK // 2))
    if ctx_kv_only:
        @pl.when(t == 0)
        def _():
            project(CTX_KV_CHUNKS)

        @pl.when(t > 0)
        def _():
            project(all_chunks)
    else:
        project(all_chunks)


def _proj(ctx, xa, mods, norm_g, cos_t, sin_t, w_in_bf, ctx_kv_only):
    b = xa.shape[0]
    ntok = xa.shape[1] + (0 if ctx is None else CTX_LEN)
    nt = ntok // TM
    mod_row = lambda bi, t: jnp.where(t == 0, b, bi)
    tok_args, tok_specs = _token_specs(ctx, xa, 0)
    return pl.pallas_call(
        functools.partial(_proj_kernel, split=ctx is not None, ctx_kv_only=ctx_kv_only),
        grid=(b, nt),
        in_specs=tok_specs + [
            pl.BlockSpec((1, 1, D_MODEL), lambda bi, t: (mod_row(bi, t), 0, 0)),
            pl.BlockSpec((1, 1, D_MODEL), lambda bi, t: (mod_row(bi, t), 0, 1)),
            pl.BlockSpec((1, D_MODEL), lambda bi, t: (0, 0)),
            pl.BlockSpec((TM, LANES), lambda bi, t: (t, 0)),
            pl.BlockSpec((TM, LANES), lambda bi, t: (t, 0)),
            pl.BlockSpec((D_MODEL, D_IN), lambda bi, t: (0, 0)),
        ],
        out_specs=pl.BlockSpec((1, TM, N_OUT_BLK * LANES), lambda bi, t: (bi, t, 0)),
        out_shape=jax.ShapeDtypeStruct((b, ntok, N_OUT_BLK * LANES), BF16),
        compiler_params=_cparams(("arbitrary", "arbitrary")),
    )(*tok_args, mods, mods, norm_g.reshape(1, -1), cos_t, sin_t, w_in_bf)


def _diff_kernel(lam_ref, g_ref, q_ref, k_ref, v_ref, o_ref, *, lam_init, t0, ntok):
    t = pl.program_id(2) + t0
    lv = lam_ref[...]
    lam = (jnp.exp(jnp.sum(lv[0:1] * lv[1:2], axis=-1, keepdims=True))
           - jnp.exp(jnp.sum(lv[2:3] * lv[3:4], axis=-1, keepdims=True)) + lam_init)
    lane = lax.broadcasted_iota(I32, (1, LANES), 1)
    nt_dims = (((1,), (1,)), ((), ()))

    def attend(nk):
        def scores(hd):
            cols = slice(hd * LANES, (hd + 1) * LANES)
            q = q_ref[0, :, cols]
            zero = jnp.zeros_like(q)
            q0 = jnp.where(lane < HEAD_DIM, q, zero)
            q1 = jnp.where(lane >= HEAD_DIM, q, zero)
            k = k_ref[0, 0:nk, cols]
            return (lax.dot_general(q0, k, nt_dims, preferred_element_type=F32),
                    lax.dot_general(q1, k, nt_dims, preferred_element_type=F32))

        ones_blk = jnp.where(lax.broadcasted_iota(I32, (nk, LANES), 1) == 0, 1.0, 0.0).astype(BF16)
        queue = [scores(hd) for hd in range(min(QK_AHEAD, DIFF_HEADS))]
        for hd in range(DIFF_HEADS):
            cols = slice(hd * LANES, (hd + 1) * LANES)
            v1 = jnp.concatenate([v_ref[0, 0:nk, cols], ones_blk], axis=1)
            s0, s1 = queue.pop(0)
            if hd + QK_AHEAD < DIFF_HEADS:
                queue.append(scores(hd + QK_AHEAD))
            e0 = jnp.exp2(s0 - jnp.max(s0, axis=-1, keepdims=True)).astype(BF16)
            e1 = jnp.exp2(s1 - jnp.max(s1, axis=-1, keepdims=True)).astype(BF16)
            ov0 = jnp.dot(e0, v1, preferred_element_type=F32)
            ov1 = jnp.dot(e1, v1, preferred_element_type=F32)
            r0 = 1.0 / ov0[:, LANES:LANES + 1]
            r1 = lam / ov1[:, LANES:LANES + 1]
            o = ov0[:, 0:LANES] * r0 - ov1[:, 0:LANES] * r1
            o = o * lax.rsqrt(jnp.mean(o * o, axis=-1, keepdims=True) + EPS) * g_ref[...] * (1.0 - lam_init)
            o_ref[0, :, cols] = o.astype(BF16)

    if t0 == 0:
        @pl.when(t == 0)
        def _():
            attend(CTX_LEN)

        @pl.when(t > 0)
        def _():
            attend(ntok)
    else:
        attend(ntok)


def _diff(p, lam_vec, subln_g, lam_init, t0):
    b, ntok, _ = p.shape
    nt = ntok // TM - t0
    heads = 4
    hw = DIFF_HEADS * LANES
    return pl.pallas_call(
        functools.partial(_diff_kernel, lam_init=lam_init, t0=t0, ntok=ntok),
        grid=(b, heads // DIFF_HEADS, nt),
        in_specs=[
            pl.BlockSpec((4, HEAD_DIM), lambda bi, h, t: (0, 0)),
            pl.BlockSpec((1, LANES), lambda bi, h, t: (0, 0)),
            pl.BlockSpec((1, TM, hw), lambda bi, h, t: (bi, t + t0, BLK_AQ // DIFF_HEADS + h)),
            pl.BlockSpec((1, ntok, hw), lambda bi, h, t: (bi, 0, BLK_AK // DIFF_HEADS + h)),
            pl.BlockSpec((1, ntok, hw), lambda bi, h, t: (bi, 0, BLK_AV // DIFF_HEADS + h)),
        ],
        out_specs=pl.BlockSpec((1, TM, hw), lambda bi, h, t: (bi, t, h)),
        out_shape=jax.ShapeDtypeStruct((b, nt * TM, heads * LANES), BF16),
        compiler_params=_cparams(("arbitrary", "arbitrary", "arbitrary")),
    )(lam_vec, subln_g.reshape(1, -1), p, p, p)


def _win_kernel(sink_ref, *refs, nsub, ntok, ctx_queries):
    q_refs, (k_ref, v_ref), o_ref = refs[:nsub], refs[nsub:nsub + 2], refs[-1]
    lane = lax.broadcasted_iota(I32, (1, LANES), 1)
    low = lane < HEAD_DIM
    nt_dims = (((1,), (1,)), ((), ()))
    wk = 2 * TM
    nkeys = CTX_LEN if ctx_queries else CTX_LEN + wk
    n_heads = 8
    group = n_heads // 2
    ones_blk = jnp.where(lax.broadcasted_iota(I32, (nkeys, LANES), 1) == 0, 1.0, 0.0).astype(BF16)

    biases, keys, vals = [], [], []
    for sub in range(nsub):
        if ctx_queries:
            biases.append(None)
            window = lambda ref, cols: ref[0, :, cols]
        else:
            t = 1 + nsub * pl.program_id(1) + sub
            start = pl.multiple_of(jnp.clip(t * TM - WINDOW, CTX_LEN, ntok - wk), WINDOW)
            qpos = t * TM + lax.broadcasted_iota(I32, (TM, wk), 0)
            kpos = start + lax.broadcasted_iota(I32, (TM, wk), 1)
            in_band = jnp.abs(kpos - qpos) <= WINDOW
            biases.append(jnp.concatenate([jnp.zeros((TM, CTX_LEN), F32),
                                           jnp.where(in_band, 0.0, -jnp.inf).astype(F32)], axis=1))
            window = lambda ref, cols, start=start: jnp.concatenate(
                [ref[0, 0:CTX_LEN, cols], ref[0, pl.ds(start, wk), cols]], axis=0)
        keys.append([window(k_ref, slice(g * LANES, (g + 1) * LANES)) for g in range(2)])
        vals.append([jnp.concatenate([window(v_ref, slice(g * LANES, (g + 1) * LANES)), ones_blk], axis=1)
                     for g in range(2)])

    def scores(item):
        h, sub = item
        qb = q_refs[sub][0, :, (h // 2) * LANES:(h // 2 + 1) * LANES]
        qm = jnp.where(low if h % 2 == 0 else jnp.logical_not(low), qb, jnp.zeros_like(qb))
        return lax.dot_general(qm, keys[sub][h // group], nt_dims, preferred_element_type=F32)

    items = [(h, sub) for h in range(n_heads) for sub in range(nsub)]
    ahead = QK_AHEAD * nsub
    queue = [scores(it) for it in items[:ahead]]
    outs = {}
    for i, (h, sub) in enumerate(items):
        s = queue.pop(0)
        if biases[sub] is not None:
            s = s + biases[sub]
        if i + ahead < len(items):
            queue.append(scores(items[i + ahead]))
        sk = sink_ref[h] * LOG2E
        m = jnp.maximum(jnp.max(s, axis=-1, keepdims=True), sk)
        e = jnp.exp2(s - m).astype(BF16)
        ov = jnp.dot(e, vals[sub][h // group], preferred_element_type=F32)
        l = ov[:, LANES:LANES + 1] + jnp.exp2(sk - m)
        outs[h, sub] = ov[:, 0:LANES] * (1.0 / l)
        if h % 2 == 1:
            o_ref[0, sub * TM:(sub + 1) * TM, (h // 2) * LANES:(h // 2 + 1) * LANES] = (
                jnp.where(low, outs.pop((h - 1, sub)), outs.pop((h, sub))).astype(BF16))


WIN_TILES = 4


def _win(p, sink, with_ctx):
    b, ntok, _ = p.shape
    n_lat = ntok // TM - 1
    assert n_lat % WIN_TILES == 0
    steps = n_lat // WIN_TILES
    q_spec = lambda sub: pl.BlockSpec((1, TM, 4 * LANES),
                                      lambda bi, j: (bi, 1 + WIN_TILES * j + sub, BLK_BQ // 4))
    out = pl.pallas_call(
        functools.partial(_win_kernel, nsub=WIN_TILES, ntok=ntok, ctx_queries=False),
        grid=(b, steps),
        in_specs=[pl.BlockSpec(memory_space=pltpu.SMEM)] + [q_spec(sub) for sub in range(WIN_TILES)] + [
            pl.BlockSpec((1, ntok, 2 * LANES), lambda bi, j: (bi, 0, BLK_BK // 2)),
            pl.BlockSpec((1, ntok, 2 * LANES), lambda bi, j: (bi, 0, BLK_BV // 2)),
        ],
        out_specs=pl.BlockSpec((1, WIN_TILES * TM, 4 * LANES), lambda bi, j: (bi, j, 0)),
        out_shape=jax.ShapeDtypeStruct((b, n_lat * TM, 4 * LANES), BF16),
        compiler_params=_cparams(("arbitrary", "arbitrary")),
    )(sink, *([p] * WIN_TILES), p, p)
    if not with_ctx:
        return None, out
    out_ctx = pl.pallas_call(
        functools.partial(_win_kernel, nsub=1, ntok=ntok, ctx_queries=True),
        grid=(b, 1),
        in_specs=[
            pl.BlockSpec(memory_space=pltpu.SMEM),
            pl.BlockSpec((1, TM, 4 * LANES), lambda bi, j: (bi, 0, BLK_BQ // 4)),
            pl.BlockSpec((1, CTX_LEN, 2 * LANES), lambda bi, j: (bi, 0, BLK_BK // 2)),
            pl.BlockSpec((1, CTX_LEN, 2 * LANES), lambda bi, j: (bi, 0, BLK_BV // 2)),
        ],
        out_specs=pl.BlockSpec((1, TM, 4 * LANES), lambda bi, j: (bi, 0, 0)),
        out_shape=jax.ShapeDtypeStruct((b, TM, 4 * LANES), BF16),
        compiler_params=_cparams(("arbitrary", "arbitrary")),
    )(sink, p, p, p)
    return out_ctx, out


def _ret_kernel(dec_ref, q_ref, k_ref, v_ref, o_ref, ob_ref, *, ntok):
    T = RT_CHUNK
    nchunk = ntok // T
    nctx = CTX_LEN // T
    x = dec_ref[...]
    lg_all = -(jnp.maximum(-x, 0.0) + jnp.log1p(jnp.exp(-jnp.abs(x))))
    lane = lax.broadcasted_iota(I32, (1, LANES), 1)
    low = lane < HEAD_DIM
    row_low = lax.broadcasted_iota(I32, (LANES, 1), 0) < HEAD_DIM
    pi = lax.broadcasted_iota(I32, (T, T), 0).astype(F32)
    pj = lax.broadcasted_iota(I32, (T, T), 1).astype(F32)
    pcol = lax.broadcasted_iota(I32, (T, 1), 0).astype(F32)
    nt_dims = (((1,), (1,)), ((), ()))
    tn_dims = (((0,), (0,)), ((), ()))

    decay, qdp, kdp, cdp = [], [], [], []
    for direction in range(2):
        lgs = [lg_all[direction:direction + 1, h:h + 1] for h in range(4)]
        if direction == 0:
            dist = pi - pj
            qd = [jnp.exp((pcol + 1.0) * lg) for lg in lgs]
            kd = [jnp.exp((T - 1.0 - pcol) * lg) for lg in lgs]
        else:
            dist = pj - pi
            qd = [jnp.exp((T - pcol) * lg) for lg in lgs]
            kd = [jnp.exp(pcol * lg) for lg in lgs]
        decay.append([jnp.where(dist >= 0, jnp.exp(jnp.maximum(dist, 0.0) * lg), 0.0) for lg in lgs])
        qdp.append([jnp.where(low, qd[2 * pr], qd[2 * pr + 1]) for pr in range(2)])
        kdp.append([jnp.where(low, kd[2 * pr], kd[2 * pr + 1]) for pr in range(2)])
        cdp.append([jnp.where(row_low, jnp.exp(T * lgs[2 * pr]), jnp.exp(T * lgs[2 * pr + 1])) for pr in range(2)])

    def step(i, states):
        chunks = (i, jnp.where(i < nctx, nctx - 1 - i, nchunk + nctx - 1 - i))
        halves = (low, jnp.logical_not(low))
        work = []
        for direction in range(2):
            r0 = pl.multiple_of(chunks[direction] * T, T)
            for pair in range(2):
                qb = q_ref[0, pl.ds(r0, T), pair * LANES:(pair + 1) * LANES]
                kb = k_ref[0, pl.ds(r0, T), pair * LANES:(pair + 1) * LANES]
                v2 = v_ref[0, pl.ds(r0, T), pair * 2 * LANES:(pair + 1) * 2 * LANES]
                kbd = (kb.astype(F32) * kdp[direction][pair]).astype(BF16)
                qsb = (qb.astype(F32) * qdp[direction][pair]).astype(BF16)
                zq = jnp.zeros_like(qb)
                q2 = jnp.concatenate([jnp.where(hm, qb, zq) for hm in halves], axis=0)
                inner2 = lax.dot_general(q2, kb, nt_dims, preferred_element_type=F32)
                kv2 = lax.dot_general(kbd, v2, tn_dims, preferred_element_type=F32)
                work.append((direction, pair, r0, inner2, kv2, qsb, v2))
        new_states = []
        for direction, pair, r0, inner2, kv2, qsb, v2 in work:
            out_ref = o_ref.at[0] if direction == 0 else ob_ref
            st = states[2 * direction + pair]
            stb = st.astype(BF16)
            zq = jnp.zeros_like(qsb)
            for sub in range(2):
                h = 2 * pair + sub
                inner = inner2[sub * T:(sub + 1) * T] * decay[direction][h]
                lhs = jnp.concatenate([inner.astype(BF16), jnp.where(halves[sub], qsb, zq)], axis=1)
                rhs = jnp.concatenate([v2[:, sub * LANES:(sub + 1) * LANES], stb], axis=0)
                out_ref[pl.ds(r0, T), h * LANES:(h + 1) * LANES] = jnp.dot(lhs, rhs, preferred_element_type=F32)
            new_states.append(st * cdp[direction][pair]
                              + jnp.where(row_low, kv2[:, 0:LANES], kv2[:, LANES:2 * LANES]))
        return tuple(new_states)

    zero = jnp.zeros((LANES, LANES), F32)
    lax.fori_loop(0, nchunk, step, (zero,) * 4, unroll=3)
    o_ref[0] = o_ref[0] + ob_ref[...]


def _ret(p, ret_decay):
    b, ntok, _ = p.shape
    return pl.pallas_call(
        functools.partial(_ret_kernel, ntok=ntok),
        grid=(b,),
        in_specs=[
            pl.BlockSpec((2, 4), lambda bi: (0, 0)),
            pl.BlockSpec((1, ntok, 2 * LANES), lambda bi: (bi, 0, BLK_CQ // 2)),
            pl.BlockSpec((1, ntok, 2 * LANES), lambda bi: (bi, 0, BLK_CK // 2)),
            pl.BlockSpec((1, ntok, 4 * LANES), lambda bi: (bi, 0, BLK_CV // 4)),
        ],
        out_specs=pl.BlockSpec((1, ntok, 4 * LANES), lambda bi: (bi, 0, 0)),
        out_shape=jax.ShapeDtypeStruct((b, ntok, 4 * LANES), F32),
        scratch_shapes=[pltpu.VMEM((ntok, 4 * LANES), F32)],
        compiler_params=_cparams(("arbitrary",)),
    )(ret_decay, p, p, p)


def _pack_rows(h, is_bf16_valued=False):
    half = h.shape[1] // 2
    rnd = (lambda v: v) if is_bf16_valued else (lambda v: v.astype(BF16).astype(F32))
    hi = lax.bitcast_convert_type(rnd(h[:, :half]), U32)
    lo = lax.bitcast_convert_type(rnd(h[:, half:]), U32)
    return hi | (lo >> 16)


def _unpack_rows(p):
    hi = lax.bitcast_convert_type(p & jnp.uint32(0xFFFF0000), F32)
    lo = lax.bitcast_convert_type(p << 16, F32)
    return hi, lo


def _merge_kernel(*refs, split, ob_split, t0):
    ctx_ref, refs = (refs[0], refs[1:]) if split else (None, refs)
    obc_ref, refs = (refs[0], refs[1:]) if ob_split else (None, refs)
    (x_ref, oa_ref, ob_ref, oc_ref, cg_ref, ga_ref, gb_ref, gc_ref, g1_ref, sh2_ref, sc2_ref,
     gn_ref, nf_ref, wb_ref, wo_ref, rw_ref, rb_ref,
     xo_ref, hp_ref, ri_ref, rwt_ref, meta_ref, cnt_ref) = refs
    first = (pl.program_id(0) == 0) & (pl.program_id(1) == 0)

    @pl.when(first)
    def _():
        cnt_ref[...] = jnp.zeros_like(cnt_ref)

    da = jnp.dot(oa_ref[0], wb_ref[0], preferred_element_type=F32)
    db = jnp.dot(_token_tile(obc_ref, ob_ref, pl.program_id(1) + t0), wb_ref[1], preferred_element_type=F32)

    oc = oc_ref[0]
    cgv = cg_ref[0].astype(F32)
    gn = gn_ref[...]
    parts = []
    for h in range(4):
        o = oc[:, h * LANES:(h + 1) * LANES]
        mu = jnp.mean(o, axis=-1, keepdims=True)
        dlt = o - mu
        var = jnp.mean(dlt * dlt, axis=-1, keepdims=True)
        y = dlt * lax.rsqrt(var + EPS) * gn[:, h * LANES:(h + 1) * LANES]
        cgh = cgv[:, h * LANES:(h + 1) * LANES]
        parts.append((y * (cgh * (1.0 + jnp.tanh(cgh)))).astype(BF16))
    ocn = jnp.concatenate(parts, axis=1)

    dc = jnp.dot(ocn, wb_ref[2], preferred_element_type=F32)
    y = (1.0 + jnp.tanh(ga_ref[0].astype(F32))) * da
    y += (1.0 + jnp.tanh(gb_ref[0].astype(F32))) * db
    y += (1.0 + jnp.tanh(gc_ref[0].astype(F32))) * dc
    y2 = jnp.dot(y.astype(BF16), wo_ref[...], preferred_element_type=F32)
    x = _token_tile(ctx_ref, x_ref, pl.program_id(1) + t0) + (0.5 * g1_ref[0]) * y2
    xo_ref[0] = x

    ms = jnp.mean(x * x, axis=-1, keepdims=True)
    h2 = x * lax.rsqrt(ms + EPS) * nf_ref[...]
    h2 = h2 * (1.0 + sc2_ref[0]) + sh2_ref[0]
    h2b = h2.astype(BF16)
    hp_ref[...] = h2b

    nt_dims = (((1,), (1,)), ((), ()))
    logits = lax.dot_general(rw_ref[...], h2b, nt_dims, preferred_element_type=F32)
    sc = jax.nn.sigmoid(logits)
    bi = sc + rb_ref[...]
    ei_i = lax.broadcasted_iota(I32, (N_EXPERTS, 1), 0)
    ei = ei_i.astype(F32)
    epg = N_EXPERTS // N_GROUPS
    egroup = (ei_i >> 2).astype(F32)
    gsum = []
    for g in range(N_GROUPS):
        r = [bi[epg * g + j:epg * g + j + 1, :] for j in range(epg)]
        m = r[0] + r[1]
        for a in range(epg):
            for c in range(a + 1, epg):
                if (a, c) != (0, 1):
                    m = jnp.maximum(m, r[a] + r[c])
        gsum.append(m)
    best = jnp.zeros_like(gsum[0])
    bsc = gsum[0]
    for g in range(1, N_GROUPS):
        upd = gsum[g] > bsc
        best = jnp.where(upd, float(g), best)
        bsc = jnp.where(upd, gsum[g], bsc)
    neg = -jnp.inf
    masked = jnp.where(egroup == best, bi, neg)
    m1 = jnp.max(masked, axis=0, keepdims=True)
    i1 = jnp.min(jnp.where(masked == m1, ei, float(N_EXPERTS)), axis=0, keepdims=True)
    masked2 = jnp.where(ei == i1, neg, masked)
    m2 = jnp.max(masked2, axis=0, keepdims=True)
    i2 = jnp.min(jnp.where(masked2 == m2, ei, float(N_EXPERTS)), axis=0, keepdims=True)
    sel1 = ei == i1
    sel2 = ei == i2
    w1 = jnp.sum(jnp.where(sel1, sc, 0.0), axis=0, keepdims=True)
    w2 = jnp.sum(jnp.where(sel2, sc, 0.0), axis=0, keepdims=True)
    ws = w1 + w2
    rwt_ref[0, 0:1, :] = w1 / ws
    rwt_ref[0, 1:2, :] = w2 / ws

    oh = (sel1 | sel2).astype(F32)
    si = lax.broadcasted_iota(I32, (TM, TM), 0)
    ti = lax.broadcasted_iota(I32, (TM, TM), 1)
    upper = (si < ti).astype(BF16)
    prefix = jnp.dot(oh.astype(BF16), upper, preferred_element_type=F32)
    n_col = jnp.sum(oh, axis=1, keepdims=True)
    seg_len = jnp.floor((n_col + (SEG_ALIGN - 1.0)) * (1.0 / SEG_ALIGN)) * SEG_ALIGN
    seg_len_b = jnp.broadcast_to(seg_len, (N_EXPERTS, LANES))
    er = lax.broadcasted_iota(I32, (N_EXPERTS, N_EXPERTS), 0)
    ec = lax.broadcasted_iota(I32, (N_EXPERTS, N_EXPERTS), 1)
    lower = (ec < er).astype(BF16)
    seg_start_b = jnp.dot(lower, seg_len_b.astype(BF16), preferred_element_type=F32)
    local = prefix + seg_start_b[:, 0:1]
    pos1 = jnp.sum(jnp.where(sel1, local, 0.0), axis=0, keepdims=True)
    pos2 = jnp.sum(jnp.where(sel2, local, 0.0), axis=0, keepdims=True)
    ri_ref[0, 0:1, :] = i1.astype(I32)
    ri_ref[0, 1:2, :] = i2.astype(I32)
    ri_ref[0, 2:3, :] = pos1.astype(I32)
    ri_ref[0, 3:4, :] = pos2.astype(I32)
    meta_ref[0, 0:N_EXPERTS, :] = seg_start_b.astype(I32)
    meta_ref[0, N_EXPERTS:2 * N_EXPERTS, :] = seg_len_b.astype(I32)
    meta_ref[0, 2 * N_EXPERTS:, :] = cnt_ref[...].astype(I32)
    cnt_ref[...] = cnt_ref[...] + seg_len_b


def _merge(ctx, xa, p, oa, ob, oc, mods, gn_g, nf_g, wb_bf, wo_bf, rw_t, rb_col, t0):
    b, ntok, _ = p.shape
    nt = ntok // TM - t0
    ntiles = b * nt
    mod_row = lambda bi, t: jnp.where(t + t0 == 0, b, bi)
    tok = lambda bi, t: (bi, t + t0, 0)
    flat = lambda bi, t: (bi * nt + t, 0, 0)
    tok_args, tok_specs = _token_specs(ctx, xa, t0)
    ob_ctx, ob_lat = ob
    ctx_args, ctx_specs = [], []
    if ctx is not None:
        ctx_args, ctx_specs, tok_args, tok_specs = tok_args[:1], tok_specs[:1], tok_args[1:], tok_specs[1:]
    if ob_ctx is not None:
        ctx_args = ctx_args + [ob_ctx]
        ctx_specs = ctx_specs + [pl.BlockSpec((1, TM, 4 * LANES), lambda bi, t: (bi, 0, 0))]
    outs = pl.pallas_call(
        functools.partial(_merge_kernel, split=ctx is not None, ob_split=ob_ctx is not None, t0=t0),
        grid=(b, nt),
        in_specs=ctx_specs + tok_specs + [
            pl.BlockSpec((1, TM, 4 * LANES), lambda bi, t: (bi, t, 0)),
            pl.BlockSpec((1, TM, 4 * LANES), lambda bi, t: (bi, jnp.maximum(t + t0 - 1, 0), 0)),
            pl.BlockSpec((1, TM, 4 * LANES), tok),
            pl.BlockSpec((1, TM, 4 * LANES), lambda bi, t: (bi, t + t0, BLK_CG // 4)),
            pl.BlockSpec((1, TM, D_MODEL), lambda bi, t: (bi, t + t0, BLK_GATE // 8)),
            pl.BlockSpec((1, TM, D_MODEL), lambda bi, t: (bi, t + t0, BLK_GATE // 8 + 1)),
            pl.BlockSpec((1, TM, D_MODEL), lambda bi, t: (bi, t + t0, BLK_GATE // 8 + 2)),
            pl.BlockSpec((1, 1, D_MODEL), lambda bi, t: (mod_row(bi, t), 0, 2)),
            pl.BlockSpec((1, 1, D_MODEL), lambda bi, t: (mod_row(bi, t), 0, 3)),
            pl.BlockSpec((1, 1, D_MODEL), lambda bi, t: (mod_row(bi, t), 0, 4)),
            pl.BlockSpec((1, 4 * LANES), lambda bi, t: (0, 0)),
            pl.BlockSpec((1, D_MODEL), lambda bi, t: (0, 0)),
            pl.BlockSpec((3, 4 * LANES, D_MODEL), lambda bi, t: (0, 0, 0)),
            pl.BlockSpec((D_MODEL, D_MODEL), lambda bi, t: (0, 0)),
            pl.BlockSpec((N_EXPERTS, D_MODEL), lambda bi, t: (0, 0)),
            pl.BlockSpec((N_EXPERTS, 1), lambda bi, t: (0, 0)),
        ],
        out_specs=[
            pl.BlockSpec((1, TM, D_MODEL), tok),
            pl.BlockSpec((TM, D_MODEL), lambda bi, t: (bi * nt + t, 0)),
            pl.BlockSpec((1, 4, TM), flat),
            pl.BlockSpec((1, 2, TM), flat),
            pl.BlockSpec((1, 3 * N_EXPERTS, LANES), flat),
            pl.BlockSpec((N_EXPERTS, LANES), lambda bi, t: (0, 0)),
        ],
        out_shape=[
            jax.ShapeDtypeStruct((b, ntok, D_MODEL), F32),
            jax.ShapeDtypeStruct((ntiles * TM, D_MODEL), BF16),
            jax.ShapeDtypeStruct((ntiles, 4, TM), I32),
            jax.ShapeDtypeStruct((ntiles, 2, TM), F32),
            jax.ShapeDtypeStruct((ntiles, 3 * N_EXPERTS, LANES), I32),
            jax.ShapeDtypeStruct((N_EXPERTS, LANES), F32),
        ],
        input_output_aliases={} if ctx is not None else {len(ctx_args): 0},
        compiler_params=_cparams(("arbitrary", "arbitrary")),
    )(*ctx_args, *tok_args, oa, ob_lat, oc, p, p, p, p, mods, mods, mods, gn_g.reshape(1, -1), nf_g.reshape(1, -1),
      wb_bf, wo_bf, rw_t, rb_col)
    return outs


def _segment_copies(tile, start_s, len_s, dst_s, make_copy, wait, priority=lambda e: 0):
    for e in range(N_EXPERTS):
        n = len_s[tile * N_EXPERTS + e]
        a = start_s[tile * N_EXPERTS + e]
        d = dst_s[tile * N_EXPERTS + e]
        def chunks(sizes, n=n, a=a, d=d, e=e):
            for size in sizes:
                @pl.when((n & size) != 0)
                def _(size=size):
                    off = n & (-2 * size)
                    cp = make_copy(pl.multiple_of(a + off, SEG_ALIGN), pl.multiple_of(d + off, SEG_ALIGN), size)
                    if wait:
                        cp.wait()
                    else:
                        cp.start(priority=priority(e))

        big = tuple(s for s in SEG_SIZES if s >= SEG_BIG)
        pl.when(n >= SEG_BIG)(functools.partial(chunks, big))
        chunks(tuple(s for s in SEG_SIZES if s < SEG_BIG))


def GATHER_QUEUE(e):
    del e
    return 1


def _wait_rows(total, make_copy):
    for size in (2 * SEG_SIZES[0],) + SEG_SIZES:
        @pl.when((total & size) != 0)
        def _(size=size):
            make_copy(size).wait()


def _sort_matrix(ri_ref):
    srow = lax.broadcasted_iota(I32, (SORT_ROWS, TM), 0)
    return srow == ri_ref[0, 2:3, :], srow == ri_ref[0, 3:4, :]


def _dispatch_kernel(start_s, len_s, dst_s, h_ref, ri_ref, rw_ref, xs_ref, sb, zb, sems):
    i = pl.program_id(0)
    last = pl.num_programs(0) - 1
    slot = lax.rem(i, 2)

    hb = h_ref[...]
    m1, m2 = _sort_matrix(ri_ref)
    perm = jnp.where(m1 | m2, 1.0, 0.0).astype(BF16)
    rows = jnp.dot(perm, hb, preferred_element_type=F32)
    w = jnp.sum(jnp.where(m1, rw_ref[0, 0:1, :], 0.0) + jnp.where(m2, rw_ref[0, 1:2, :], 0.0),
                axis=1, keepdims=True)
    sb[slot, :, 0:D_MODEL // 2] = _pack_rows(rows, is_bf16_valued=True)
    sb[slot, :, D_MODEL // 2:] = lax.bitcast_convert_type(jnp.broadcast_to(w, (SORT_ROWS, LANES)), U32)

    def copy_for(slot_k):
        def make_copy(local_row, global_row, size):
            return pltpu.make_async_copy(sb.at[slot_k, pl.ds(local_row, size)],
                                         xs_ref.at[pl.ds(global_row, size)], sems.at[slot_k])
        return make_copy

    def tile_rows(tile):
        j = tile * N_EXPERTS + N_EXPERTS - 1
        return start_s[j] + len_s[j]

    for k in range(2):
        @pl.when(slot == k)
        def _(k=k):
            _segment_copies(i, start_s, len_s, dst_s, copy_for(k), wait=False, priority=lambda e: e % 2)

            @pl.when(i > 0)
            def _():
                _wait_rows(tile_rows(i - 1), lambda size: copy_for(1 - k)(0, 0, size))

            @pl.when(i == last)
            def _():
                _wait_rows(tile_rows(i), lambda size: copy_for(k)(0, 0, size))

    @pl.when(i == last)
    def _():
        zb[...] = jnp.zeros_like(zb)

        def tail_copy(local_row, global_row, size):
            del local_row
            return pltpu.make_async_copy(zb.at[pl.ds(0, size)], xs_ref.at[pl.ds(global_row, size)], sems.at[0])

        _segment_copies(last + 1, start_s, len_s, dst_s, tail_copy, wait=False)
        _segment_copies(last + 1, start_s, len_s, dst_s, tail_copy, wait=True)

        def fill(j, carry):
            cp = pltpu.make_async_copy(zb, xs_ref.at[pl.ds(pl.multiple_of(j * TMF, TMF), TMF)], sems.at[1])
            cp.start()
            cp.wait()
            return carry

        lax.fori_loop(dst_s[(last + 2) * N_EXPERTS], xs_ref.shape[0] // TMF, fill, 0)


def _dispatch(hp, ri, rwt, seg_start, seg_len, seg_dst, n_rows):
    ntiles = ri.shape[0]
    grid_spec = pltpu.PrefetchScalarGridSpec(
        num_scalar_prefetch=3,
        grid=(ntiles,),
        in_specs=[
            pl.BlockSpec((TM, D_MODEL), lambda i, *_: (i, 0)),
            pl.BlockSpec((1, 4, TM), lambda i, *_: (i, 0, 0)),
            pl.BlockSpec((1, 2, TM), lambda i, *_: (i, 0, 0)),
        ],
        out_specs=pl.BlockSpec(memory_space=pl.ANY),
        scratch_shapes=[pltpu.VMEM((2, SORT_ROWS, XS_COLS), U32), pltpu.VMEM((TMF, XS_COLS), U32),
                        pltpu.SemaphoreType.DMA((2,))],
    )
    return pl.pallas_call(
        _dispatch_kernel,
        grid_spec=grid_spec,
        out_shape=jax.ShapeDtypeStruct((n_rows, XS_COLS), U32),
        compiler_params=_cparams(("arbitrary",)),
    )(seg_start, seg_len, seg_dst, hp, ri, rwt)


def _ffn_kernel(te_ref, tv_ref, ts_ref, xs_ref, wg_ref, wu_ref, wd_ref, ys_ref, wgb, wub, wdb):
    del ts_ref
    i = pl.program_id(0)
    prev = te_ref[jnp.maximum(i - 1, 0)]
    fresh = (i == 0) | (te_ref[i] != prev)

    @pl.when(fresh)
    def _():
        wgb[...] = wg_ref[0, 0].astype(BF16)
        wub[...] = wu_ref[0, 0].astype(BF16)
        wdb[...] = wd_ref[0, 0].astype(BF16)

    @pl.when(tv_ref[i] > 0)
    def _():
        hi, lo = _unpack_rows(xs_ref[:, 0:D_MODEL // 2])
        xb = jnp.concatenate([hi.astype(BF16), lo.astype(BF16)], axis=1)
        w = lax.bitcast_convert_type(xs_ref[:, D_MODEL // 2:D_MODEL // 2 + 1], F32)
        g = jnp.dot(xb, wgb[...], preferred_element_type=F32)
        u = jnp.dot(xb, wub[...], preferred_element_type=F32)
        he = ((g * (1.0 + jnp.tanh(0.5 * g))) * (u * (0.5 * w))).astype(BF16)
        ys_ref[...] = _pack_rows(jnp.dot(he, wdb[...], preferred_element_type=F32))

    @pl.when(tv_ref[i] == 0)
    def _():
        ys_ref[...] = jnp.zeros_like(ys_ref)


def _ffn(xs, tile_e, tile_v, tile_src, wg, wu, wd, layer):
    n_rows = xs.shape[0]
    n_tiles = n_rows // TMF
    grid_spec = pltpu.PrefetchScalarGridSpec(
        num_scalar_prefetch=3,
        grid=(n_tiles,),
        in_specs=[
            pl.BlockSpec((TMF, XS_COLS), lambda i, te, tv, ts: (ts[i], 0)),
            pl.BlockSpec((1, 1, D_MODEL, D_EXPERT), lambda i, te, tv, ts: (layer, te[i], 0, 0)),
            pl.BlockSpec((1, 1, D_MODEL, D_EXPERT), lambda i, te, tv, ts: (layer, te[i], 0, 0)),
            pl.BlockSpec((1, 1, D_EXPERT, D_MODEL), lambda i, te, tv, ts: (layer, te[i], 0, 0)),
        ],
        out_specs=pl.BlockSpec((TMF, D_MODEL // 2), lambda i, te, tv, ts: (i, 0)),
        scratch_shapes=[
            pltpu.VMEM((D_MODEL, D_EXPERT), BF16),
            pltpu.VMEM((D_MODEL, D_EXPERT), BF16),
            pltpu.VMEM((D_EXPERT, D_MODEL), BF16),
        ],
    )
    return pl.pallas_call(
        _ffn_kernel,
        grid_spec=grid_spec,
        out_shape=jax.ShapeDtypeStruct((n_rows, D_MODEL // 2), U32),
        compiler_params=_cparams(("arbitrary",)),
    )(tile_e, tile_v, tile_src, xs, wg, wu, wd)


def _combine_kernel(start_s, len_s, dst_s, x_ref, ri_ref, g2_ref, fg_ref, ys_ref, o_ref, yb, sems, *, final):
    nt = pl.num_programs(1)
    i = pl.program_id(0) * nt + pl.program_id(1)
    last = pl.num_programs(0) * nt - 1
    slot = lax.rem(i, 2)

    def copy_for(slot_k):
        def make_copy(local_row, global_row, size):
            return pltpu.make_async_copy(ys_ref.at[pl.ds(global_row, size)],
                                         yb.at[slot_k, pl.ds(local_row, size)], sems.at[slot_k])
        return make_copy

    @pl.when(i == 0)
    def _():
        yb[...] = jnp.zeros_like(yb)
        _segment_copies(i, start_s, len_s, dst_s, copy_for(0), wait=False, priority=GATHER_QUEUE)

    for k in range(2):
        @pl.when(slot == k)
        def _(k=k):
            @pl.when(i < last)
            def _():
                _segment_copies(i + 1, start_s, len_s, dst_s, copy_for(1 - k), wait=False, priority=GATHER_QUEUE)

            j = i * N_EXPERTS + N_EXPERTS - 1
            _wait_rows(start_s[j] + len_s[j], lambda size: copy_for(k)(0, 0, size))

    hi, lo = _unpack_rows(yb[slot])
    m1, m2 = _sort_matrix(ri_ref)
    perm = jnp.where(m1 | m2, 1.0, 0.0).astype(BF16)
    tn_dims = (((0,), (0,)), ((), ()))
    moe = jnp.concatenate(
        [lax.dot_general(perm, hi.astype(BF16), tn_dims, preferred_element_type=F32),
         lax.dot_general(perm, lo.astype(BF16), tn_dims, preferred_element_type=F32)], axis=1)
    x = x_ref[0] + g2_ref[0] * moe
    if final:
        ms = jnp.mean(x * x, axis=-1, keepdims=True)
        x = x * lax.rsqrt(ms + EPS) * fg_ref[...]
    o_ref[0] = x


def _combine(xa, ys, ri, seg_start, seg_len, seg_dst, mods, final_g, t0, final):
    b, ntok, _ = xa.shape
    nt = ntok // TM - t0
    mod_row = lambda bi, t: jnp.where(t + t0 == 0, b, bi)
    if final:
        out_shape = jax.ShapeDtypeStruct((b, nt * TM, D_MODEL), F32)
        out_spec = pl.BlockSpec((1, TM, D_MODEL), lambda bi, t, *_: (bi, t, 0))
        aliases = {}
    else:
        out_shape = jax.ShapeDtypeStruct(xa.shape, F32)
        out_spec = pl.BlockSpec((1, TM, D_MODEL), lambda bi, t, *_: (bi, t + t0, 0))
        aliases = {3: 0}
    grid_spec = pltpu.PrefetchScalarGridSpec(
        num_scalar_prefetch=3,
        grid=(b, nt),
        in_specs=[
            pl.BlockSpec((1, TM, D_MODEL), lambda bi, t, *_: (bi, t + t0, 0)),
            pl.BlockSpec((1, 4, TM), lambda bi, t, *_: (bi * nt + t, 0, 0)),
            pl.BlockSpec((1, 1, D_MODEL), lambda bi, t, *_: (mod_row(bi, t), 0, 5)),
            pl.BlockSpec((1, D_MODEL), lambda bi, t, *_: (0, 0)),
            pl.BlockSpec(memory_space=pl.ANY),
        ],
        out_specs=out_spec,
        scratch_shapes=[pltpu.VMEM((2, SORT_ROWS, D_MODEL // 2), U32), pltpu.SemaphoreType.DMA((2,))],
    )
    return pl.pallas_call(
        functools.partial(_combine_kernel, final=final),
        grid_spec=grid_spec,
        out_shape=out_shape,
        input_output_aliases=aliases,
        compiler_params=_cparams(("arbitrary", "arbitrary")),
    )(seg_start, seg_len, seg_dst, xa, ri, mods, final_g.reshape(1, -1), ys)


def _route_tables(meta, cnt):
    ntiles = meta.shape[0]
    m = meta[:, :, 0].reshape(ntiles, 3, N_EXPERTS)
    seg_start, seg_len, before = m[:, 0], m[:, 1], m[:, 2]
    totals = cnt[:, 0].astype(I32)
    region = ((totals + TMF - 1) // TMF) * TMF
    ends = jnp.cumsum(region)
    seg_dst = (ends - region)[None, :] + before
    seg_start = jnp.concatenate([seg_start, jnp.zeros((1, N_EXPERTS), I32)], axis=0)
    seg_len = jnp.concatenate([seg_len, (region - totals)[None, :]], axis=0)
    seg_dst = jnp.concatenate([seg_dst, (ends - region + totals)[None, :]], axis=0)
    max_rows = 2 * ntiles * TM + ntiles * N_EXPERTS * (SEG_ALIGN - 1)
    n_tiles = max_rows // TMF + 1 + N_EXPERTS
    tile_id = jnp.arange(n_tiles, dtype=I32)
    starts = tile_id * TMF
    tile_e = jnp.minimum(jnp.sum(starts[:, None] >= ends[None, :], axis=1), N_EXPERTS - 1).astype(I32)
    valid = starts < ends[-1]
    last_used = jnp.maximum(ends[-1] // TMF - 1, 0)
    tile_e = jnp.where(valid, tile_e, tile_e[last_used])
    tile_src = jnp.minimum(tile_id, last_used)
    flat = lambda v: v.reshape(-1).astype(I32)
    seg_dst = jnp.concatenate([flat(seg_dst), (ends[-1:] // TMF).astype(I32)])
    return flat(seg_start), flat(seg_len), seg_dst, tile_e, valid.astype(I32), tile_src, n_tiles * TMF


def _rope_tables(n_lat):
    quarter = HEAD_DIM // 4
    inv = ROPE_BASE ** (-jnp.arange(quarter, dtype=F32) / quarter)
    n_rows = n_lat // GRID_W
    rows = jnp.repeat(jnp.arange(n_rows, dtype=F32), GRID_W)
    cols = jnp.tile(jnp.arange(GRID_W, dtype=F32), n_rows)
    ang = jnp.concatenate([rows[:, None] * inv, cols[:, None] * inv], axis=-1)
    cos = jnp.cos(ang)
    sin = jnp.sin(ang)
    cos_t = jnp.tile(cos, (1, LANES // cos.shape[1]))
    sin_t = jnp.tile(jnp.concatenate([-sin, sin], axis=-1), (1, LANES // HEAD_DIM))
    cos_t = jnp.concatenate([jnp.ones((CTX_LEN, LANES), F32), cos_t], axis=0)
    sin_t = jnp.concatenate([jnp.zeros((CTX_LEN, LANES), F32), sin_t], axis=0)
    return cos_t, sin_t


def kernel(x, c, ctx, c_ctx, w_ada, b_ada, norm_mix_g, norm_ffn_g, w_in, diff_lambda, diff_subln_g, swa_sink,
           ret_decay, ret_gn_g, w_branch, w_out, router_w, router_b, w_exp_gate, w_exp_up, w_exp_down, final_g):
    b, n_lat, _ = x.shape
    depth = w_ada.shape[0]
    assert ctx.shape[1] == CTX_LEN == TM and n_lat % TM == 0 and n_lat >= 2 * TM

    pad_rows = (-(b + 1)) % 8
    c_all = jnp.concatenate([c, c_ctx[None, :], jnp.zeros((pad_rows, D_MODEL), F32)], axis=0)
    mods_all = _ada(c_all, w_ada, b_ada)
    cos_t, sin_t = _rope_tables(n_lat)
    xa = x
    rw_t = router_w.T.astype(BF16)
    rb_col = router_b.reshape(-1, 1).astype(F32)

    for l in range(depth):
        last = l == depth - 1
        t0 = 1 if last else 0
        lam_init = 0.8 - 0.6 * math.exp(-0.3 * l)
        mods = mods_all[l].reshape(mods_all.shape[1], 1, -1)
        ctx_in = ctx if l == 0 else None
        p = _proj(ctx_in, xa, mods, norm_mix_g[l], cos_t, sin_t, w_in[l].astype(BF16), ctx_kv_only=last)
        oa = _diff(p, diff_lambda[l], diff_subln_g[l], lam_init, t0)
        ob = _win(p, swa_sink[l], with_ctx=not last)
        oc = _ret(p, ret_decay[l])
        xa, hp, ri, rwt, meta, cnt = _merge(ctx_in, xa, p, oa, ob, oc, mods, ret_gn_g[l], norm_ffn_g[l],
                                            w_branch[l].astype(BF16), w_out[l].astype(BF16), rw_t, rb_col, t0)
        seg_start, seg_len, seg_dst, tile_e, tile_v, tile_src, n_rows = _route_tables(meta, cnt)
        xs = _dispatch(hp, ri, rwt, seg_start, seg_len, seg_dst, n_rows)
        ys = _ffn(xs, tile_e, tile_v, tile_src, w_exp_gate, w_exp_up, w_exp_down, l)
        xa = _combine(xa, ys, ri, seg_start, seg_len, seg_dst, mods, final_g, t0, last)
    return xa
```
